```python
import math
import jax
import jax.numpy as jnp
from jax import lax
import numpy as np

D_MODEL = 4096
BATCH = 1
SEQ = 8192
DEPTH = 1

D_MIX = D_MODEL
D_RWKV = D_MIX // 2
D_DIFF = D_MIX - D_RWKV
RWKV_HEAD = 64
N_RWKV_HEADS = D_RWKV // RWKV_HEAD
DECAY_RANK = 96
ICLR_RANK = 96
GATE_RANK = 256
DIFF_D = 64
DIFF_HEAD = 2 * DIFF_D
N_DIFF_HEADS = D_DIFF // DIFF_HEAD
D_FF = 256 * ((8 * D_MODEL // 3 + 255) // 256)
Q_BLOCK = 128
ALIBI_MAX_EXP = 8.0
NORM_EPS = 1e-6
RWKV_GN_EPS = 64e-5
SUBLN_EPS = 1e-5
FFN_RESIDUAL = 0.5

RWKV_SPLITS = (D_RWKV, 2 * D_RWKV, 3 * D_RWKV, 3 * D_RWKV + DECAY_RANK, 3 * D_RWKV + DECAY_RANK + ICLR_RANK)
RWKV_COLS = 3 * D_RWKV + DECAY_RANK + ICLR_RANK + GATE_RANK
DIFF_COLS = 3 * D_DIFF
D_IN = RWKV_COLS + DIFF_COLS

kernel_name = "hybrid_rwkv7_diffattn_macaron_block"


def rms_norm(x, g, eps=NORM_EPS):
    xf = x.astype(jnp.float32)
    y = xf * lax.rsqrt(jnp.mean(xf * xf, axis=-1, keepdims=True) + eps)
    return (y * g.astype(jnp.float32)).astype(x.dtype)


def swiglu(x, w_gate, w_up, w_down):
    return (jax.nn.silu(x @ w_gate) * (x @ w_up)) @ w_down


def centred_token_shift(p, mu_prev, mu_next):
    zero = jnp.zeros_like(p[:, :1])
    p_prev = jnp.concatenate([zero, p[:, :-1]], axis=1)
    p_next = jnp.concatenate([p[:, 1:], zero], axis=1)
    return p + mu_prev * (p_prev - p) + mu_next * (p_next - p)


def wkv7_scan(r, decay, k, v, kk, b, reverse):
    bsz, _, h, n = r.shape

    def step(S, inp):
        r_t, w_t, k_t, v_t, kk_t, b_t = inp
        sa = jnp.einsum('bhvk,bhk->bhv', S, -kk_t)
        S = S * w_t[:, :, None, :] + sa[..., :, None] * b_t[..., None, :] + v_t[..., :, None] * k_t[..., None, :]
        y = jnp.einsum('bhvk,bhk->bhv', S, r_t)
        return S, y

    xs = tuple(jnp.swapaxes(a.astype(jnp.float32), 0, 1) for a in (r, decay, k, v, kk, b))
    S0 = jnp.zeros((bsz, h, n, n), jnp.float32)
    _, ys = lax.scan(step, S0, xs, reverse=reverse)
    return jnp.swapaxes(ys, 0, 1)


def rwkv7_bidir(p, mu_prev, mu_next, w0_f, w2_f, w0_b, w2_b, a0_f, a2_f, a0_b, a2_b,
                g2, k_k, k_a, r_k, gn_w, gn_b):
    bsz, t, _ = p.shape
    H, N = N_RWKV_HEADS, RWKV_HEAD
    heads = lambda a: a.reshape(bsz, t, H, N)
    p = centred_token_shift(p, mu_prev, mu_next)
    r, k, v, xw, xa, xg = jnp.split(p, list(RWKV_SPLITS), axis=-1)
    hw = jnp.tanh(xw)
    g = jax.nn.sigmoid(xg) @ g2
    kk = heads((k * k_k).astype(jnp.float32))
    kk = kk / jnp.maximum(jnp.linalg.norm(kk, axis=-1, keepdims=True), 1e-12)
    r_h = heads(r).astype(jnp.float32)
    v_h = heads(v).astype(jnp.float32)
    k_h = heads(k).astype(jnp.float32)
    k_a_h = k_a.astype(jnp.float32).reshape(H, N)
    y = jnp.zeros((bsz, t, H, N), jnp.float32)
    k_sum = jnp.zeros((bsz, t, H, N), jnp.float32)
    for w0, w2, a0, a2, rev in ((w0_f, w2_f, a0_f, a2_f, False), (w0_b, w2_b, a0_b, a2_b, True)):
        w = -jax.nn.softplus(-(w0 + hw @ w2)) - 0.5
        decay = jnp.exp(-jnp.exp(w.astype(jnp.float32)))
        a = heads(jax.nn.sigmoid(a0 + xa @ a2).astype(jnp.float32))
        k_dir = k_h * (1.0 + (a - 1.0) * k_a_h)
        y = y + wkv7_scan(r_h, heads(decay), k_dir, v_h, kk, kk * a, rev)
        k_sum = k_sum + k_dir
    mean = jnp.mean(y, axis=-1, keepdims=True)
    var = jnp.mean(jnp.square(y - mean), axis=-1, keepdims=True)
    yn = ((y - mean) * lax.rsqrt(var + RWKV_GN_EPS)).reshape(bsz, t, D_RWKV) * gn_w.astype(jnp.float32) + gn_b.astype(jnp.float32)
    bonus = jnp.sum(r_h * k_sum * r_k.astype(jnp.float32), axis=-1, keepdims=True) * v_h
    out = (yn + bonus.reshape(bsz, t, D_RWKV)) * g.astype(jnp.float32)
    return out.astype(p.dtype)


def diff_attention_alibi(p, lq1, lk1, lq2, lk2, subln_g, lam_init):
    bsz, t, _ = p.shape
    H, d = N_DIFF_HEADS, DIFF_D
    q, k, v = jnp.split(p, 3, axis=-1)
    q = q.reshape(bsz, t, H, 2, d)
    k = k.reshape(bsz, t, H, 2, d)
    v = v.reshape(bsz, t, H, 2 * d)
    lam = (jnp.exp(jnp.sum(lq1.astype(jnp.float32) * lk1.astype(jnp.float32)))
           - jnp.exp(jnp.sum(lq2.astype(jnp.float32) * lk2.astype(jnp.float32))) + lam_init)
    slopes = 2.0 ** (-ALIBI_MAX_EXP * jnp.arange(1, H + 1, dtype=jnp.float32) / H)
    kpos = jnp.arange(t, dtype=jnp.float32)
    scale = d ** -0.5
    nblk = t // Q_BLOCK
    qb = jnp.transpose(q.reshape(bsz, nblk, Q_BLOCK, H, 2, d), (1, 0, 2, 3, 4, 5))

    def block(args):
        q_blk, i = args
        qpos = (i * Q_BLOCK + jnp.arange(Q_BLOCK)).astype(jnp.float32)
        s = jnp.einsum('bqhcd,bkhcd->bchqk', q_blk, k).astype(jnp.float32) * scale
        dist = jnp.abs(qpos[:, None] - kpos[None, :])
        s = s - slopes[:, None, None] * dist[None]
        prob = jax.nn.softmax(s, axis=-1)
        attn = prob[:, 0] - lam * prob[:, 1]
        return jnp.einsum('bhqk,bkhv->bqhv', attn.astype(v.dtype), v)

    ob = lax.map(block, (qb, jnp.arange(nblk)))
    o = jnp.transpose(ob, (1, 0, 2, 3, 4)).reshape(bsz, t, H, 2 * d)
    o = rms_norm(o, subln_g, SUBLN_EPS) * (1.0 - lam_init)
    return o.reshape(bsz, t, D_DIFF)


def setup_inputs(seed: int = 0) -> dict:
    key = jax.random.key(seed)
    ks = iter(jax.random.split(key, 64))
    L, D, F = DEPTH, D_MODEL, D_FF
    nrm = lambda shape, s: s * jax.random.normal(next(ks), shape, jnp.float32)
    gain = lambda n: 1.0 + nrm((L, n), 0.02)
    unif = lambda shape, lo, hi: jax.random.uniform(next(ks), shape, jnp.float32, lo, hi)
    return {
        "x": nrm((BATCH, SEQ, D), 1.0),
        "ffn1_pre_g": gain(D),
        "ffn1_w_gate": nrm((L, D, F), D ** -0.5),
        "ffn1_w_up": nrm((L, D, F), D ** -0.5),
        "ffn1_w_down": nrm((L, F, D), F ** -0.5),
        "ffn1_post_g": gain(D),
        "mix_pre_g": gain(D),
        "w_in": nrm((L, D, D_IN), D ** -0.5),
        "mu_prev": unif((L, RWKV_COLS), 0.0, 0.5),
        "mu_next": unif((L, RWKV_COLS), 0.0, 0.5),
        "w0_f": unif((L, D_RWKV), -7.0, -1.0),
        "w2_f": nrm((L, DECAY_RANK, D_RWKV), 0.1 * DECAY_RANK ** -0.5),
        "w0_b": unif((L, D_RWKV), -7.0, -1.0),
        "w2_b": nrm((L, DECAY_RANK, D_RWKV), 0.1 * DECAY_RANK ** -0.5),
        "a0_f": nrm((L, D_RWKV), 0.1),
        "a2_f": nrm((L, ICLR_RANK, D_RWKV), ICLR_RANK ** -0.5),
        "a0_b": nrm((L, D_RWKV), 0.1),
        "a2_b": nrm((L, ICLR_RANK, D_RWKV), ICLR_RANK ** -0.5),
        "g2": nrm((L, GATE_RANK, D_RWKV), GATE_RANK ** -0.5),
        "k_k": 0.85 + nrm((L, D_RWKV), 0.02),
        "k_a": 1.0 + nrm((L, D_RWKV), 0.02),
        "r_k": nrm((L, N_RWKV_HEADS, RWKV_HEAD), 0.1),
        "gn_w": gain(D_RWKV),
        "gn_b": nrm((L, D_RWKV), 0.02),
        "lq1": nrm((L, DIFF_D), 0.1),
        "lk1": nrm((L, DIFF_D), 0.1),
        "lq2": nrm((L, DIFF_D), 0.1),
        "lk2": nrm((L, DIFF_D), 0.1),
        "subln_g": gain(DIFF_HEAD),
        "w_out": nrm((L, D_MIX, D), D_MIX ** -0.5),
        "mix_post_g": gain(D),
        "ffn2_pre_g": gain(D),
        "ffn2_w_gate": nrm((L, D, F), D ** -0.5),
        "ffn2_w_up": nrm((L, D, F), D ** -0.5),
        "ffn2_w_down": nrm((L, F, D), F ** -0.5),
        "ffn2_post_g": gain(D),
        "final_g": gain(D),
    }


def reference(x, ffn1_pre_g, ffn1_w_gate, ffn1_w_up, ffn1_w_down, ffn1_post_g,
              mix_pre_g, w_in, mu_prev, mu_next, w0_f, w2_f, w0_b, w2_b,
              a0_f, a2_f, a0_b, a2_b, g2, k_k, k_a, r_k, gn_w, gn_b,
              lq1, lk1, lq2, lk2, subln_g, w_out, mix_post_g,
              ffn2_pre_g, ffn2_w_gate, ffn2_w_up, ffn2_w_down, ffn2_post_g, final_g):
    h = x
    for l in range(DEPTH):
        lam_init = 0.8 - 0.6 * math.exp(-0.3 * l)
        f1 = swiglu(rms_norm(h, ffn1_pre_g[l]), ffn1_w_gate[l], ffn1_w_up[l], ffn1_w_down[l])
        h = h + FFN_RESIDUAL * rms_norm(f1, ffn1_post_g[l])
        proj = rms_norm(h, mix_pre_g[l]) @ w_in[l]
        y_a = rwkv7_bidir(proj[..., :RWKV_COLS], mu_prev[l], mu_next[l], w0_f[l], w2_f[l], w0_b[l], w2_b[l],
                          a0_f[l], a2_f[l], a0_b[l], a2_b[l], g2[l], k_k[l], k_a[l], r_k[l], gn_w[l], gn_b[l])
        y_b = diff_attention_alibi(proj[..., RWKV_COLS:], lq1[l], lk1[l], lq2[l], lk2[l], subln_g[l], lam_init)
        mix = jnp.concatenate([y_a, y_b.astype(y_a.dtype)], axis=-1) @ w_out[l]
        h = h + rms_norm(mix, mix_post_g[l])
        f2 = swiglu(rms_norm(h, ffn2_pre_g[l]), ffn2_w_gate[l], ffn2_w_up[l], ffn2_w_down[l])
        h = h + FFN_RESIDUAL * rms_norm(f2, ffn2_post_g[l])
        h = rms_norm(h, final_g[l])
    return h
```

```python
import functools
import math

import jax
import jax.numpy as jnp
from jax import lax
from jax.experimental import pallas as pl
from jax.experimental.pallas import tpu as pltpu

D_MODEL = 4096
D_RWKV = 2048
D_DIFF = 2048
RWKV_HEAD = 64
DECAY_RANK = 96
ICLR_RANK = 96
GATE_RANK = 256
DIFF_D = 64
DIFF_HEAD = 128
N_DIFF_HEADS = 16
ALIBI_MAX_EXP = 8.0
NORM_EPS = 1e-6
RWKV_GN_EPS = 64e-5
SUBLN_EPS = 1e-5
FFN_RESIDUAL = 0.5

LANES = 128
SUBLANES = 8
VMEM_LIMIT = 56 * 1024 * 1024

RANK_PAD = 128
SMALL_COLS = 2 * RANK_PAD + GATE_RANK
CHUNK = 64
PAIR = 2 * RWKV_HEAD
HI = lax.Precision.HIGHEST


def _cparams(n_axes):
    return pltpu.CompilerParams(
        dimension_semantics=("arbitrary",) * n_axes, vmem_limit_bytes=VMEM_LIMIT)


def _dot(a, b, dims=(((1,), (0,)), ((), ())), precision=None):
    return lax.dot_general(a, b, dims, precision=precision,
                           preferred_element_type=jnp.float32)


_NT = (((1,), (1,)), ((), ()))
_TN = (((0,), (0,)), ((), ()))


def _rms(x, g, eps):
    return x * lax.rsqrt(jnp.mean(x * x, axis=-1, keepdims=True) + eps) * g


def _norm_kernel(x_ref, g_ref, o_ref):
    o_ref[...] = _rms(x_ref[...], g_ref[...], NORM_EPS).astype(o_ref.dtype)


def rms_norm_cast(x, g, out_dtype, tm=256):
    m, d = x.shape
    return pl.pallas_call(
        _norm_kernel,
        grid=(m // tm,),
        in_specs=[pl.BlockSpec((tm, d), lambda i: (i, 0)),
                  pl.BlockSpec((1, d), lambda i: (0, 0))],
        out_specs=pl.BlockSpec((tm, d), lambda i: (i, 0)),
        out_shape=jax.ShapeDtypeStruct((m, d), out_dtype),
        compiler_params=_cparams(1),
        name="rms_norm",
    )(x, g.reshape(1, d))


def _resid_norm_kernel(h_ref, f_ref, gp_ref, gn_ref, h_out_ref, n_out_ref, *, scale):
    h = h_ref[...] + scale * _rms(f_ref[...], gp_ref[...], NORM_EPS)
    h_out_ref[...] = h
    n_out_ref[...] = _rms(h, gn_ref[...], NORM_EPS).astype(n_out_ref.dtype)


def resid_norm(h, f, g_post, g_next, scale, out_dtype, tm=256):
    m, d = h.shape
    row = pl.BlockSpec((tm, d), lambda i: (i, 0))
    vec = pl.BlockSpec((1, d), lambda i: (0, 0))
    return pl.pallas_call(
        functools.partial(_resid_norm_kernel, scale=scale),
        grid=(m // tm,),
        in_specs=[row, row, vec, vec],
        out_specs=[row, row],
        out_shape=[jax.ShapeDtypeStruct((m, d), jnp.float32),
                   jax.ShapeDtypeStruct((m, d), out_dtype)],
        compiler_params=_cparams(1),
        name="resid_norm",
    )(h, f, g_post.reshape(1, d), g_next.reshape(1, d))


def _mm_kernel(x_ref, w_ref, o_ref):
    o_ref[...] = _dot(x_ref[...], w_ref[...]).astype(o_ref.dtype)


def matmul(x, w, out_dtype, tm, tn):
    m, k = x.shape
    _, n = w.shape
    return pl.pallas_call(
        _mm_kernel,
        grid=(n // tn, m // tm),
        in_specs=[pl.BlockSpec((tm, k), lambda j, i: (i, 0)),
                  pl.BlockSpec((k, tn), lambda j, i: (0, j))],
        out_specs=pl.BlockSpec((tm, tn), lambda j, i: (i, j)),
        out_shape=jax.ShapeDtypeStruct((m, n), out_dtype),
        compiler_params=_cparams(2),
        name="matmul",
    )(x, w)


def _gate_up_kernel(x_ref, wg_ref, wu_ref, o_ref):
    x = x_ref[...]
    g = _dot(x, wg_ref[...])
    u = _dot(x, wu_ref[...])
    o_ref[...] = (g * jax.nn.sigmoid(g) * u).astype(o_ref.dtype)


def gate_up(x, wg, wu, tm, tn):
    m, k = x.shape
    _, n = wg.shape
    wspec = pl.BlockSpec((k, tn), lambda j, i: (0, j))
    return pl.pallas_call(
        _gate_up_kernel,
        grid=(n // tn, m // tm),
        in_specs=[pl.BlockSpec((tm, k), lambda j, i: (i, 0)), wspec, wspec],
        out_specs=pl.BlockSpec((tm, tn), lambda j, i: (i, j)),
        out_shape=jax.ShapeDtypeStruct((m, n), jnp.bfloat16),
        compiler_params=_cparams(2),
        name="gate_up",
    )(x, wg, wu)


def swiglu_ffn(xn, wg, wu, wd, tm_gu, tn_gu, tm_d, tn_d):
    a = gate_up(xn, wg, wu, tm_gu, tn_gu)
    return matmul(a, wd, jnp.float32, tm_d, tn_d)


def _head_sum(x, ones_bd):
    parts = []
    for s in range(x.shape[1] // LANES):
        parts.append(_dot(x[:, s * LANES:(s + 1) * LANES], ones_bd, precision=HI))
    return jnp.concatenate(parts, axis=1)


def _token_shift(p_ref, pp_ref, pn_ref, mup_ref, mun_ref, first, last):
    p = p_ref[...]
    rows = p.shape[0]
    rid = lax.broadcasted_iota(jnp.int32, p.shape, 0)
    prev_row = jnp.where(first, 0.0, pp_ref[SUBLANES - 1:SUBLANES, :])
    next_row = jnp.where(last, 0.0, pn_ref[0:1, :])
    p_prev = jnp.where(rid == 0, prev_row, pltpu.roll(p, 1, 0))
    p_next = jnp.where(rid == rows - 1, next_row, pltpu.roll(p, rows - 1, 0))
    return p + mup_ref[...] * (p_prev - p) + mun_ref[...] * (p_next - p)


def _rwkv_prep_kernel(
        p_ref, pp_ref, pn_ref, s_ref, sp_ref, sn_ref,
        mup_ref, mun_ref, mups_ref, muns_ref,
        w0f_ref, w0b_ref, a0f_ref, a0b_ref, kk_ref, ka_ref, rk_ref,
        w2f_ref, w2b_ref, a2f_ref, a2b_ref, g2_ref,
        atf_ref, btf_ref, ktf_ref, rtf_ref, glf_ref,
        atb_ref, btb_ref, ktb_ref, rtb_ref, glb_ref,
        v_ref, gate_ref, bonus_ref):
    i = pl.program_id(0)
    first = i == 0
    last = i == pl.num_programs(0) - 1
    tm = p_ref.shape[0]
    n_chunks = tm // CHUNK

    p = _token_shift(p_ref, pp_ref, pn_ref, mup_ref, mun_ref, first, last)
    s = _token_shift(s_ref, sp_ref, sn_ref, mups_ref, muns_ref, first, last)
    r = p[:, 0:D_RWKV]
    k = p[:, D_RWKV:2 * D_RWKV]
    v = p[:, 2 * D_RWKV:3 * D_RWKV]
    hw = jnp.tanh(s[:, 0:RANK_PAD]).astype(jnp.bfloat16)
    xa = s[:, RANK_PAD:2 * RANK_PAD].astype(jnp.bfloat16)
    sg = jax.nn.sigmoid(s[:, 2 * RANK_PAD:]).astype(jnp.bfloat16)

    li = lax.broadcasted_iota(jnp.int32, (LANES, LANES), 0) // RWKV_HEAD
    lj = lax.broadcasted_iota(jnp.int32, (LANES, LANES), 1) // RWKV_HEAD
    ones_bd = (li == lj).astype(jnp.float32)

    kk = k * kk_ref[...]
    kk = kk / jnp.maximum(jnp.sqrt(_head_sum(kk * kk, ones_bd)), 1e-12)

    ti = lax.broadcasted_iota(jnp.int32, (tm, tm), 0)
    tj = lax.broadcasted_iota(jnp.int32, (tm, tm), 1)
    same_chunk = (ti // CHUNK) == (tj // CHUNK)

    k_sum = jnp.zeros_like(k)
    dirs = ((w0f_ref, w2f_ref, a0f_ref, a2f_ref, atf_ref, btf_ref, ktf_ref, rtf_ref, glf_ref, False),
            (w0b_ref, w2b_ref, a0b_ref, a2b_ref, atb_ref, btb_ref, ktb_ref, rtb_ref, glb_ref, True))
    for w0_ref, w2_ref, a0_ref, a2_ref, at_ref, bt_ref, kt_ref, rt_ref, gl_ref, rev in dirs:
        z = w0_ref[...] + _dot(hw, w2_ref[...])
        lw = -math.exp(-0.5) * jax.nn.sigmoid(z)
        a = jax.nn.sigmoid(a0_ref[...] + _dot(xa, a2_ref[...]))
        k_dir = k * (1.0 + (a - 1.0) * ka_ref[...])
        k_sum = k_sum + k_dir
        order = (tj >= ti) if rev else (tj <= ti)
        tri = jnp.where(same_chunk & order, 1.0, 0.0)
        c = _dot(tri, lw, precision=HI)
        e_neg = jnp.exp(-c)
        at_ref[...] = -kk * jnp.exp(c - lw)
        bt_ref[...] = kk * a * e_neg
        kt_ref[...] = k_dir * e_neg
        e_pos = jnp.exp(c)
        rt_ref[...] = r * e_pos
        for j in range(n_chunks):
            end = j * CHUNK if rev else (j + 1) * CHUNK - 1
            gl_ref[j] = e_pos[end:end + 1, :]

    v_ref[...] = v
    gate_ref[...] = _dot(sg, g2_ref[...])
    bonus_ref[...] = _head_sum(r * k_sum * rk_ref[...], ones_bd) * v


def rwkv_prep(proj, small, params, tm=128):
    t = proj.shape[0]
    nb = t // tm
    hb = tm // SUBLANES
    n_halo = t // SUBLANES
    wide = proj.shape[1]

    def row(c):
        return pl.BlockSpec((tm, c), lambda i: (i, 0))

    def halo_prev(c):
        return pl.BlockSpec((SUBLANES, c), lambda i: (jnp.maximum(i * hb - 1, 0), 0))

    def halo_next(c):
        return pl.BlockSpec((SUBLANES, c), lambda i: (jnp.minimum((i + 1) * hb, n_halo - 1), 0))

    def vec(c):
        return pl.BlockSpec((1, c), lambda i: (0, 0))

    def mat(r_, c):
        return pl.BlockSpec((r_, c), lambda i: (0, 0))

    gl_spec = pl.BlockSpec((tm // CHUNK, 1, D_RWKV), lambda i: (i, 0, 0))
    big = jax.ShapeDtypeStruct((t, D_RWKV), jnp.float32)
    gl = jax.ShapeDtypeStruct((t // CHUNK, 1, D_RWKV), jnp.float32)
    in_specs = [row(wide), halo_prev(wide), halo_next(wide),
                row(SMALL_COLS), halo_prev(SMALL_COLS), halo_next(SMALL_COLS),
                vec(wide), vec(wide), vec(SMALL_COLS), vec(SMALL_COLS)]
    in_specs += [vec(D_RWKV)] * 7
    in_specs += [mat(RANK_PAD, D_RWKV)] * 4 + [mat(GATE_RANK, D_RWKV)]
    dir_specs = [row(D_RWKV)] * 4 + [gl_spec]
    dir_shapes = [big] * 4 + [gl]
    return pl.pallas_call(
        _rwkv_prep_kernel,
        grid=(nb,),
        in_specs=in_specs,
        out_specs=dir_specs + dir_specs + [row(D_RWKV)] * 3,
        out_shape=dir_shapes + dir_shapes + [big] * 3,
        compiler_params=_cparams(1),
        name="rwkv_prep",
    )(proj, proj, proj, small, small, small, *params)


def _stack(x, lane_head):
    return jnp.concatenate([jnp.where(lane_head == 0, x, 0.0),
                            jnp.where(lane_head == 1, x, 0.0)], axis=0)


def _scan_chunk(at, bt, kt, rt, v, gl, s0, rev, prec):
    n2 = 2 * CHUNK
    lane_head = lax.broadcasted_iota(jnp.int32, (CHUNK, PAIR), 1) // RWKV_HEAD
    a2, b2, k2, r2, v2 = (_stack(x, lane_head) for x in (at, bt, kt, rt, v))
    ri = lax.broadcasted_iota(jnp.int32, (n2, n2), 0)
    ci = lax.broadcasted_iota(jnp.int32, (n2, n2), 1)
    same = (ri // CHUNK) == (ci // CHUNK)
    before = (ci > ri) if rev else (ci < ri)
    strict = same & before
    incl = same & (before | (ri == ci))

    ar = jnp.concatenate([a2, r2], axis=0)
    bk = jnp.concatenate([b2, k2], axis=0)
    g = _dot(ar, bk, _NT, prec)
    a_ab = jnp.where(strict, g[:n2, :n2], 0.0)
    a_ak = jnp.where(strict, g[:n2, n2:], 0.0)
    a_rb = jnp.where(incl, g[n2:, :n2], 0.0)
    a_rk = jnp.where(incl, g[n2:, n2:], 0.0)

    eye = jnp.where(ri == ci, 1.0, 0.0)
    x = a_ab
    tinv = eye + x
    for _ in range(int(math.log2(CHUNK)) - 1):
        x = _dot(x, x, precision=prec)
        tinv = tinv + _dot(tinv, x, precision=prec)

    rhs = jnp.concatenate([a2, _dot(a_ak, v2, precision=prec)], axis=1)
    wq = _dot(tinv, rhs, precision=prec)
    u = _dot(wq[:, :PAIR], s0, _NT, prec) + wq[:, PAIR:]
    uv = jnp.concatenate([u, v2], axis=0)
    y2 = _dot(r2, s0, _NT, prec) + _dot(jnp.concatenate([a_rb, a_rk], axis=1), uv, precision=prec)
    s1 = s0 * gl + _dot(uv, bk * gl, _TN, prec)
    return y2[:CHUNK] + y2[CHUNK:], s1


def _rwkv_scan_kernel(atf_ref, btf_ref, ktf_ref, rtf_ref, glf_ref, vf_ref,
                      atb_ref, btb_ref, ktb_ref, rtb_ref, glb_ref, vb_ref,
                      yf_ref, yb_ref, sf_ref, sb_ref, *, prec):
    @pl.when(pl.program_id(1) == 0)
    def _():
        sf_ref[...] = jnp.zeros_like(sf_ref)
        sb_ref[...] = jnp.zeros_like(sb_ref)

    yf, sf = _scan_chunk(atf_ref[...], btf_ref[...], ktf_ref[...], rtf_ref[...], vf_ref[...],
                         glf_ref[0], sf_ref[...], False, prec)
    yb, sb = _scan_chunk(atb_ref[...], btb_ref[...], ktb_ref[...], rtb_ref[...], vb_ref[...],
                         glb_ref[0], sb_ref[...], True, prec)
    yf_ref[...] = yf
    yb_ref[...] = yb
    sf_ref[...] = sf
    sb_ref[...] = sb


def rwkv_scan(fwd, bwd, v, prec=HI):
    t = v.shape[0]
    nc = t // CHUNK
    n_pairs = D_RWKV // PAIR
    f_blk = pl.BlockSpec((CHUNK, PAIR), lambda p, c: (c, p))
    b_blk = pl.BlockSpec((CHUNK, PAIR), lambda p, c: (nc - 1 - c, p))
    f_gl = pl.BlockSpec((1, 1, PAIR), lambda p, c: (c, 0, p))
    b_gl = pl.BlockSpec((1, 1, PAIR), lambda p, c: (nc - 1 - c, 0, p))
    out = jax.ShapeDtypeStruct((t, D_RWKV), jnp.float32)
    return pl.pallas_call(
        functools.partial(_rwkv_scan_kernel, prec=prec),
        grid=(n_pairs, nc),
        in_specs=[f_blk] * 4 + [f_gl, f_blk] + [b_blk] * 4 + [b_gl, b_blk],
        out_specs=[f_blk, b_blk],
        out_shape=[out, out],
        scratch_shapes=[pltpu.VMEM((PAIR, PAIR), jnp.float32)] * 2,
        compiler_params=_cparams(2),
        name="rwkv_scan",
    )(*fwd, v, *bwd, v)


def _rwkv_post_kernel(yf_ref, yb_ref, bonus_ref, gate_ref, gw_ref, gb_ref, o_ref):
    li = lax.broadcasted_iota(jnp.int32, (LANES, LANES), 0) // RWKV_HEAD
    lj = lax.broadcasted_iota(jnp.int32, (LANES, LANES), 1) // RWKV_HEAD
    mean_bd = jnp.where(li == lj, 1.0 / RWKV_HEAD, 0.0)
    y = yf_ref[...] + yb_ref[...]
    d = y - _head_sum(y, mean_bd)
    var = _head_sum(d * d, mean_bd)
    yn = d * lax.rsqrt(var + RWKV_GN_EPS) * gw_ref[...] + gb_ref[...]
    o_ref[...] = ((yn + bonus_ref[...]) * gate_ref[...]).astype(o_ref.dtype)


def rwkv_post(yf, yb, bonus, gate, gn_w, gn_b, tm=256):
    t = yf.shape[0]
    row = pl.BlockSpec((tm, D_RWKV), lambda i: (i, 0))
    vec = pl.BlockSpec((1, D_RWKV), lambda i: (0, 0))
    return pl.pallas_call(
        _rwkv_post_kernel,
        grid=(t // tm,),
        in_specs=[row, row, row, row, vec, vec],
        out_specs=row,
        out_shape=jax.ShapeDtypeStruct((t, D_RWKV), jnp.bfloat16),
        compiler_params=_cparams(1),
        name="rwkv_post",
    )(yf, yb, bonus, gate, gn_w.reshape(1, -1), gn_b.reshape(1, -1))


def _diff_attn_kernel(lam_ref, q_ref, k_ref, v_ref, g_ref, o_ref, *, tq, tk, lam_init):
    h = pl.program_id(0)
    qi = pl.program_id(1)
    t = k_ref.shape[0]
    lam = lam_ref[0]
    slope = lam_ref[1 + h]
    scale = DIFF_D ** -0.5

    q = q_ref[...]
    lane_map = lax.broadcasted_iota(jnp.int32, q.shape, 1) // DIFF_D
    zero = jnp.zeros_like(q)
    q2 = jnp.concatenate([jnp.where(lane_map == 0, q, zero),
                          jnp.where(lane_map == 1, q, zero)], axis=0)
    qpos = qi * tq + lax.broadcasted_iota(jnp.int32, (tq, tk), 0)
    krel = lax.broadcasted_iota(jnp.int32, (tq, tk), 1)

    def body(j, carry):
        m, l, acc = carry
        start = pl.multiple_of(j * tk, tk)
        kb = k_ref[pl.ds(start, tk), :]
        vb = v_ref[pl.ds(start, tk), :]
        dist = jnp.abs(qpos - (krel + j * tk)).astype(jnp.float32)
        bias = slope * dist
        s = _dot(q2, kb, _NT) * scale - jnp.concatenate([bias, bias], axis=0)
        m_new = jnp.maximum(m, jnp.max(s, axis=-1, keepdims=True))
        alpha = jnp.exp(m - m_new)
        p = jnp.exp(s - m_new)
        l_new = alpha * l + jnp.sum(p, axis=-1, keepdims=True)
        acc_new = alpha * acc + _dot(p.astype(vb.dtype), vb)
        return m_new, l_new, acc_new

    init = (jnp.full((2 * tq, 1), -jnp.inf, jnp.float32),
            jnp.zeros((2 * tq, 1), jnp.float32),
            jnp.zeros((2 * tq, DIFF_HEAD), jnp.float32))
    m, l, acc = lax.fori_loop(0, t // tk, body, init)
    o = acc / l
    o = o[:tq] - lam * o[tq:]
    o_ref[...] = (_rms(o, g_ref[...], SUBLN_EPS) * (1.0 - lam_init)).astype(o_ref.dtype)


def diff_attention(q, k, v, lam, subln_g, lam_init, tq=256, tk=512):
    t = q.shape[0]
    slopes = 2.0 ** (-ALIBI_MAX_EXP * jnp.arange(1, N_DIFF_HEADS + 1, dtype=jnp.float32) / N_DIFF_HEADS)
    lam = jnp.concatenate([lam, slopes])
    return pl.pallas_call(
        functools.partial(_diff_attn_kernel, tq=tq, tk=tk, lam_init=lam_init),
        grid=(N_DIFF_HEADS, t // tq),
        in_specs=[pl.BlockSpec(memory_space=pltpu.SMEM),
                  pl.BlockSpec((tq, DIFF_HEAD), lambda h, i: (i, h)),
                  pl.BlockSpec((t, DIFF_HEAD), lambda h, i: (0, h)),
                  pl.BlockSpec((t, DIFF_HEAD), lambda h, i: (0, h)),
                  pl.BlockSpec((1, DIFF_HEAD), lambda h, i: (0, 0))],
        out_specs=pl.BlockSpec((tq, DIFF_HEAD), lambda h, i: (i, h)),
        out_shape=jax.ShapeDtypeStruct((t, D_DIFF), jnp.bfloat16),
        compiler_params=_cparams(2),
        name="diff_attn",
    )(lam, q, k, v, subln_g.reshape(1, DIFF_HEAD))


def _pad_cols(w, to):
    return jnp.pad(w, ((0, 0), (0, to - w.shape[1])))


def _layer(h, l, lam_init, x_norm, prm):
    bf = jnp.bfloat16
    f32 = jnp.float32
    f1 = swiglu_ffn(x_norm, prm["ffn1_w_gate"][l].astype(bf), prm["ffn1_w_up"][l].astype(bf),
                    prm["ffn1_w_down"][l].astype(bf), 1024, 256, 512, 512)
    h, xn = resid_norm(h, f1, prm["ffn1_post_g"][l], prm["mix_pre_g"][l], FFN_RESIDUAL, bf)

    w_in = prm["w_in"][l]
    c_rkv = 3 * D_RWKV
    c_w = c_rkv + DECAY_RANK
    c_a = c_w + ICLR_RANK
    c_g = c_a + GATE_RANK
    w_small = jnp.concatenate([_pad_cols(w_in[:, c_rkv:c_w], RANK_PAD),
                               _pad_cols(w_in[:, c_w:c_a], RANK_PAD),
                               w_in[:, c_a:c_g]], axis=1)
    w_rwkv = jnp.concatenate([w_in[:, :c_rkv], w_small], axis=1).astype(bf)
    proj = matmul(xn, w_rwkv, f32, 1024, 512)
    qkv = matmul(xn, w_in[:, c_g:].astype(bf), bf, 1024, 512)

    def small_vec(a):
        return jnp.concatenate([_pad_cols(a[None, c_rkv:c_w], RANK_PAD),
                                _pad_cols(a[None, c_w:c_a], RANK_PAD),
                                a[None, c_a:c_g]], axis=1)

    def pad_rows(w):
        return jnp.pad(w, ((0, RANK_PAD - w.shape[0]), (0, 0))).astype(bf)

    mu_p, mu_n = prm["mu_prev"][l], prm["mu_next"][l]
    vecs = [prm[n][l].reshape(1, D_RWKV) for n in ("w0_f", "w0_b", "a0_f", "a0_b", "k_k", "k_a", "r_k")]
    mats = [pad_rows(prm["w2_f"][l]), pad_rows(prm["w2_b"][l]),
            pad_rows(prm["a2_f"][l]), pad_rows(prm["a2_b"][l]), prm["g2"][l].astype(bf)]
    prep = rwkv_prep(proj[:, :c_rkv], proj[:, c_rkv:],
                     [mu_p[None, :c_rkv], mu_n[None, :c_rkv], small_vec(mu_p), small_vec(mu_n)]
                     + vecs + mats)
    fwd, bwd, (v_r, gate, bonus) = prep[0:5], prep[5:10], prep[10:13]
    yf, yb = rwkv_scan(fwd, bwd, v_r)
    y_a = rwkv_post(yf, yb, bonus, gate, prm["gn_w"][l], prm["gn_b"][l])

    lam = (jnp.exp(jnp.sum(prm["lq1"][l] * prm["lk1"][l]))
           - jnp.exp(jnp.sum(prm["lq2"][l] * prm["lk2"][l])) + lam_init).reshape(1)
    y_b = diff_attention(qkv[:, :D_DIFF], qkv[:, D_DIFF:2 * D_DIFF], qkv[:, 2 * D_DIFF:],
                         lam, prm["subln_g"][l], lam_init)

    mix = matmul(jnp.concatenate([y_a, y_b], axis=1), prm["w_out"][l].astype(bf), f32, 1024, 512)
    h, xn = resid_norm(h, mix, prm["mix_post_g"][l], prm["ffn2_pre_g"][l], 1.0, bf)

    f2 = swiglu_ffn(xn, prm["ffn2_w_gate"][l].astype(bf), prm["ffn2_w_up"][l].astype(bf),
                    prm["ffn2_w_down"][l].astype(bf), 1024, 256, 512, 512)
    return resid_norm(h, f2, prm["ffn2_post_g"][l], prm["final_g"][l], FFN_RESIDUAL, f32)


def kernel(x, ffn1_pre_g, ffn1_w_gate, ffn1_w_up, ffn1_w_down, ffn1_post_g, mix_pre_g, w_in, mu_prev, mu_next, w0_f, w2_f, w0_b, w2_b, a0_f, a2_f, a0_b, a2_b, g2, k_k, k_a, r_k, gn_w, gn_b, lq1, lk1, lq2, lk2, subln_g, w_out, mix_post_g, ffn2_pre_g, ffn2_w_gate, ffn2_w_up, ffn2_w_down, ffn2_post_g, final_g):
    prm = dict(locals())
    bsz, t, d = x.shape
    depth = ffn1_pre_g.shape[0]
    outs = []
    for b in range(bsz):
        h = x[b]
        for l in range(depth):
            lam_init = 0.8 - 0.6 * math.exp(-0.3 * l)
            x_norm = rms_norm_cast(h, ffn1_pre_g[l], jnp.bfloat16)
            _, h = _layer(h, l, lam_init, x_norm, prm)
        outs.append(h)
    return jnp.stack(outs, axis=0)
```

```python
import functools
import math

import jax
import jax.numpy as jnp
from jax import lax
from jax.experimental import pallas as pl
from jax.experimental.pallas import tpu as pltpu

D_MODEL = 4096
D_RWKV = 2048
D_DIFF = 2048
RWKV_HEAD = 64
DECAY_RANK = 96
ICLR_RANK = 96
GATE_RANK = 256
DIFF_D = 64
DIFF_HEAD = 128
N_DIFF_HEADS = 16
ALIBI_MAX_EXP = 8.0
NORM_EPS = 1e-6
RWKV_GN_EPS = 64e-5
SUBLN_EPS = 1e-5
FFN_RESIDUAL = 0.5

LANES = 128
SUBLANES = 8
VMEM_LIMIT = 56 * 1024 * 1024

RANK_PAD = 128
SMALL_COLS = 2 * RANK_PAD + GATE_RANK
CHUNK = 64
PAIR = 2 * RWKV_HEAD
HI = lax.Precision.HIGHEST


def _cparams(n_axes):
    return pltpu.CompilerParams(
        dimension_semantics=("arbitrary",) * n_axes, vmem_limit_bytes=VMEM_LIMIT)


def _dot(a, b, dims=(((1,), (0,)), ((), ())), precision=None):
    return lax.dot_general(a, b, dims, precision=precision,
                           preferred_element_type=jnp.float32)


_NT = (((1,), (1,)), ((), ()))
_TN = (((0,), (0,)), ((), ()))


def _rms(x, g, eps):
    return x * lax.rsqrt(jnp.mean(x * x, axis=-1, keepdims=True) + eps) * g


def _norm_kernel(x_ref, g_ref, o_ref):
    o_ref[...] = _rms(x_ref[...], g_ref[...], NORM_EPS).astype(o_ref.dtype)


def rms_norm_cast(x, g, out_dtype, tm=256):
    m, d = x.shape
    return pl.pallas_call(
        _norm_kernel,
        grid=(m // tm,),
        in_specs=[pl.BlockSpec((tm, d), lambda i: (i, 0)),
                  pl.BlockSpec((1, d), lambda i: (0, 0))],
        out_specs=pl.BlockSpec((tm, d), lambda i: (i, 0)),
        out_shape=jax.ShapeDtypeStruct((m, d), out_dtype),
        compiler_params=_cparams(1),
        name="rms_norm",
    )(x, g.reshape(1, d))


def _resid_norm_kernel(h_ref, f_ref, gp_ref, gn_ref, h_out_ref, n_out_ref, *, scale):
    h = h_ref[...] + scale * _rms(f_ref[...], gp_ref[...], NORM_EPS)
    h_out_ref[...] = h
    n_out_ref[...] = _rms(h, gn_ref[...], NORM_EPS).astype(n_out_ref.dtype)


def resid_norm(h, f, g_post, g_next, scale, out_dtype, tm=256):
    m, d = h.shape
    row = pl.BlockSpec((tm, d), lambda i: (i, 0))
    vec = pl.BlockSpec((1, d), lambda i: (0, 0))
    return pl.pallas_call(
        functools.partial(_resid_norm_kernel, scale=scale),
        grid=(m // tm,),
        in_specs=[row, row, vec, vec],
        out_specs=[row, row],
        out_shape=[jax.ShapeDtypeStruct((m, d), jnp.float32),
                   jax.ShapeDtypeStruct((m, d), out_dtype)],
        compiler_params=_cparams(1),
        name="resid_norm",
    )(h, f, g_post.reshape(1, d), g_next.reshape(1, d))


def _mm_kernel(x_ref, w_ref, o_ref):
    o_ref[...] = _dot(x_ref[...], w_ref[...]).astype(o_ref.dtype)


def matmul(x, w, out_dtype, tm, tn):
    m, k = x.shape
    _, n = w.shape
    return pl.pallas_call(
        _mm_kernel,
        grid=(n // tn, m // tm),
        in_specs=[pl.BlockSpec((tm, k), lambda j, i: (i, 0)),
                  pl.BlockSpec((k, tn), lambda j, i: (0, j))],
        out_specs=pl.BlockSpec((tm, tn), lambda j, i: (i, j)),
        out_shape=jax.ShapeDtypeStruct((m, n), out_dtype),
        compiler_params=_cparams(2),
        name="matmul",
    )(x, w)


def _mm_scaled_kernel(x_ref, w_ref, cs_ref, o_ref):
    o_ref[...] = (_dot(x_ref[...], w_ref[...]) * cs_ref[...]).astype(o_ref.dtype)


def matmul_col_scaled(x, w, col_scale, out_dtype, tm, tn):
    m, k = x.shape
    _, n = w.shape
    return pl.pallas_call(
        _mm_scaled_kernel,
        grid=(n // tn, m // tm),
        in_specs=[pl.BlockSpec((tm, k), lambda j, i: (i, 0)),
                  pl.BlockSpec((k, tn), lambda j, i: (0, j)),
                  pl.BlockSpec((1, tn), lambda j, i: (0, j))],
        out_specs=pl.BlockSpec((tm, tn), lambda j, i: (i, j)),
        out_shape=jax.ShapeDtypeStruct((m, n), out_dtype),
        compiler_params=_cparams(2),
        name="matmul_col_scaled",
    )(x, w, col_scale)


def _gate_up_kernel(x_ref, wg_ref, wu_ref, o_ref):
    x = x_ref[...]
    g = _dot(x, wg_ref[...])
    u = _dot(x, wu_ref[...])
    o_ref[...] = (g * jax.nn.sigmoid(g) * u).astype(o_ref.dtype)


def gate_up(x, wg, wu, tm, tn):
    m, k = x.shape
    _, n = wg.shape
    wspec = pl.BlockSpec((k, tn), lambda j, i: (0, j))
    return pl.pallas_call(
        _gate_up_kernel,
        grid=(n // tn, m // tm),
        in_specs=[pl.BlockSpec((tm, k), lambda j, i: (i, 0)), wspec, wspec],
        out_specs=pl.BlockSpec((tm, tn), lambda j, i: (i, j)),
        out_shape=jax.ShapeDtypeStruct((m, n), jnp.bfloat16),
        compiler_params=_cparams(2),
        name="gate_up",
    )(x, wg, wu)


def swiglu_ffn(xn, wg, wu, wd, tm_gu, tn_gu, tm_d, tn_d):
    a = gate_up(xn, wg, wu, tm_gu, tn_gu)
    return matmul(a, wd, jnp.float32, tm_d, tn_d)


def _head_sum(x, ones_bd):
    parts = []
    for s in range(x.shape[1] // LANES):
        parts.append(_dot(x[:, s * LANES:(s + 1) * LANES], ones_bd, precision=HI))
    return jnp.concatenate(parts, axis=1)


def _token_shift(p_ref, pp_ref, pn_ref, mup_ref, mun_ref, first, last):
    p = p_ref[...]
    rows = p.shape[0]
    rid = lax.broadcasted_iota(jnp.int32, p.shape, 0)
    prev_row = jnp.where(first, 0.0, pp_ref[SUBLANES - 1:SUBLANES, :])
    next_row = jnp.where(last, 0.0, pn_ref[0:1, :])
    p_prev = jnp.where(rid == 0, prev_row, pltpu.roll(p, 1, 0))
    p_next = jnp.where(rid == rows - 1, next_row, pltpu.roll(p, rows - 1, 0))
    return p + mup_ref[...] * (p_prev - p) + mun_ref[...] * (p_next - p)


def _rwkv_prep_kernel(
        p_ref, pp_ref, pn_ref, s_ref, sp_ref, sn_ref,
        mup_ref, mun_ref, mups_ref, muns_ref,
        w0f_ref, w0b_ref, a0f_ref, a0b_ref, kk_ref, ka_ref, rk_ref,
        w2f_ref, w2b_ref, a2f_ref, a2b_ref, g2_ref,
        atf_ref, btf_ref, ktf_ref, rtf_ref, glf_ref,
        atb_ref, btb_ref, ktb_ref, rtb_ref, glb_ref,
        v_ref, gate_ref, bonus_ref):
    i = pl.program_id(0)
    first = i == 0
    last = i == pl.num_programs(0) - 1
    tm = p_ref.shape[0]
    n_chunks = tm // CHUNK

    p = _token_shift(p_ref, pp_ref, pn_ref, mup_ref, mun_ref, first, last)
    s = _token_shift(s_ref, sp_ref, sn_ref, mups_ref, muns_ref, first, last)
    r = p[:, 0:D_RWKV]
    k = p[:, D_RWKV:2 * D_RWKV]
    v = p[:, 2 * D_RWKV:3 * D_RWKV]
    hw = jnp.tanh(s[:, 0:RANK_PAD]).astype(jnp.bfloat16)
    xa = s[:, RANK_PAD:2 * RANK_PAD].astype(jnp.bfloat16)
    sg = jax.nn.sigmoid(s[:, 2 * RANK_PAD:]).astype(jnp.bfloat16)

    li = lax.broadcasted_iota(jnp.int32, (LANES, LANES), 0) // RWKV_HEAD
    lj = lax.broadcasted_iota(jnp.int32, (LANES, LANES), 1) // RWKV_HEAD
    ones_bd = (li == lj).astype(jnp.float32)

    kk = k * kk_ref[...]
    kk = kk / jnp.maximum(jnp.sqrt(_head_sum(kk * kk, ones_bd)), 1e-12)

    ti = lax.broadcasted_iota(jnp.int32, (tm, tm), 0)
    tj = lax.broadcasted_iota(jnp.int32, (tm, tm), 1)
    same_chunk = (ti // CHUNK) == (tj // CHUNK)

    k_sum = jnp.zeros_like(k)
    dirs = ((w0f_ref, w2f_ref, a0f_ref, a2f_ref, atf_ref, btf_ref, ktf_ref, rtf_ref, glf_ref, False),
            (w0b_ref, w2b_ref, a0b_ref, a2b_ref, atb_ref, btb_ref, ktb_ref, rtb_ref, glb_ref, True))
    for w0_ref, w2_ref, a0_ref, a2_ref, at_ref, bt_ref, kt_ref, rt_ref, gl_ref, rev in dirs:
        z = w0_ref[...] + _dot(hw, w2_ref[...])
        lw = -math.exp(-0.5) * jax.nn.sigmoid(z)
        a = jax.nn.sigmoid(a0_ref[...] + _dot(xa, a2_ref[...]))
        k_dir = k * (1.0 + (a - 1.0) * ka_ref[...])
        k_sum = k_sum + k_dir
        order = (tj >= ti) if rev else (tj <= ti)
        tri = jnp.where(same_chunk & order, 1.0, 0.0)
        c = _dot(tri, lw, precision=HI)
        e_neg = jnp.exp(-c)
        at_ref[...] = (-kk * jnp.exp(c - lw)).astype(at_ref.dtype)
        bt_ref[...] = (kk * a * e_neg).astype(bt_ref.dtype)
        kt_ref[...] = (k_dir * e_neg).astype(kt_ref.dtype)
        e_pos = jnp.exp(c)
        rt_ref[...] = (r * e_pos).astype(rt_ref.dtype)
        for j in range(n_chunks):
            end = j * CHUNK if rev else (j + 1) * CHUNK - 1
            gl_ref[j] = e_pos[end:end + 1, :]

    v_ref[...] = v.astype(v_ref.dtype)
    gate_ref[...] = _dot(sg, g2_ref[...])
    bonus_ref[...] = _head_sum(r * k_sum * rk_ref[...], ones_bd) * v


def rwkv_prep(proj, small, params, tm=128):
    t = proj.shape[0]
    nb = t // tm
    hb = tm // SUBLANES
    n_halo = t // SUBLANES
    wide = proj.shape[1]

    def row(c):
        return pl.BlockSpec((tm, c), lambda i: (i, 0))

    def halo_prev(c):
        return pl.BlockSpec((SUBLANES, c), lambda i: (jnp.maximum(i * hb - 1, 0), 0))

    def halo_next(c):
        return pl.BlockSpec((SUBLANES, c), lambda i: (jnp.minimum((i + 1) * hb, n_halo - 1), 0))

    def vec(c):
        return pl.BlockSpec((1, c), lambda i: (0, 0))

    def mat(r_, c):
        return pl.BlockSpec((r_, c), lambda i: (0, 0))

    gl_spec = pl.BlockSpec((tm // CHUNK, 1, D_RWKV), lambda i: (i, 0, 0))
    big = jax.ShapeDtypeStruct((t, D_RWKV), jnp.float32)
    gl = jax.ShapeDtypeStruct((t // CHUNK, 1, D_RWKV), jnp.float32)
    in_specs = [row(wide), halo_prev(wide), halo_next(wide),
                row(SMALL_COLS), halo_prev(SMALL_COLS), halo_next(SMALL_COLS),
                vec(wide), vec(wide), vec(SMALL_COLS), vec(SMALL_COLS)]
    in_specs += [vec(D_RWKV)] * 7
    in_specs += [mat(RANK_PAD, D_RWKV)] * 4 + [mat(GATE_RANK, D_RWKV)]
    dir_specs = [row(D_RWKV)] * 4 + [gl_spec]
    big_bf = jax.ShapeDtypeStruct((t, D_RWKV), jnp.bfloat16)
    dir_shapes = [big_bf] * 4 + [gl]
    return pl.pallas_call(
        _rwkv_prep_kernel,
        grid=(nb,),
        in_specs=in_specs,
        out_specs=dir_specs + dir_specs + [row(D_RWKV)] * 3,
        out_shape=dir_shapes + dir_shapes + [big_bf, big, big],
        compiler_params=_cparams(1),
        name="rwkv_prep",
    )(proj, proj, proj, small, small, small, *params)


def _stack(x, lane_head):
    zero = jnp.zeros_like(x)
    return jnp.concatenate([jnp.where(lane_head == 0, x, zero),
                            jnp.where(lane_head == 1, x, zero)], axis=0)


def _scan_masks(rev):
    n2 = 2 * CHUNK
    ri = lax.broadcasted_iota(jnp.int32, (n2, n2), 0)
    ci = lax.broadcasted_iota(jnp.int32, (n2, n2), 1)
    same = (ri // CHUNK) == (ci // CHUNK)
    before = (ci > ri) if rev else (ci < ri)
    strict = same & before
    incl = same & (before | (ri == ci))
    eye = jnp.where(ri == ci, 1.0, 0.0)
    return strict, jnp.concatenate([incl, incl], axis=1), eye


def _scan_prepare(units):
    bf = jnp.bfloat16
    n2 = 2 * CHUNK
    lane_head = lax.broadcasted_iota(jnp.int32, (CHUNK, PAIR), 1) // RWKV_HEAD
    stacked = [[_stack(x, lane_head) for x in u[:5]] for u in units]
    bks = [jnp.concatenate([s[1], s[2]], axis=0) for s in stacked]
    g_as = [_dot(s[0], bk, _NT) for s, bk in zip(stacked, bks)]
    g_rs = [_dot(s[3], bk, _NT) for s, bk in zip(stacked, bks)]
    xs = [jnp.where(u[6][0], g[:, :n2], 0.0) for u, g in zip(units, g_as)]
    a_aks = [jnp.where(u[6][0], g[:, n2:], 0.0).astype(bf) for u, g in zip(units, g_as)]
    a_rs = [jnp.where(u[6][1], g, 0.0).astype(bf) for u, g in zip(units, g_rs)]
    akv = [_dot(a, s[4]).astype(bf) for a, s in zip(a_aks, stacked)]

    tinvs = [u[6][2] + x for u, x in zip(units, xs)]
    for _ in range(int(math.log2(CHUNK)) - 1):
        xbs = [x.astype(bf) for x in xs]
        xs = [_dot(xb, xb) for xb in xbs]
        tinvs = [t + _dot(t.astype(bf), x.astype(bf)) for t, x in zip(tinvs, xs)]

    wqs = [_dot(t.astype(bf), jnp.concatenate([s[0], kv], axis=1))
           for t, s, kv in zip(tinvs, stacked, akv)]
    out = []
    for u, s, bk, wq, a_r in zip(units, stacked, bks, wqs, a_rs):
        gl = u[5]
        w = wq[:, :PAIR].astype(bf)
        q = wq[:, PAIR:]
        bkg = (bk * gl).astype(bf)
        m = _dot(w, bkg[:n2], _TN)
        n = _dot(jnp.concatenate([q.astype(bf), s[4]], axis=0), bkg, _TN)
        out.append((w, q, s[3], a_r, s[4], m.astype(bf), n, gl))
    return out


def _scan_output(prep, s0b):
    w, q, r2, a_r, v2 = prep[:5]
    u = (_dot(w, s0b, _NT) + q).astype(jnp.bfloat16)
    y2 = _dot(r2, s0b, _NT) + _dot(a_r, jnp.concatenate([u, v2], axis=0))
    return y2[:CHUNK] + y2[CHUNK:]


def _rwkv_scan_kernel(atf_ref, btf_ref, ktf_ref, rtf_ref, glf_ref, vf_ref,
                      atb_ref, btb_ref, ktb_ref, rtb_ref, glb_ref, vb_ref,
                      yf_ref, yb_ref, sf_ref, sb_ref, *, n_blk):
    @pl.when(pl.program_id(1) == 0)
    def _():
        sf_ref[...] = jnp.zeros_like(sf_ref)
        sb_ref[...] = jnp.zeros_like(sb_ref)

    def rows(j):
        return pl.ds(j * CHUNK, CHUNK)

    masks_f = _scan_masks(False)
    masks_b = _scan_masks(True)
    units = []
    for j in range(n_blk):
        units.append((atf_ref[rows(j), :], btf_ref[rows(j), :], ktf_ref[rows(j), :],
                      rtf_ref[rows(j), :], vf_ref[rows(j), :], glf_ref[j], masks_f))
        units.append((atb_ref[rows(j), :], btb_ref[rows(j), :], ktb_ref[rows(j), :],
                      rtb_ref[rows(j), :], vb_ref[rows(j), :], glb_ref[j], masks_b))
    prep = _scan_prepare(units)
    prep_f, prep_b = prep[0::2], prep[1::2]

    sf = sf_ref[...]
    sb = sb_ref[...]
    for j in range(n_blk):
        jb = n_blk - 1 - j
        pf, pb = prep_f[j], prep_b[jb]
        sfb = sf.astype(jnp.bfloat16)
        sbb = sb.astype(jnp.bfloat16)
        sf = sf * pf[7] + _dot(sfb, pf[5]) + pf[6]
        sb = sb * pb[7] + _dot(sbb, pb[5]) + pb[6]
        yf_ref[rows(j), :] = _scan_output(pf, sfb)
        yb_ref[rows(jb), :] = _scan_output(pb, sbb)
    sf_ref[...] = sf
    sb_ref[...] = sb


def rwkv_scan(fwd, bwd, v, n_blk=4):
    t = v.shape[0]
    nb = t // (CHUNK * n_blk)
    n_pairs = D_RWKV // PAIR
    f_blk = pl.BlockSpec((CHUNK * n_blk, PAIR), lambda p, c: (c, p))
    b_blk = pl.BlockSpec((CHUNK * n_blk, PAIR), lambda p, c: (nb - 1 - c, p))
    f_gl = pl.BlockSpec((n_blk, 1, PAIR), lambda p, c: (c, 0, p))
    b_gl = pl.BlockSpec((n_blk, 1, PAIR), lambda p, c: (nb - 1 - c, 0, p))
    out = jax.ShapeDtypeStruct((t, D_RWKV), jnp.float32)
    return pl.pallas_call(
        functools.partial(_rwkv_scan_kernel, n_blk=n_blk),
        grid=(n_pairs, nb),
        in_specs=[f_blk] * 4 + [f_gl, f_blk] + [b_blk] * 4 + [b_gl, b_blk],
        out_specs=[f_blk, b_blk],
        out_shape=[out, out],
        scratch_shapes=[pltpu.VMEM((PAIR, PAIR), jnp.float32)] * 2,
        compiler_params=_cparams(2),
        name="rwkv_scan",
    )(*fwd, v, *bwd, v)


def _rwkv_post_kernel(yf_ref, yb_ref, bonus_ref, gate_ref, gw_ref, gb_ref, o_ref):
    li = lax.broadcasted_iota(jnp.int32, (LANES, LANES), 0) // RWKV_HEAD
    lj = lax.broadcasted_iota(jnp.int32, (LANES, LANES), 1) // RWKV_HEAD
    mean_bd = jnp.where(li == lj, 1.0 / RWKV_HEAD, 0.0)
    y = yf_ref[...] + yb_ref[...]
    d = y - _head_sum(y, mean_bd)
    var = _head_sum(d * d, mean_bd)
    yn = d * lax.rsqrt(var + RWKV_GN_EPS) * gw_ref[...] + gb_ref[...]
    o_ref[...] = ((yn + bonus_ref[...]) * gate_ref[...]).astype(o_ref.dtype)


def rwkv_post(yf, yb, bonus, gate, gn_w, gn_b, tm=256):
    t = yf.shape[0]
    row = pl.BlockSpec((tm, D_RWKV), lambda i: (i, 0))
    vec = pl.BlockSpec((1, D_RWKV), lambda i: (0, 0))
    return pl.pallas_call(
        _rwkv_post_kernel,
        grid=(t // tm,),
        in_specs=[row, row, row, row, vec, vec],
        out_specs=row,
        out_shape=jax.ShapeDtypeStruct((t, D_RWKV), jnp.bfloat16),
        compiler_params=_cparams(1),
        name="rwkv_post",
    )(yf, yb, bonus, gate, gn_w.reshape(1, -1), gn_b.reshape(1, -1))


SLOPE_PIECES = 3
SC_PER_HEAD = 1 + SLOPE_PIECES


def _diff_attn_kernel(sc_ref, q_ref, k_ref, v_ref, g_ref, o_ref, *, tq, tk, lam_init):
    bf = jnp.bfloat16
    h = pl.program_id(0)
    qi = pl.program_id(1)
    n_kv = k_ref.shape[0] // tk
    lam = sc_ref[0]
    base = 1 + h * SC_PER_HEAD
    slope = sc_ref[base]
    pieces = [sc_ref[base + 1 + i] for i in range(SLOPE_PIECES)]

    q = q_ref[...]
    lane = lax.broadcasted_iota(jnp.int32, (2 * tq, LANES), 1)
    row = lax.broadcasted_iota(jnp.int32, (2 * tq, LANES), 0)
    qq = jnp.concatenate([q, q], axis=0)
    q2 = jnp.where((lane // DIFF_D) == (row // tq), qq, jnp.zeros_like(qq))

    feat = jnp.zeros((2 * tq, LANES), jnp.float32)
    for i, pc in enumerate(pieces):
        feat = jnp.where(lane == i, 2.0 * pc, feat)
        feat = jnp.where(lane == SLOPE_PIECES + i, pc, feat)
    q_before = jnp.concatenate([q2, feat.astype(bf)], axis=1)
    q_after = jnp.concatenate([q2, (-feat).astype(bf)], axis=1)
    kl = lax.broadcasted_iota(jnp.int32, (tk, LANES), 1)
    kr = lax.broadcasted_iota(jnp.int32, (tk, LANES), 0)
    kfeat = jnp.where(kl < SLOPE_PIECES, kr >> 1, jnp.where(kl < 2 * SLOPE_PIECES, kr & 1, 0))
    kfeat = kfeat.astype(jnp.float32).astype(bf)

    qpos = qi * tq + lax.broadcasted_iota(jnp.int32, (2 * tq, 1), 0) % tq
    row_bias = slope * qpos.astype(jnp.float32)

    def online(carry, s, t_row, vb):
        m, l, acc = carry
        m_new = jnp.maximum(m, jnp.max(s, axis=-1, keepdims=True) + t_row)
        alpha = jnp.exp2(m - m_new)
        p = jnp.exp2(s - (m_new - t_row))
        l_new = alpha * l + jnp.sum(p, axis=-1, keepdims=True)
        acc_new = alpha * acc + _dot(p.astype(bf), vb)
        return m_new, l_new, acc_new

    def linear_tile(q_aug, sign):
        def body(j, carry):
            start = pl.multiple_of(j * tk, tk)
            k_aug = jnp.concatenate([k_ref[pl.ds(start, tk), :], kfeat], axis=1)
            s = _dot(q_aug, k_aug, _NT)
            t_row = sign * (slope * (j * tk).astype(jnp.float32) - row_bias)
            return online(carry, s, t_row, v_ref[pl.ds(start, tk), :])
        return body

    def diagonal_tile(j, carry):
        start = pl.multiple_of(j * tk, tk)
        kpos = j * tk + lax.broadcasted_iota(jnp.int32, (2 * tq, tk), 1)
        bias = slope * jnp.abs(qpos - kpos).astype(jnp.float32)
        s = _dot(q2, k_ref[pl.ds(start, tk), :], _NT) - bias
        return online(carry, s, 0.0, v_ref[pl.ds(start, tk), :])

    carry = (jnp.full((2 * tq, 1), -jnp.inf, jnp.float32),
             jnp.zeros((2 * tq, 1), jnp.float32),
             jnp.zeros((2 * tq, DIFF_HEAD), jnp.float32))
    jd = (qi * tq) // tk
    carry = lax.fori_loop(0, jd, linear_tile(q_before, 1.0), carry)
    carry = diagonal_tile(jd, carry)
    m, l, acc = lax.fori_loop(jd + 1, n_kv, linear_tile(q_after, -1.0), carry)
    o = acc / l
    o = o[:tq] - lam * o[tq:]
    o_ref[...] = (_rms(o, g_ref[...], SUBLN_EPS) * (1.0 - lam_init)).astype(o_ref.dtype)


def _bf16_pieces(x, n):
    out = []
    for _ in range(n):
        p = x.astype(jnp.bfloat16).astype(jnp.float32)
        out.append(p)
        x = x - p
    return out


def diff_attention(q, k, v, lam, subln_g, lam_init, tq=256, tk=512):
    t = q.shape[0]
    assert tk % tq == 0 and tk // 2 <= 256
    slopes = 2.0 ** (-ALIBI_MAX_EXP * jnp.arange(1, N_DIFF_HEADS + 1, dtype=jnp.float32) / N_DIFF_HEADS)
    slopes = slopes * math.log2(math.e)
    per_head = jnp.stack([slopes] + _bf16_pieces(slopes, SLOPE_PIECES), axis=1)
    lam = jnp.concatenate([lam, per_head.reshape(-1)])
    return pl.pallas_call(
        functools.partial(_diff_attn_kernel, tq=tq, tk=tk, lam_init=lam_init),
        grid=(N_DIFF_HEADS, t // tq),
        in_specs=[pl.BlockSpec(memory_space=pltpu.SMEM),
                  pl.BlockSpec((tq, DIFF_HEAD), lambda h, i: (i, h)),
                  pl.BlockSpec((t, DIFF_HEAD), lambda h, i: (0, h)),
                  pl.BlockSpec((t, DIFF_HEAD), lambda h, i: (0, h)),
                  pl.BlockSpec((1, DIFF_HEAD), lambda h, i: (0, 0))],
        out_specs=pl.BlockSpec((tq, DIFF_HEAD), lambda h, i: (i, h)),
        out_shape=jax.ShapeDtypeStruct((t, D_DIFF), jnp.bfloat16),
        compiler_params=_cparams(2),
        name="diff_attn",
    )(lam, q, k, v, subln_g.reshape(1, DIFF_HEAD))


def _pad_cols(w, to):
    return jnp.pad(w, ((0, 0), (0, to - w.shape[1])))


def _layer(h, l, lam_init, x_norm, prm):
    bf = jnp.bfloat16
    f32 = jnp.float32
    f1 = swiglu_ffn(x_norm, prm["ffn1_w_gate"][l].astype(bf), prm["ffn1_w_up"][l].astype(bf),
                    prm["ffn1_w_down"][l].astype(bf), 1024, 256, 512, 512)
    h, xn = resid_norm(h, f1, prm["ffn1_post_g"][l], prm["mix_pre_g"][l], FFN_RESIDUAL, bf)

    w_in = prm["w_in"][l]
    c_rkv = 3 * D_RWKV
    c_w = c_rkv + DECAY_RANK
    c_a = c_w + ICLR_RANK
    c_g = c_a + GATE_RANK
    w_small = jnp.concatenate([_pad_cols(w_in[:, c_rkv:c_w], RANK_PAD),
                               _pad_cols(w_in[:, c_w:c_a], RANK_PAD),
                               w_in[:, c_a:c_g]], axis=1)
    w_rwkv = jnp.concatenate([w_in[:, :c_rkv], w_small], axis=1).astype(bf)
    proj = matmul(xn, w_rwkv, f32, 1024, 512)
    q_scale = DIFF_D ** -0.5 * math.log2(math.e)
    col_scale = jnp.concatenate([jnp.full((1, D_DIFF), q_scale, f32), jnp.ones((1, 2 * D_DIFF), f32)], axis=1)
    qkv = matmul_col_scaled(xn, w_in[:, c_g:].astype(bf), col_scale, bf, 1024, 512)

    def small_vec(a):
        return jnp.concatenate([_pad_cols(a[None, c_rkv:c_w], RANK_PAD),
                                _pad_cols(a[None, c_w:c_a], RANK_PAD),
                                a[None, c_a:c_g]], axis=1)

    def pad_rows(w):
        return jnp.pad(w, ((0, RANK_PAD - w.shape[0]), (0, 0))).astype(bf)

    mu_p, mu_n = prm["mu_prev"][l], prm["mu_next"][l]
    vecs = [prm[n][l].reshape(1, D_RWKV) for n in ("w0_f", "w0_b", "a0_f", "a0_b", "k_k", "k_a", "r_k")]
    mats = [pad_rows(prm["w2_f"][l]), pad_rows(prm["w2_b"][l]),
            pad_rows(prm["a2_f"][l]), pad_rows(prm["a2_b"][l]), prm["g2"][l].astype(bf)]
    prep = rwkv_prep(proj[:, :c_rkv], proj[:, c_rkv:],
                     [mu_p[None, :c_rkv], mu_n[None, :c_rkv], small_vec(mu_p), small_vec(mu_n)]
                     + vecs + mats)
    fwd, bwd, (v_r, gate, bonus) = prep[0:5], prep[5:10], prep[10:13]
    yf, yb = rwkv_scan(fwd, bwd, v_r)
    y_a = rwkv_post(yf, yb, bonus, gate, prm["gn_w"][l], prm["gn_b"][l])

    lam = (jnp.exp(jnp.sum(prm["lq1"][l] * prm["lk1"][l]))
           - jnp.exp(jnp.sum(prm["lq2"][l] * prm["lk2"][l])) + lam_init).reshape(1)
    y_b = diff_attention(qkv[:, :D_DIFF], qkv[:, D_DIFF:2 * D_DIFF], qkv[:, 2 * D_DIFF:],
                         lam, prm["subln_g"][l], lam_init)

    mix = matmul(jnp.concatenate([y_a, y_b], axis=1), prm["w_out"][l].astype(bf), f32, 1024, 512)
    h, xn = resid_norm(h, mix, prm["mix_post_g"][l], prm["ffn2_pre_g"][l], 1.0, bf)

    f2 = swiglu_ffn(xn, prm["ffn2_w_gate"][l].astype(bf), prm["ffn2_w_up"][l].astype(bf),
                    prm["ffn2_w_down"][l].astype(bf), 1024, 256, 512, 512)
    return resid_norm(h, f2, prm["ffn2_post_g"][l], prm["final_g"][l], FFN_RESIDUAL, f32)


def kernel(x, ffn1_pre_g, ffn1_w_gate, ffn1_w_up, ffn1_w_down, ffn1_post_g, mix_pre_g, w_in, mu_prev, mu_next, w0_f, w2_f, w0_b, w2_b, a0_f, a2_f, a0_b, a2_b, g2, k_k, k_a, r_k, gn_w, gn_b, lq1, lk1, lq2, lk2, subln_g, w_out, mix_post_g, ffn2_pre_g, ffn2_w_gate, ffn2_w_up, ffn2_w_down, ffn2_post_g, final_g):
    prm = dict(locals())
    bsz, t, d = x.shape
    depth = ffn1_pre_g.shape[0]
    outs = []
    for b in range(bsz):
        h = x[b]
        for l in range(depth):
            lam_init = 0.8 - 0.6 * math.exp(-0.3 * l)
            x_norm = rms_norm_cast(h, ffn1_pre_g[l], jnp.bfloat16)
            _, h = _layer(h, l, lam_init, x_norm, prm)
        outs.append(h)
    return jnp.stack(outs, axis=0)
```

```python
import functools
import math

import jax
import jax.numpy as jnp
import numpy as np
from jax import lax
from jax.experimental import pallas as pl
from jax.experimental.pallas import tpu as pltpu

D_MODEL = 4096
D_RWKV = 2048
D_DIFF = 2048
RWKV_HEAD = 64
DECAY_RANK = 96
ICLR_RANK = 96
GATE_RANK = 256
DIFF_D = 64
DIFF_HEAD = 128
N_DIFF_HEADS = 16
ALIBI_MAX_EXP = 8.0
NORM_EPS = 1e-6
RWKV_GN_EPS = 64e-5
SUBLN_EPS = 1e-5
FFN_RESIDUAL = 0.5

LANES = 128
SUBLANES = 8
VMEM_LIMIT = 56 * 1024 * 1024

RANK_PAD = 128
SMALL_COLS = 2 * RANK_PAD + GATE_RANK
CHUNK = 64
PAIR = 2 * RWKV_HEAD
HI = lax.Precision.HIGHEST


def _cparams(n_axes):
    return pltpu.CompilerParams(
        dimension_semantics=("arbitrary",) * n_axes, vmem_limit_bytes=VMEM_LIMIT)


def _dot(a, b, dims=(((1,), (0,)), ((), ())), precision=None):
    return lax.dot_general(a, b, dims, precision=precision,
                           preferred_element_type=jnp.float32)


_NT = (((1,), (1,)), ((), ()))
_TN = (((0,), (0,)), ((), ()))


def _rms(x, g, eps):
    return x * lax.rsqrt(jnp.mean(x * x, axis=-1, keepdims=True) + eps) * g


def _norm_kernel(x_ref, g_ref, o_ref):
    o_ref[...] = _rms(x_ref[...], g_ref[...], NORM_EPS).astype(o_ref.dtype)


def rms_norm_cast(x, g, out_dtype, tm=256):
    m, d = x.shape
    return pl.pallas_call(
        _norm_kernel,
        grid=(m // tm,),
        in_specs=[pl.BlockSpec((tm, d), lambda i: (i, 0)),
                  pl.BlockSpec((1, d), lambda i: (0, 0))],
        out_specs=pl.BlockSpec((tm, d), lambda i: (i, 0)),
        out_shape=jax.ShapeDtypeStruct((m, d), out_dtype),
        compiler_params=_cparams(1),
        name="rms_norm",
    )(x, g.reshape(1, d))


def _resid_norm_kernel(h_ref, f_ref, gp_ref, gn_ref, h_out_ref, n_out_ref, *, scale):
    h = h_ref[...] + scale * _rms(f_ref[...], gp_ref[...], NORM_EPS)
    h_out_ref[...] = h
    n_out_ref[...] = _rms(h, gn_ref[...], NORM_EPS).astype(n_out_ref.dtype)


def resid_norm(h, f, g_post, g_next, scale, out_dtype, tm=256):
    m, d = h.shape
    row = pl.BlockSpec((tm, d), lambda i: (i, 0))
    vec = pl.BlockSpec((1, d), lambda i: (0, 0))
    return pl.pallas_call(
        functools.partial(_resid_norm_kernel, scale=scale),
        grid=(m // tm,),
        in_specs=[row, row, vec, vec],
        out_specs=[row, row],
        out_shape=[jax.ShapeDtypeStruct((m, d), jnp.float32),
                   jax.ShapeDtypeStruct((m, d), out_dtype)],
        compiler_params=_cparams(1),
        name="resid_norm",
    )(h, f, g_post.reshape(1, d), g_next.reshape(1, d))


def _mm_kernel(x_ref, w_ref, o_ref):
    o_ref[...] = _dot(x_ref[...], w_ref[...]).astype(o_ref.dtype)


def matmul(x, w, out_dtype, tm, tn):
    m, k = x.shape
    _, n = w.shape
    return pl.pallas_call(
        _mm_kernel,
        grid=(n // tn, m // tm),
        in_specs=[pl.BlockSpec((tm, k), lambda j, i: (i, 0)),
                  pl.BlockSpec((k, tn), lambda j, i: (0, j))],
        out_specs=pl.BlockSpec((tm, tn), lambda j, i: (i, j)),
        out_shape=jax.ShapeDtypeStruct((m, n), out_dtype),
        compiler_params=_cparams(2),
        name="matmul",
    )(x, w)


def _mm_scaled_kernel(x_ref, w_ref, cs_ref, o_ref):
    o_ref[...] = (_dot(x_ref[...], w_ref[...]) * cs_ref[...]).astype(o_ref.dtype)


def matmul_col_scaled(x, w, col_scale, out_dtype, tm, tn):
    m, k = x.shape
    _, n = w.shape
    return pl.pallas_call(
        _mm_scaled_kernel,
        grid=(n // tn, m // tm),
        in_specs=[pl.BlockSpec((tm, k), lambda j, i: (i, 0)),
                  pl.BlockSpec((k, tn), lambda j, i: (0, j)),
                  pl.BlockSpec((1, tn), lambda j, i: (0, j))],
        out_specs=pl.BlockSpec((tm, tn), lambda j, i: (i, j)),
        out_shape=jax.ShapeDtypeStruct((m, n), out_dtype),
        compiler_params=_cparams(2),
        name="matmul_col_scaled",
    )(x, w, col_scale)


def _gate_up_kernel(x_ref, wg_ref, wu_ref, o_ref):
    x = x_ref[...]
    g = _dot(x, wg_ref[...])
    u = _dot(x, wu_ref[...])
    o_ref[...] = (g * jax.nn.sigmoid(g) * u).astype(o_ref.dtype)


def gate_up(x, wg, wu, tm, tn):
    m, k = x.shape
    _, n = wg.shape
    wspec = pl.BlockSpec((k, tn), lambda j, i: (0, j))
    return pl.pallas_call(
        _gate_up_kernel,
        grid=(n // tn, m // tm),
        in_specs=[pl.BlockSpec((tm, k), lambda j, i: (i, 0)), wspec, wspec],
        out_specs=pl.BlockSpec((tm, tn), lambda j, i: (i, j)),
        out_shape=jax.ShapeDtypeStruct((m, n), jnp.bfloat16),
        compiler_params=_cparams(2),
        name="gate_up",
    )(x, wg, wu)


def swiglu_ffn(xn, wg, wu, wd, tm_gu, tn_gu, tm_d, tn_d):
    a = gate_up(xn, wg, wu, tm_gu, tn_gu)
    return matmul(a, wd, jnp.float32, tm_d, tn_d)


def _head_sum(x, ones_bd):
    parts = []
    for s in range(x.shape[1] // LANES):
        parts.append(_dot(x[:, s * LANES:(s + 1) * LANES], ones_bd, precision=HI))
    return jnp.concatenate(parts, axis=1)


def _token_shift(p_ref, pp_ref, pn_ref, mup_ref, mun_ref, first, last):
    p = p_ref[...]
    rows = p.shape[0]
    rid = lax.broadcasted_iota(jnp.int32, p.shape, 0)
    prev_row = jnp.where(first, 0.0, pp_ref[SUBLANES - 1:SUBLANES, :])
    next_row = jnp.where(last, 0.0, pn_ref[0:1, :])
    p_prev = jnp.where(rid == 0, prev_row, pltpu.roll(p, 1, 0))
    p_next = jnp.where(rid == rows - 1, next_row, pltpu.roll(p, rows - 1, 0))
    return p + mup_ref[...] * (p_prev - p) + mun_ref[...] * (p_next - p)


def _rwkv_prep_kernel(
        p_ref, pp_ref, pn_ref, s_ref, sp_ref, sn_ref,
        mup_ref, mun_ref, mups_ref, muns_ref,
        w0f_ref, w0b_ref, a0f_ref, a0b_ref, kk_ref, ka_ref, rk_ref,
        w2f_ref, w2b_ref, a2f_ref, a2b_ref, g2_ref,
        atf_ref, btf_ref, ktf_ref, rtf_ref, glf_ref,
        atb_ref, btb_ref, ktb_ref, rtb_ref, glb_ref,
        v_ref, gate_ref, bonus_ref):
    i = pl.program_id(0)
    first = i == 0
    last = i == pl.num_programs(0) - 1
    tm = p_ref.shape[0]
    n_chunks = tm // CHUNK

    p = _token_shift(p_ref, pp_ref, pn_ref, mup_ref, mun_ref, first, last)
    s = _token_shift(s_ref, sp_ref, sn_ref, mups_ref, muns_ref, first, last)
    r = p[:, 0:D_RWKV]
    k = p[:, D_RWKV:2 * D_RWKV]
    v = p[:, 2 * D_RWKV:3 * D_RWKV]
    hw = jnp.tanh(s[:, 0:RANK_PAD]).astype(jnp.bfloat16)
    xa = s[:, RANK_PAD:2 * RANK_PAD].astype(jnp.bfloat16)
    sg = jax.nn.sigmoid(s[:, 2 * RANK_PAD:]).astype(jnp.bfloat16)

    li = lax.broadcasted_iota(jnp.int32, (LANES, LANES), 0) // RWKV_HEAD
    lj = lax.broadcasted_iota(jnp.int32, (LANES, LANES), 1) // RWKV_HEAD
    ones_bd = (li == lj).astype(jnp.float32)

    kk = k * kk_ref[...]
    kk = kk / jnp.maximum(jnp.sqrt(_head_sum(kk * kk, ones_bd)), 1e-12)

    ti = lax.broadcasted_iota(jnp.int32, (tm, tm), 0)
    tj = lax.broadcasted_iota(jnp.int32, (tm, tm), 1)
    same_chunk = (ti // CHUNK) == (tj // CHUNK)

    k_sum = jnp.zeros_like(k)
    dirs = ((w0f_ref, w2f_ref, a0f_ref, a2f_ref, atf_ref, btf_ref, ktf_ref, rtf_ref, glf_ref, False),
            (w0b_ref, w2b_ref, a0b_ref, a2b_ref, atb_ref, btb_ref, ktb_ref, rtb_ref, glb_ref, True))
    for w0_ref, w2_ref, a0_ref, a2_ref, at_ref, bt_ref, kt_ref, rt_ref, gl_ref, rev in dirs:
        z = w0_ref[...] + _dot(hw, w2_ref[...])
        lw = -math.exp(-0.5) * jax.nn.sigmoid(z)
        a = jax.nn.sigmoid(a0_ref[...] + _dot(xa, a2_ref[...]))
        k_dir = k * (1.0 + (a - 1.0) * ka_ref[...])
        k_sum = k_sum + k_dir
        order = (tj >= ti) if rev else (tj <= ti)
        tri = jnp.where(same_chunk & order, 1.0, 0.0)
        c = _dot(tri, lw, precision=HI)
        e_neg = jnp.exp(-c)
        at_ref[...] = (-kk * jnp.exp(c - lw)).astype(at_ref.dtype)
        bt_ref[...] = (kk * a * e_neg).astype(bt_ref.dtype)
        kt_ref[...] = (k_dir * e_neg).astype(kt_ref.dtype)
        e_pos = jnp.exp(c)
        rt_ref[...] = (r * e_pos).astype(rt_ref.dtype)
        for j in range(n_chunks):
            end = j * CHUNK if rev else (j + 1) * CHUNK - 1
            gl_ref[j] = e_pos[end:end + 1, :]

    v_ref[...] = v.astype(v_ref.dtype)
    gate_ref[...] = _dot(sg, g2_ref[...])
    bonus_ref[...] = _head_sum(r * k_sum * rk_ref[...], ones_bd) * v


def rwkv_prep(proj, small, params, tm=128):
    t = proj.shape[0]
    nb = t // tm
    hb = tm // SUBLANES
    n_halo = t // SUBLANES
    wide = proj.shape[1]

    def row(c):
        return pl.BlockSpec((tm, c), lambda i: (i, 0))

    def halo_prev(c):
        return pl.BlockSpec((SUBLANES, c), lambda i: (jnp.maximum(i * hb - 1, 0), 0))

    def halo_next(c):
        return pl.BlockSpec((SUBLANES, c), lambda i: (jnp.minimum((i + 1) * hb, n_halo - 1), 0))

    def vec(c):
        return pl.BlockSpec((1, c), lambda i: (0, 0))

    def mat(r_, c):
        return pl.BlockSpec((r_, c), lambda i: (0, 0))

    gl_spec = pl.BlockSpec((tm // CHUNK, 1, D_RWKV), lambda i: (i, 0, 0))
    big = jax.ShapeDtypeStruct((t, D_RWKV), jnp.float32)
    gl = jax.ShapeDtypeStruct((t // CHUNK, 1, D_RWKV), jnp.float32)
    in_specs = [row(wide), halo_prev(wide), halo_next(wide),
                row(SMALL_COLS), halo_prev(SMALL_COLS), halo_next(SMALL_COLS),
                vec(wide), vec(wide), vec(SMALL_COLS), vec(SMALL_COLS)]
    in_specs += [vec(D_RWKV)] * 7
    in_specs += [mat(RANK_PAD, D_RWKV)] * 4 + [mat(GATE_RANK, D_RWKV)]
    dir_specs = [row(D_RWKV)] * 4 + [gl_spec]
    big_bf = jax.ShapeDtypeStruct((t, D_RWKV), jnp.bfloat16)
    dir_shapes = [big_bf] * 4 + [gl]
    return pl.pallas_call(
        _rwkv_prep_kernel,
        grid=(nb,),
        in_specs=in_specs,
        out_specs=dir_specs + dir_specs + [row(D_RWKV)] * 3,
        out_shape=dir_shapes + dir_shapes + [big_bf, big, big],
        compiler_params=_cparams(1),
        name="rwkv_prep",
    )(proj, proj, proj, small, small, small, *params)


def _stack(x, lane_head):
    zero = jnp.zeros_like(x)
    return jnp.concatenate([jnp.where(lane_head == 0, x, zero),
                            jnp.where(lane_head == 1, x, zero)], axis=0)


def _scan_masks(rev):
    n2 = 2 * CHUNK
    ri = lax.broadcasted_iota(jnp.int32, (n2, n2), 0)
    ci = lax.broadcasted_iota(jnp.int32, (n2, n2), 1)
    same = (ri // CHUNK) == (ci // CHUNK)
    before = (ci > ri) if rev else (ci < ri)
    strict = same & before
    incl = same & (before | (ri == ci))
    eye = jnp.where(ri == ci, 1.0, 0.0)
    return strict, jnp.concatenate([incl, incl], axis=1), eye


def _scan_prepare(units):
    bf = jnp.bfloat16
    n2 = 2 * CHUNK
    lane_head = lax.broadcasted_iota(jnp.int32, (CHUNK, PAIR), 1) // RWKV_HEAD
    stacked = [[_stack(x, lane_head) for x in u[:5]] for u in units]
    bks = [jnp.concatenate([s[1], s[2]], axis=0) for s in stacked]
    g_as = [_dot(s[0], bk, _NT) for s, bk in zip(stacked, bks)]
    g_rs = [_dot(s[3], bk, _NT) for s, bk in zip(stacked, bks)]
    xs = [jnp.where(u[6][0], g[:, :n2], 0.0) for u, g in zip(units, g_as)]
    a_aks = [jnp.where(u[6][0], g[:, n2:], 0.0).astype(bf) for u, g in zip(units, g_as)]
    a_rs = [jnp.where(u[6][1], g, 0.0).astype(bf) for u, g in zip(units, g_rs)]
    akv = [_dot(a, s[4]).astype(bf) for a, s in zip(a_aks, stacked)]

    tinvs = [u[6][2] + x for u, x in zip(units, xs)]
    for _ in range(int(math.log2(CHUNK)) - 1):
        xbs = [x.astype(bf) for x in xs]
        xs = [_dot(xb, xb) for xb in xbs]
        tinvs = [t + _dot(t.astype(bf), x.astype(bf)) for t, x in zip(tinvs, xs)]

    wqs = [_dot(t.astype(bf), jnp.concatenate([s[0], kv], axis=1))
           for t, s, kv in zip(tinvs, stacked, akv)]
    out = []
    for u, s, bk, wq, a_r in zip(units, stacked, bks, wqs, a_rs):
        gl = u[5]
        w = wq[:, :PAIR].astype(bf)
        q = wq[:, PAIR:]
        bkg = (bk * gl).astype(bf)
        m = _dot(w, bkg[:n2], _TN)
        n = _dot(jnp.concatenate([q.astype(bf), s[4]], axis=0), bkg, _TN)
        out.append((w, q, s[3], a_r, s[4], m.astype(bf), n, gl))
    return out


def _scan_output(prep, s0b):
    w, q, r2, a_r, v2 = prep[:5]
    u = (_dot(w, s0b, _NT) + q).astype(jnp.bfloat16)
    y2 = _dot(r2, s0b, _NT) + _dot(a_r, jnp.concatenate([u, v2], axis=0))
    return y2[:CHUNK] + y2[CHUNK:]


def _rwkv_scan_kernel(atf_ref, btf_ref, ktf_ref, rtf_ref, glf_ref, vf_ref,
                      atb_ref, btb_ref, ktb_ref, rtb_ref, glb_ref, vb_ref,
                      yf_ref, yb_ref, sf_ref, sb_ref, *, n_blk):
    @pl.when(pl.program_id(1) == 0)
    def _():
        sf_ref[...] = jnp.zeros_like(sf_ref)
        sb_ref[...] = jnp.zeros_like(sb_ref)

    def rows(j):
        return pl.ds(j * CHUNK, CHUNK)

    masks_f = _scan_masks(False)
    masks_b = _scan_masks(True)
    units = []
    for j in range(n_blk):
        units.append((atf_ref[rows(j), :], btf_ref[rows(j), :], ktf_ref[rows(j), :],
                      rtf_ref[rows(j), :], vf_ref[rows(j), :], glf_ref[j], masks_f))
        units.append((atb_ref[rows(j), :], btb_ref[rows(j), :], ktb_ref[rows(j), :],
                      rtb_ref[rows(j), :], vb_ref[rows(j), :], glb_ref[j], masks_b))
    prep = _scan_prepare(units)
    prep_f, prep_b = prep[0::2], prep[1::2]

    sf = sf_ref[...]
    sb = sb_ref[...]
    for j in range(n_blk):
        jb = n_blk - 1 - j
        pf, pb = prep_f[j], prep_b[jb]
        sfb = sf.astype(jnp.bfloat16)
        sbb = sb.astype(jnp.bfloat16)
        sf = sf * pf[7] + _dot(sfb, pf[5]) + pf[6]
        sb = sb * pb[7] + _dot(sbb, pb[5]) + pb[6]
        yf_ref[rows(j), :] = _scan_output(pf, sfb)
        yb_ref[rows(jb), :] = _scan_output(pb, sbb)
    sf_ref[...] = sf
    sb_ref[...] = sb


def rwkv_scan(fwd, bwd, v, n_blk=4):
    t = v.shape[0]
    nb = t // (CHUNK * n_blk)
    n_pairs = D_RWKV // PAIR
    f_blk = pl.BlockSpec((CHUNK * n_blk, PAIR), lambda p, c: (c, p))
    b_blk = pl.BlockSpec((CHUNK * n_blk, PAIR), lambda p, c: (nb - 1 - c, p))
    f_gl = pl.BlockSpec((n_blk, 1, PAIR), lambda p, c: (c, 0, p))
    b_gl = pl.BlockSpec((n_blk, 1, PAIR), lambda p, c: (nb - 1 - c, 0, p))
    out = jax.ShapeDtypeStruct((t, D_RWKV), jnp.float32)
    return pl.pallas_call(
        functools.partial(_rwkv_scan_kernel, n_blk=n_blk),
        grid=(n_pairs, nb),
        in_specs=[f_blk] * 4 + [f_gl, f_blk] + [b_blk] * 4 + [b_gl, b_blk],
        out_specs=[f_blk, b_blk],
        out_shape=[out, out],
        scratch_shapes=[pltpu.VMEM((PAIR, PAIR), jnp.float32)] * 2,
        compiler_params=_cparams(2),
        name="rwkv_scan",
    )(*fwd, v, *bwd, v)


def _rwkv_post_kernel(yf_ref, yb_ref, bonus_ref, gate_ref, gw_ref, gb_ref, o_ref):
    li = lax.broadcasted_iota(jnp.int32, (LANES, LANES), 0) // RWKV_HEAD
    lj = lax.broadcasted_iota(jnp.int32, (LANES, LANES), 1) // RWKV_HEAD
    mean_bd = jnp.where(li == lj, 1.0 / RWKV_HEAD, 0.0)
    y = yf_ref[...] + yb_ref[...]
    d = y - _head_sum(y, mean_bd)
    var = _head_sum(d * d, mean_bd)
    yn = d * lax.rsqrt(var + RWKV_GN_EPS) * gw_ref[...] + gb_ref[...]
    o_ref[...] = ((yn + bonus_ref[...]) * gate_ref[...]).astype(o_ref.dtype)


def rwkv_post(yf, yb, bonus, gate, gn_w, gn_b, tm=256):
    t = yf.shape[0]
    row = pl.BlockSpec((tm, D_RWKV), lambda i: (i, 0))
    vec = pl.BlockSpec((1, D_RWKV), lambda i: (0, 0))
    return pl.pallas_call(
        _rwkv_post_kernel,
        grid=(t // tm,),
        in_specs=[row, row, row, row, vec, vec],
        out_specs=row,
        out_shape=jax.ShapeDtypeStruct((t, D_RWKV), jnp.bfloat16),
        compiler_params=_cparams(1),
        name="rwkv_post",
    )(yf, yb, bonus, gate, gn_w.reshape(1, -1), gn_b.reshape(1, -1))


SLOPE_PIECES = 3
SC_PER_HEAD = 1 + SLOPE_PIECES


def _diff_attn_kernel(sc_ref, qt_ref, k_ref, vt_ref, g_ref, o_ref, s_ref, *, tq, tk, lam_init):
    bf = jnp.bfloat16
    h = pl.program_id(0)
    qi = pl.program_id(1)
    n_kv = k_ref.shape[0] // tk
    lam = sc_ref[0]
    base = 1 + h * SC_PER_HEAD
    slope = sc_ref[base]
    pieces = [sc_ref[base + 1 + i] for i in range(SLOPE_PIECES)]

    qt = qt_ref[...]
    qrow = lax.broadcasted_iota(jnp.int32, (DIFF_HEAD, tq), 0)
    feat = jnp.zeros((DIFF_HEAD, tq), jnp.float32)
    for i, pc in enumerate(pieces):
        feat = jnp.where(qrow % DIFF_D == i, 2.0 * pc, feat)
        feat = jnp.where(qrow % DIFF_D == SLOPE_PIECES + i, pc, feat)
    feat = feat.astype(bf)
    own_rows = [qrow < DIFF_D, qrow >= DIFF_D]
    qts = [jnp.where(own, qt, feat) for own in own_rows]

    kl = lax.broadcasted_iota(jnp.int32, (tk, LANES), 1)
    kr = lax.broadcasted_iota(jnp.int32, (tk, LANES), 0)
    kfeat = jnp.where(kl % DIFF_D < SLOPE_PIECES, kr >> 1,
                      jnp.where(kl % DIFF_D < 2 * SLOPE_PIECES, kr & 1, 0)).astype(jnp.float32)
    kfeat_before = kfeat.astype(bf)
    kfeat_after = (-kfeat).astype(bf)
    kzero = jnp.zeros((tk, LANES), bf)
    own_lanes = [kl < DIFF_D, kl >= DIFF_D]

    qpos = qi * tq + lax.broadcasted_iota(jnp.int32, (1, tq), 1)
    q_bias = slope * qpos.astype(jnp.float32)

    def online(carry, s, t_q, vt):
        m, l, acc = carry
        m_new = jnp.maximum(m, jnp.max(s, axis=0, keepdims=True) + t_q)
        alpha = jnp.exp2(m - m_new)
        p = jnp.exp2(s - (m_new - t_q))
        l_new = alpha * l + jnp.sum(p, axis=0, keepdims=True)
        acc_new = alpha * acc + _dot(vt, p.astype(bf))
        return m_new, l_new, acc_new

    jd = (qi * tq) // tk

    def tile_of(n):
        return n + (n >= jd).astype(jnp.int32)

    def scores(n, slot):
        j = tile_of(n)
        start = pl.multiple_of(j * tk, tk)
        kb = k_ref[pl.ds(start, tk), :]
        kf = jnp.where(j < jd, kfeat_before, kfeat_after)
        for mp, (own, q_m) in enumerate(zip(own_lanes, qts)):
            s_ref[slot, mp] = _dot(jnp.where(own, kb, kf), q_m)

    def consume(n, slot, carries):
        j = tile_of(n)
        start = pl.multiple_of(j * tk, tk)
        vt = vt_ref[:, pl.ds(start, tk)]
        sign = jnp.where(j < jd, 1.0, -1.0)
        t_q = sign * (slope * (j * tk).astype(jnp.float32) - q_bias)
        return tuple(online(c, s_ref[slot, mp], t_q, vt) for mp, c in enumerate(carries))

    start_d = pl.multiple_of(jd * tk, tk)
    kb_d = k_ref[pl.ds(start_d, tk), :]
    vt_d = vt_ref[:, pl.ds(start_d, tk)]
    kpos = jd * tk + lax.broadcasted_iota(jnp.int32, (tk, tq), 0)
    bias = slope * jnp.abs(kpos - qpos).astype(jnp.float32)
    init = (jnp.full((1, tq), -jnp.inf, jnp.float32),
            jnp.zeros((1, tq), jnp.float32),
            jnp.zeros((DIFF_HEAD, tq), jnp.float32))
    carries = tuple(online(init, _dot(jnp.where(own, kb_d, kzero), q_m) - bias, 0.0, vt_d)
                    for own, q_m in zip(own_lanes, qts))

    def body(i, carries):
        n = 2 * i
        scores(n + 1, 1)
        carries = consume(n, 0, carries)
        scores(n + 2, 0)
        return consume(n + 1, 1, carries)

    n_off = n_kv - 1
    scores(jnp.int32(0), 0)
    carries = lax.fori_loop(0, n_off // 2, body, carries)
    (_, l0, acc0), (_, l1, acc1) = consume(jnp.int32(n_off - 1), 0, carries)
    o = (acc0 / l0 - lam * (acc1 / l1)).T
    o_ref[...] = (_rms(o, g_ref[...], SUBLN_EPS) * (1.0 - lam_init)).astype(o_ref.dtype)


def _bf16_pieces(x, n):
    out = []
    for _ in range(n):
        p = (x.view(np.uint32) & np.uint32(0xFFFF0000)).view(np.float32)
        out.append(p)
        x = (x - p).astype(np.float32)
    return out


def diff_attention(qt, k, vt, lam, subln_g, lam_init, tq=256, tk=512):
    t = k.shape[0]
    assert tk % tq == 0 and tk // 2 <= 256
    assert t % (2 * tk) == 0
    slopes = 2.0 ** (-ALIBI_MAX_EXP * np.arange(1, N_DIFF_HEADS + 1, dtype=np.float64) / N_DIFF_HEADS)
    slopes = (slopes * math.log2(math.e)).astype(np.float32)
    pieces = _bf16_pieces(slopes, SLOPE_PIECES)
    per_head = np.stack([sum(pieces)] + pieces, axis=1)
    lam = jnp.concatenate([lam, jnp.asarray(per_head.reshape(-1))])
    return pl.pallas_call(
        functools.partial(_diff_attn_kernel, tq=tq, tk=tk, lam_init=lam_init),
        grid=(N_DIFF_HEADS, t // tq),
        in_specs=[pl.BlockSpec(memory_space=pltpu.SMEM),
                  pl.BlockSpec((DIFF_HEAD, tq), lambda h, i: (h, i)),
                  pl.BlockSpec((t, DIFF_HEAD), lambda h, i: (0, h)),
                  pl.BlockSpec((DIFF_HEAD, t), lambda h, i: (h, 0)),
                  pl.BlockSpec((1, DIFF_HEAD), lambda h, i: (0, 0))],
        out_specs=pl.BlockSpec((tq, DIFF_HEAD), lambda h, i: (i, h)),
        out_shape=jax.ShapeDtypeStruct((t, D_DIFF), jnp.bfloat16),
        scratch_shapes=[pltpu.VMEM((2, 2, tk, tq), jnp.float32)],
        compiler_params=_cparams(2),
        name="diff_attn",
    )(lam, qt, k, vt, subln_g.reshape(1, DIFF_HEAD))


def _pad_cols(w, to):
    return jnp.pad(w, ((0, 0), (0, to - w.shape[1])))


def _layer(h, l, lam_init, x_norm, prm):
    bf = jnp.bfloat16
    f32 = jnp.float32
    f1 = swiglu_ffn(x_norm, prm["ffn1_w_gate"][l].astype(bf), prm["ffn1_w_up"][l].astype(bf),
                    prm["ffn1_w_down"][l].astype(bf), 1024, 256, 512, 512)
    h, xn = resid_norm(h, f1, prm["ffn1_post_g"][l], prm["mix_pre_g"][l], FFN_RESIDUAL, bf)

    w_in = prm["w_in"][l]
    c_rkv = 3 * D_RWKV
    c_w = c_rkv + DECAY_RANK
    c_a = c_w + ICLR_RANK
    c_g = c_a + GATE_RANK
    w_small = jnp.concatenate([_pad_cols(w_in[:, c_rkv:c_w], RANK_PAD),
                               _pad_cols(w_in[:, c_w:c_a], RANK_PAD),
                               w_in[:, c_a:c_g]], axis=1)
    w_rwkv = jnp.concatenate([w_in[:, :c_rkv], w_small], axis=1).astype(bf)
    proj = matmul(xn, w_rwkv, f32, 1024, 512)
    q_scale = DIFF_D ** -0.5 * math.log2(math.e)
    col_scale = jnp.concatenate([jnp.full((1, D_DIFF), q_scale, f32), jnp.ones((1, 2 * D_DIFF), f32)], axis=1)
    qkv = matmul_col_scaled(xn, w_in[:, c_g:].astype(bf), col_scale, bf, 1024, 512)

    def small_vec(a):
        return jnp.concatenate([_pad_cols(a[None, c_rkv:c_w], RANK_PAD),
                                _pad_cols(a[None, c_w:c_a], RANK_PAD),
                                a[None, c_a:c_g]], axis=1)

    def pad_rows(w):
        return jnp.pad(w, ((0, RANK_PAD - w.shape[0]), (0, 0))).astype(bf)

    mu_p, mu_n = prm["mu_prev"][l], prm["mu_next"][l]
    vecs = [prm[n][l].reshape(1, D_RWKV) for n in ("w0_f", "w0_b", "a0_f", "a0_b", "k_k", "k_a", "r_k")]
    mats = [pad_rows(prm["w2_f"][l]), pad_rows(prm["w2_b"][l]),
            pad_rows(prm["a2_f"][l]), pad_rows(prm["a2_b"][l]), prm["g2"][l].astype(bf)]
    prep = rwkv_prep(proj[:, :c_rkv], proj[:, c_rkv:],
                     [mu_p[None, :c_rkv], mu_n[None, :c_rkv], small_vec(mu_p), small_vec(mu_n)]
                     + vecs + mats)
    fwd, bwd, (v_r, gate, bonus) = prep[0:5], prep[5:10], prep[10:13]
    yf, yb = rwkv_scan(fwd, bwd, v_r)
    y_a = rwkv_post(yf, yb, bonus, gate, prm["gn_w"][l], prm["gn_b"][l])

    lam = (jnp.exp(jnp.sum(prm["lq1"][l] * prm["lk1"][l]))
           - jnp.exp(jnp.sum(prm["lq2"][l] * prm["lk2"][l])) + lam_init).reshape(1)
    y_b = diff_attention(qkv[:, :D_DIFF].T, qkv[:, D_DIFF:2 * D_DIFF], qkv[:, 2 * D_DIFF:].T,
                         lam, prm["subln_g"][l], lam_init)

    mix = matmul(jnp.concatenate([y_a, y_b], axis=1), prm["w_out"][l].astype(bf), f32, 1024, 512)
    h, xn = resid_norm(h, mix, prm["mix_post_g"][l], prm["ffn2_pre_g"][l], 1.0, bf)

    f2 = swiglu_ffn(xn, prm["ffn2_w_gate"][l].astype(bf), prm["ffn2_w_up"][l].astype(bf),
                    prm["ffn2_w_down"][l].astype(bf), 1024, 256, 512, 512)
    return resid_norm(h, f2, prm["ffn2_post_g"][l], prm["final_g"][l], FFN_RESIDUAL, f32)


def kernel(x, ffn1_pre_g, ffn1_w_gate, ffn1_w_up, ffn1_w_down, ffn1_post_g, mix_pre_g, w_in, mu_prev, mu_next, w0_f, w2_f, w0_b, w2_b, a0_f, a2_f, a0_b, a2_b, g2, k_k, k_a, r_k, gn_w, gn_b, lq1, lk1, lq2, lk2, subln_g, w_out, mix_post_g, ffn2_pre_g, ffn2_w_gate, ffn2_w_up, ffn2_w_down, ffn2_post_g, final_g):
    prm = dict(locals())
    bsz, t, d = x.shape
    depth = ffn1_pre_g.shape[0]
    outs = []
    for b in range(bsz):
        h = x[b]
        for l in range(depth):
            lam_init = 0.8 - 0.6 * math.exp(-0.3 * l)
            x_norm = rms_norm_cast(h, ffn1_pre_g[l], jnp.bfloat16)
            _, h = _layer(h, l, lam_init, x_norm, prm)
        outs.append(h)
    return jnp.stack(outs, axis=0)
```

```python
import functools
import math

import jax
import jax.numpy as jnp
import numpy as np
from jax import lax
from jax.experimental import pallas as pl
from jax.experimental.pallas import tpu as pltpu

D_MODEL = 4096
D_RWKV = 2048
D_DIFF = 2048
RWKV_HEAD = 64
DECAY_RANK = 96
ICLR_RANK = 96
GATE_RANK = 256
DIFF_D = 64
DIFF_HEAD = 128
N_DIFF_HEADS = 16
ALIBI_MAX_EXP = 8.0
NORM_EPS = 1e-6
RWKV_GN_EPS = 64e-5
SUBLN_EPS = 1e-5
FFN_RESIDUAL = 0.5

LANES = 128
SUBLANES = 8
VMEM_LIMIT = 56 * 1024 * 1024

RANK_PAD = 128
SMALL_COLS = 2 * RANK_PAD + GATE_RANK
CHUNK = 64
PAIR = 2 * RWKV_HEAD
HI = lax.Precision.HIGHEST


def _cparams(n_axes):
    return pltpu.CompilerParams(
        dimension_semantics=("arbitrary",) * n_axes, vmem_limit_bytes=VMEM_LIMIT)


def _dot(a, b, dims=(((1,), (0,)), ((), ())), precision=None):
    return lax.dot_general(a, b, dims, precision=precision,
                           preferred_element_type=jnp.float32)


_NT = (((1,), (1,)), ((), ()))
_TN = (((0,), (0,)), ((), ()))


def _rms(x, g, eps):
    return x * lax.rsqrt(jnp.mean(x * x, axis=-1, keepdims=True) + eps) * g


def _norm_kernel(x_ref, g_ref, o_ref):
    o_ref[...] = _rms(x_ref[...], g_ref[...], NORM_EPS).astype(o_ref.dtype)


def rms_norm_cast(x, g, out_dtype, tm=256):
    m, d = x.shape
    return pl.pallas_call(
        _norm_kernel,
        grid=(m // tm,),
        in_specs=[pl.BlockSpec((tm, d), lambda i: (i, 0)),
                  pl.BlockSpec((1, d), lambda i: (0, 0))],
        out_specs=pl.BlockSpec((tm, d), lambda i: (i, 0)),
        out_shape=jax.ShapeDtypeStruct((m, d), out_dtype),
        compiler_params=_cparams(1),
        name="rms_norm",
    )(x, g.reshape(1, d))


def _resid_norm_kernel(h_ref, f_ref, gp_ref, gn_ref, h_out_ref, n_out_ref, *, scale):
    h = h_ref[...] + scale * _rms(f_ref[...], gp_ref[...], NORM_EPS)
    h_out_ref[...] = h
    n_out_ref[...] = _rms(h, gn_ref[...], NORM_EPS).astype(n_out_ref.dtype)


def resid_norm(h, f, g_post, g_next, scale, out_dtype, tm=256):
    m, d = h.shape
    row = pl.BlockSpec((tm, d), lambda i: (i, 0))
    vec = pl.BlockSpec((1, d), lambda i: (0, 0))
    return pl.pallas_call(
        functools.partial(_resid_norm_kernel, scale=scale),
        grid=(m // tm,),
        in_specs=[row, row, vec, vec],
        out_specs=[row, row],
        out_shape=[jax.ShapeDtypeStruct((m, d), jnp.float32),
                   jax.ShapeDtypeStruct((m, d), out_dtype)],
        compiler_params=_cparams(1),
        name="resid_norm",
    )(h, f, g_post.reshape(1, d), g_next.reshape(1, d))


def _mm_kernel(x_ref, w_ref, o_ref):
    o_ref[...] = _dot(x_ref[...], w_ref[...]).astype(o_ref.dtype)


def matmul(x, w, out_dtype, tm, tn):
    m, k = x.shape
    _, n = w.shape
    return pl.pallas_call(
        _mm_kernel,
        grid=(n // tn, m // tm),
        in_specs=[pl.BlockSpec((tm, k), lambda j, i: (i, 0)),
                  pl.BlockSpec((k, tn), lambda j, i: (0, j))],
        out_specs=pl.BlockSpec((tm, tn), lambda j, i: (i, j)),
        out_shape=jax.ShapeDtypeStruct((m, n), out_dtype),
        compiler_params=_cparams(2),
        name="matmul",
    )(x, w)


def _mm_two_kernel(xa_ref, xb_ref, wa_ref, wb_ref, o_ref):
    o_ref[...] = (_dot(xa_ref[...], wa_ref[...]) + _dot(xb_ref[...], wb_ref[...])).astype(o_ref.dtype)


def matmul_two(xa, xb, wa, wb, out_dtype, tm, tn):
    m, ka = xa.shape
    _, kb = xb.shape
    _, n = wa.shape
    return pl.pallas_call(
        _mm_two_kernel,
        grid=(n // tn, m // tm),
        in_specs=[pl.BlockSpec((tm, ka), lambda j, i: (i, 0)),
                  pl.BlockSpec((tm, kb), lambda j, i: (i, 0)),
                  pl.BlockSpec((ka, tn), lambda j, i: (0, j)),
                  pl.BlockSpec((kb, tn), lambda j, i: (0, j))],
        out_specs=pl.BlockSpec((tm, tn), lambda j, i: (i, j)),
        out_shape=jax.ShapeDtypeStruct((m, n), out_dtype),
        compiler_params=_cparams(2),
        name="matmul_two",
    )(xa, xb, wa, wb)


def _mm_scaled_kernel(x_ref, w_ref, cs_ref, o_ref):
    o_ref[...] = (_dot(x_ref[...], w_ref[...]) * cs_ref[...]).astype(o_ref.dtype)


def matmul_col_scaled(x, w, col_scale, out_dtype, tm, tn):
    m, k = x.shape
    _, n = w.shape
    return pl.pallas_call(
        _mm_scaled_kernel,
        grid=(n // tn, m // tm),
        in_specs=[pl.BlockSpec((tm, k), lambda j, i: (i, 0)),
                  pl.BlockSpec((k, tn), lambda j, i: (0, j)),
                  pl.BlockSpec((1, tn), lambda j, i: (0, j))],
        out_specs=pl.BlockSpec((tm, tn), lambda j, i: (i, j)),
        out_shape=jax.ShapeDtypeStruct((m, n), out_dtype),
        compiler_params=_cparams(2),
        name="matmul_col_scaled",
    )(x, w, col_scale)


def _gate_up_kernel(x_ref, wg_ref, wu_ref, o_ref, wg_bf_ref, wu_bf_ref):
    @pl.when(pl.program_id(1) == 0)
    def _():
        wg_bf_ref[...] = wg_ref[...].astype(jnp.bfloat16)
        wu_bf_ref[...] = wu_ref[...].astype(jnp.bfloat16)

    x = x_ref[...]
    g = _dot(x, wg_bf_ref[...])
    u = _dot(x, wu_bf_ref[...])
    o_ref[...] = (g * jax.nn.sigmoid(g) * u).astype(o_ref.dtype)


def gate_up(x, wg, wu, tm, tn):
    m, k = x.shape
    _, n = wg.shape
    wspec = pl.BlockSpec((k, tn), lambda j, i: (0, j))
    return pl.pallas_call(
        _gate_up_kernel,
        grid=(n // tn, m // tm),
        in_specs=[pl.BlockSpec((tm, k), lambda j, i: (i, 0)), wspec, wspec],
        out_specs=pl.BlockSpec((tm, tn), lambda j, i: (i, j)),
        out_shape=jax.ShapeDtypeStruct((m, n), jnp.bfloat16),
        scratch_shapes=[pltpu.VMEM((k, tn), jnp.bfloat16)] * 2,
        compiler_params=_cparams(2),
        name="gate_up",
    )(x, wg, wu)


def swiglu_ffn(xn, wg, wu, wd, tm_gu, tn_gu, tm_d, tn_d):
    a = gate_up(xn, wg, wu, tm_gu, tn_gu)
    return matmul(a, wd, jnp.float32, tm_d, tn_d)


def _head_sum(x, ones_bd):
    parts = []
    for s in range(x.shape[1] // LANES):
        parts.append(_dot(x[:, s * LANES:(s + 1) * LANES], ones_bd, precision=HI))
    return jnp.concatenate(parts, axis=1)


def _token_shift(p_ref, pp_ref, pn_ref, mup_ref, mun_ref, first, last):
    p = p_ref[...]
    rows = p.shape[0]
    rid = lax.broadcasted_iota(jnp.int32, p.shape, 0)
    prev_row = jnp.where(first, 0.0, pp_ref[SUBLANES - 1:SUBLANES, :])
    next_row = jnp.where(last, 0.0, pn_ref[0:1, :])
    p_prev = jnp.where(rid == 0, prev_row, pltpu.roll(p, 1, 0))
    p_next = jnp.where(rid == rows - 1, next_row, pltpu.roll(p, rows - 1, 0))
    return p + mup_ref[...] * (p_prev - p) + mun_ref[...] * (p_next - p)


def _rwkv_prep_kernel(
        p_ref, pp_ref, pn_ref, s_ref, sp_ref, sn_ref,
        mup_ref, mun_ref, mups_ref, muns_ref,
        w0f_ref, w0b_ref, a0f_ref, a0b_ref, kk_ref, ka_ref, rk_ref,
        w2f_ref, w2b_ref, a2f_ref, a2b_ref, g2_ref,
        atf_ref, btf_ref, ktf_ref, rtf_ref, glf_ref,
        atb_ref, btb_ref, ktb_ref, rtb_ref, glb_ref,
        v_ref, gate_ref, bonus_ref):
    i = pl.program_id(0)
    first = i == 0
    last = i == pl.num_programs(0) - 1
    tm = p_ref.shape[0]
    n_chunks = tm // CHUNK

    p = _token_shift(p_ref, pp_ref, pn_ref, mup_ref, mun_ref, first, last)
    s = _token_shift(s_ref, sp_ref, sn_ref, mups_ref, muns_ref, first, last)
    r = p[:, 0:D_RWKV]
    k = p[:, D_RWKV:2 * D_RWKV]
    v = p[:, 2 * D_RWKV:3 * D_RWKV]
    hw = jnp.tanh(s[:, 0:RANK_PAD]).astype(jnp.bfloat16)
    xa = s[:, RANK_PAD:2 * RANK_PAD].astype(jnp.bfloat16)
    sg = jax.nn.sigmoid(s[:, 2 * RANK_PAD:]).astype(jnp.bfloat16)

    li = lax.broadcasted_iota(jnp.int32, (LANES, LANES), 0) // RWKV_HEAD
    lj = lax.broadcasted_iota(jnp.int32, (LANES, LANES), 1) // RWKV_HEAD
    ones_bd = (li == lj).astype(jnp.float32)

    kk = k * kk_ref[...]
    kk = kk / jnp.maximum(jnp.sqrt(_head_sum(kk * kk, ones_bd)), 1e-12)

    ti = lax.broadcasted_iota(jnp.int32, (tm, tm), 0)
    tj = lax.broadcasted_iota(jnp.int32, (tm, tm), 1)
    same_chunk = (ti // CHUNK) == (tj // CHUNK)

    k_sum = jnp.zeros_like(k)
    dirs = ((w0f_ref, w2f_ref, a0f_ref, a2f_ref, atf_ref, btf_ref, ktf_ref, rtf_ref, glf_ref, False),
            (w0b_ref, w2b_ref, a0b_ref, a2b_ref, atb_ref, btb_ref, ktb_ref, rtb_ref, glb_ref, True))
    for w0_ref, w2_ref, a0_ref, a2_ref, at_ref, bt_ref, kt_ref, rt_ref, gl_ref, rev in dirs:
        z = w0_ref[...] + _dot(hw, w2_ref[...])
        lw = -math.exp(-0.5) * jax.nn.sigmoid(z)
        a = jax.nn.sigmoid(a0_ref[...] + _dot(xa, a2_ref[...]))
        k_dir = k * (1.0 + (a - 1.0) * ka_ref[...])
        k_sum = k_sum + k_dir
        order = (tj >= ti) if rev else (tj <= ti)
        tri = jnp.where(same_chunk & order, 1.0, 0.0)
        c = _dot(tri, lw, precision=HI)
        e_neg = jnp.exp(-c)
        at_ref[...] = (-kk * jnp.exp(c - lw)).astype(at_ref.dtype)
        bt_ref[...] = (kk * a * e_neg).astype(bt_ref.dtype)
        kt_ref[...] = (k_dir * e_neg).astype(kt_ref.dtype)
        e_pos = jnp.exp(c)
        rt_ref[...] = (r * e_pos).astype(rt_ref.dtype)
        for j in range(n_chunks):
            end = j * CHUNK if rev else (j + 1) * CHUNK - 1
            gl_ref[j] = e_pos[end:end + 1, :]

    v_ref[...] = v.astype(v_ref.dtype)
    gate_ref[...] = _dot(sg, g2_ref[...])
    bonus_ref[...] = _head_sum(r * k_sum * rk_ref[...], ones_bd) * v


def rwkv_prep(proj, params, tm=128):
    t = proj.shape[0]
    nb = t // tm
    hb = tm // SUBLANES
    n_halo = t // SUBLANES
    wide = 3 * D_RWKV
    small_blk = wide // SMALL_COLS

    def row(c, cb=0):
        return pl.BlockSpec((tm, c), lambda i: (i, cb))

    def halo_prev(c, cb=0):
        return pl.BlockSpec((SUBLANES, c), lambda i: (jnp.maximum(i * hb - 1, 0), cb))

    def halo_next(c, cb=0):
        return pl.BlockSpec((SUBLANES, c), lambda i: (jnp.minimum((i + 1) * hb, n_halo - 1), cb))

    def vec(c):
        return pl.BlockSpec((1, c), lambda i: (0, 0))

    def mat(r_, c):
        return pl.BlockSpec((r_, c), lambda i: (0, 0))

    gl_spec = pl.BlockSpec((tm // CHUNK, 1, D_RWKV), lambda i: (i, 0, 0))
    big = jax.ShapeDtypeStruct((t, D_RWKV), jnp.float32)
    gl = jax.ShapeDtypeStruct((t // CHUNK, 1, D_RWKV), jnp.float32)
    in_specs = [row(wide), halo_prev(wide), halo_next(wide),
                row(SMALL_COLS, small_blk), halo_prev(SMALL_COLS, small_blk), halo_next(SMALL_COLS, small_blk),
                vec(wide), vec(wide), vec(SMALL_COLS), vec(SMALL_COLS)]
    in_specs += [vec(D_RWKV)] * 7
    in_specs += [mat(RANK_PAD, D_RWKV)] * 4 + [mat(GATE_RANK, D_RWKV)]
    dir_specs = [row(D_RWKV)] * 4 + [gl_spec]
    big_bf = jax.ShapeDtypeStruct((t, D_RWKV), jnp.bfloat16)
    dir_shapes = [big_bf] * 4 + [gl]
    return pl.pallas_call(
        _rwkv_prep_kernel,
        grid=(nb,),
        in_specs=in_specs,
        out_specs=dir_specs + dir_specs + [row(D_RWKV)] * 3,
        out_shape=dir_shapes + dir_shapes + [big_bf, big, big],
        compiler_params=_cparams(1),
        name="rwkv_prep",
    )(proj, proj, proj, proj, proj, proj, *params)


def _stack(x, lane_head):
    zero = jnp.zeros_like(x)
    return jnp.concatenate([jnp.where(lane_head == 0, x, zero),
                            jnp.where(lane_head == 1, x, zero)], axis=0)


def _scan_masks(rev):
    n2 = 2 * CHUNK
    ri = lax.broadcasted_iota(jnp.int32, (n2, n2), 0)
    ci = lax.broadcasted_iota(jnp.int32, (n2, n2), 1)
    same = (ri // CHUNK) == (ci // CHUNK)
    before = (ci > ri) if rev else (ci < ri)
    strict = same & before
    incl = same & (before | (ri == ci))
    eye = jnp.where(ri == ci, 1.0, 0.0)
    return strict, jnp.concatenate([incl, incl], axis=1), eye


def _scan_prepare(units):
    bf = jnp.bfloat16
    n2 = 2 * CHUNK
    lane_head = lax.broadcasted_iota(jnp.int32, (CHUNK, PAIR), 1) // RWKV_HEAD
    stacked = [[_stack(x, lane_head) for x in u[:5]] for u in units]
    bks = [jnp.concatenate([s[1], s[2]], axis=0) for s in stacked]
    g_as = [_dot(s[0], bk, _NT) for s, bk in zip(stacked, bks)]
    g_rs = [_dot(s[3], bk, _NT) for s, bk in zip(stacked, bks)]
    xs = [jnp.where(u[6][0], g[:, :n2], 0.0) for u, g in zip(units, g_as)]
    a_aks = [jnp.where(u[6][0], g[:, n2:], 0.0).astype(bf) for u, g in zip(units, g_as)]
    a_rs = [jnp.where(u[6][1], g, 0.0).astype(bf) for u, g in zip(units, g_rs)]
    akv = [_dot(a, s[4]).astype(bf) for a, s in zip(a_aks, stacked)]

    tinvs = [u[6][2] + x for u, x in zip(units, xs)]
    for _ in range(int(math.log2(CHUNK)) - 1):
        xbs = [x.astype(bf) for x in xs]
        xs = [_dot(xb, xb) for xb in xbs]
        tinvs = [t + _dot(t.astype(bf), x.astype(bf)) for t, x in zip(tinvs, xs)]

    wqs = [_dot(t.astype(bf), jnp.concatenate([s[0], kv], axis=1))
           for t, s, kv in zip(tinvs, stacked, akv)]
    out = []
    for u, s, bk, wq, a_r in zip(units, stacked, bks, wqs, a_rs):
        gl = u[5]
        w = wq[:, :PAIR].astype(bf)
        q = wq[:, PAIR:]
        bkg = (bk * gl).astype(bf)
        m = _dot(w, bkg[:n2], _TN)
        n = _dot(jnp.concatenate([q.astype(bf), s[4]], axis=0), bkg, _TN)
        out.append((w, q, s[3], a_r, s[4], m.astype(bf), n, gl))
    return out


def _scan_output(prep, s0b):
    w, q, r2, a_r, v2 = prep[:5]
    u = (_dot(w, s0b, _NT) + q).astype(jnp.bfloat16)
    y2 = _dot(r2, s0b, _NT) + _dot(a_r, jnp.concatenate([u, v2], axis=0))
    return y2[:CHUNK] + y2[CHUNK:]


def _rwkv_scan_kernel(atf_ref, btf_ref, ktf_ref, rtf_ref, glf_ref, vf_ref,
                      atb_ref, btb_ref, ktb_ref, rtb_ref, glb_ref, vb_ref,
                      yf_ref, yb_ref, sf_ref, sb_ref, *, n_blk):
    @pl.when(pl.program_id(1) == 0)
    def _():
        sf_ref[...] = jnp.zeros_like(sf_ref)
        sb_ref[...] = jnp.zeros_like(sb_ref)

    def rows(j):
        return pl.ds(j * CHUNK, CHUNK)

    masks_f = _scan_masks(False)
    masks_b = _scan_masks(True)
    units = []
    for j in range(n_blk):
        units.append((atf_ref[rows(j), :], btf_ref[rows(j), :], ktf_ref[rows(j), :],
                      rtf_ref[rows(j), :], vf_ref[rows(j), :], glf_ref[j], masks_f))
        units.append((atb_ref[rows(j), :], btb_ref[rows(j), :], ktb_ref[rows(j), :],
                      rtb_ref[rows(j), :], vb_ref[rows(j), :], glb_ref[j], masks_b))
    prep = _scan_prepare(units)
    prep_f, prep_b = prep[0::2], prep[1::2]

    sf = sf_ref[...]
    sb = sb_ref[...]
    for j in range(n_blk):
        jb = n_blk - 1 - j
        pf, pb = prep_f[j], prep_b[jb]
        sfb = sf.astype(jnp.bfloat16)
        sbb = sb.astype(jnp.bfloat16)
        sf = sf * pf[7] + _dot(sfb, pf[5]) + pf[6]
        sb = sb * pb[7] + _dot(sbb, pb[5]) + pb[6]
        yf_ref[rows(j), :] = _scan_output(pf, sfb)
        yb_ref[rows(jb), :] = _scan_output(pb, sbb)
    sf_ref[...] = sf
    sb_ref[...] = sb


def rwkv_scan(fwd, bwd, v, n_blk=4):
    t = v.shape[0]
    nb = t // (CHUNK * n_blk)
    n_pairs = D_RWKV // PAIR
    f_blk = pl.BlockSpec((CHUNK * n_blk, PAIR), lambda p, c: (c, p))
    b_blk = pl.BlockSpec((CHUNK * n_blk, PAIR), lambda p, c: (nb - 1 - c, p))
    f_gl = pl.BlockSpec((n_blk, 1, PAIR), lambda p, c: (c, 0, p))
    b_gl = pl.BlockSpec((n_blk, 1, PAIR), lambda p, c: (nb - 1 - c, 0, p))
    out = jax.ShapeDtypeStruct((t, D_RWKV), jnp.float32)
    return pl.pallas_call(
        functools.partial(_rwkv_scan_kernel, n_blk=n_blk),
        grid=(n_pairs, nb),
        in_specs=[f_blk] * 4 + [f_gl, f_blk] + [b_blk] * 4 + [b_gl, b_blk],
        out_specs=[f_blk, b_blk],
        out_shape=[out, out],
        scratch_shapes=[pltpu.VMEM((PAIR, PAIR), jnp.float32)] * 2,
        compiler_params=_cparams(2),
        name="rwkv_scan",
    )(*fwd, v, *bwd, v)


def _rwkv_post_kernel(yf_ref, yb_ref, bonus_ref, gate_ref, gw_ref, gb_ref, o_ref):
    li = lax.broadcasted_iota(jnp.int32, (LANES, LANES), 0) // RWKV_HEAD
    lj = lax.broadcasted_iota(jnp.int32, (LANES, LANES), 1) // RWKV_HEAD
    mean_bd = jnp.where(li == lj, 1.0 / RWKV_HEAD, 0.0)
    y = yf_ref[...] + yb_ref[...]
    d = y - _head_sum(y, mean_bd)
    var = _head_sum(d * d, mean_bd)
    yn = d * lax.rsqrt(var + RWKV_GN_EPS) * gw_ref[...] + gb_ref[...]
    o_ref[...] = ((yn + bonus_ref[...]) * gate_ref[...]).astype(o_ref.dtype)


def rwkv_post(yf, yb, bonus, gate, gn_w, gn_b, tm=256):
    t = yf.shape[0]
    row = pl.BlockSpec((tm, D_RWKV), lambda i: (i, 0))
    vec = pl.BlockSpec((1, D_RWKV), lambda i: (0, 0))
    return pl.pallas_call(
        _rwkv_post_kernel,
        grid=(t // tm,),
        in_specs=[row, row, row, row, vec, vec],
        out_specs=row,
        out_shape=jax.ShapeDtypeStruct((t, D_RWKV), jnp.bfloat16),
        compiler_params=_cparams(1),
        name="rwkv_post",
    )(yf, yb, bonus, gate, gn_w.reshape(1, -1), gn_b.reshape(1, -1))


SLOPE_PIECES = 3
SC_PER_HEAD = 1 + SLOPE_PIECES


def _diff_attn_kernel(sc_ref, qt_ref, k_ref, vt_ref, g_ref, o_ref, s_ref, *, tq, tk, lam_init):
    bf = jnp.bfloat16
    h = pl.program_id(0)
    qi = pl.program_id(1)
    n_kv = k_ref.shape[0] // tk
    lam = sc_ref[0]
    base = 1 + h * SC_PER_HEAD
    slope = sc_ref[base]
    pieces = [sc_ref[base + 1 + i] for i in range(SLOPE_PIECES)]

    qt = qt_ref[...]
    qrow = lax.broadcasted_iota(jnp.int32, (DIFF_HEAD, tq), 0)
    feat = jnp.zeros((DIFF_HEAD, tq), jnp.float32)
    for i, pc in enumerate(pieces):
        feat = jnp.where(qrow % DIFF_D == i, 2.0 * pc, feat)
        feat = jnp.where(qrow % DIFF_D == SLOPE_PIECES + i, pc, feat)
    feat = feat.astype(bf)
    own_rows = [qrow < DIFF_D, qrow >= DIFF_D]
    qts = [jnp.where(own, qt, feat) for own in own_rows]

    kl = lax.broadcasted_iota(jnp.int32, (tk, LANES), 1)
    kr = lax.broadcasted_iota(jnp.int32, (tk, LANES), 0)
    kfeat = jnp.where(kl % DIFF_D < SLOPE_PIECES, kr >> 1,
                      jnp.where(kl % DIFF_D < 2 * SLOPE_PIECES, kr & 1, 0)).astype(jnp.float32)
    kfeat_before = kfeat.astype(bf)
    kfeat_after = (-kfeat).astype(bf)
    kzero = jnp.zeros((tk, LANES), bf)
    own_lanes = [kl < DIFF_D, kl >= DIFF_D]

    qpos = qi * tq + lax.broadcasted_iota(jnp.int32, (1, tq), 1)
    q_bias = slope * qpos.astype(jnp.float32)

    def online(carry, s, t_q, vt):
        m, l, acc = carry
        m_new = jnp.maximum(m, jnp.max(s, axis=0, keepdims=True) + t_q)
        alpha = jnp.exp2(m - m_new)
        p = jnp.exp2(s - (m_new - t_q))
        l_new = alpha * l + jnp.sum(p, axis=0, keepdims=True)
        acc_new = alpha * acc + _dot(vt, p.astype(bf))
        return m_new, l_new, acc_new

    jd = (qi * tq) // tk

    def tile_of(n):
        return n + (n >= jd).astype(jnp.int32)

    def scores(n, slot):
        j = tile_of(n)
        start = pl.multiple_of(j * tk, tk)
        kb = k_ref[pl.ds(start, tk), :]
        kf = jnp.where(j < jd, kfeat_before, kfeat_after)
        for mp, (own, q_m) in enumerate(zip(own_lanes, qts)):
            s_ref[slot, mp] = _dot(jnp.where(own, kb, kf), q_m)

    def consume(n, slot, carries):
        j = tile_of(n)
        start = pl.multiple_of(j * tk, tk)
        vt = vt_ref[:, pl.ds(start, tk)]
        sign = jnp.where(j < jd, 1.0, -1.0)
        t_q = sign * (slope * (j * tk).astype(jnp.float32) - q_bias)
        return tuple(online(c, s_ref[slot, mp], t_q, vt) for mp, c in enumerate(carries))

    start_d = pl.multiple_of(jd * tk, tk)
    kb_d = k_ref[pl.ds(start_d, tk), :]
    vt_d = vt_ref[:, pl.ds(start_d, tk)]
    kpos = jd * tk + lax.broadcasted_iota(jnp.int32, (tk, tq), 0)
    bias = slope * jnp.abs(kpos - qpos).astype(jnp.float32)
    init = (jnp.full((1, tq), -jnp.inf, jnp.float32),
            jnp.zeros((1, tq), jnp.float32),
            jnp.zeros((DIFF_HEAD, tq), jnp.float32))
    for mp, (own, q_m) in enumerate(zip(own_lanes, qts)):
        s_ref[1, mp] = _dot(jnp.where(own, kb_d, kzero), q_m) - bias
    scores(jnp.int32(0), 0)
    carries = tuple(online(init, s_ref[1, mp], 0.0, vt_d) for mp in range(2))

    def body(i, carries):
        n = 2 * i
        scores(n + 1, 1)
        carries = consume(n, 0, carries)
        scores(n + 2, 0)
        return consume(n + 1, 1, carries)

    n_off = n_kv - 1
    carries = lax.fori_loop(0, n_off // 2, body, carries)
    (_, l0, acc0), (_, l1, acc1) = consume(jnp.int32(n_off - 1), 0, carries)
    o = (acc0 / l0 - lam * (acc1 / l1)).T
    o_ref[...] = (_rms(o, g_ref[...], SUBLN_EPS) * (1.0 - lam_init)).astype(o_ref.dtype)


def _bf16_pieces(x, n):
    out = []
    for _ in range(n):
        p = (x.view(np.uint32) & np.uint32(0xFFFF0000)).view(np.float32)
        out.append(p)
        x = (x - p).astype(np.float32)
    return out


def diff_attention(qt, k, vt, lam, subln_g, lam_init, tq=512, tk=512):
    t = k.shape[0]
    assert tk % tq == 0 and tk // 2 <= 256
    assert t % (2 * tk) == 0
    slopes = 2.0 ** (-ALIBI_MAX_EXP * np.arange(1, N_DIFF_HEADS + 1, dtype=np.float64) / N_DIFF_HEADS)
    slopes = (slopes * math.log2(math.e)).astype(np.float32)
    pieces = _bf16_pieces(slopes, SLOPE_PIECES)
    per_head = np.stack([sum(pieces)] + pieces, axis=1)
    lam = jnp.concatenate([lam, jnp.asarray(per_head.reshape(-1))])
    return pl.pallas_call(
        functools.partial(_diff_attn_kernel, tq=tq, tk=tk, lam_init=lam_init),
        grid=(N_DIFF_HEADS, t // tq),
        in_specs=[pl.BlockSpec(memory_space=pltpu.SMEM),
                  pl.BlockSpec((DIFF_HEAD, tq), lambda h, i: (h, i)),
                  pl.BlockSpec((t, DIFF_HEAD), lambda h, i: (0, h)),
                  pl.BlockSpec((DIFF_HEAD, t), lambda h, i: (h, 0)),
                  pl.BlockSpec((1, DIFF_HEAD), lambda h, i: (0, 0))],
        out_specs=pl.BlockSpec((tq, DIFF_HEAD), lambda h, i: (i, h)),
        out_shape=jax.ShapeDtypeStruct((t, D_DIFF), jnp.bfloat16),
        scratch_shapes=[pltpu.VMEM((2, 2, tk, tq), jnp.float32)],
        compiler_params=_cparams(2),
        name="diff_attn",
    )(lam, qt, k, vt, subln_g.reshape(1, DIFF_HEAD))


def _pad_cols(w, to):
    return jnp.pad(w, ((0, 0), (0, to - w.shape[1])))


def _layer(h, l, lam_init, x_norm, prm):
    bf = jnp.bfloat16
    f32 = jnp.float32
    f1 = swiglu_ffn(x_norm, prm["ffn1_w_gate"][l], prm["ffn1_w_up"][l],
                    prm["ffn1_w_down"][l].astype(bf), 1024, 256, 512, 512)
    h, xn = resid_norm(h, f1, prm["ffn1_post_g"][l], prm["mix_pre_g"][l], FFN_RESIDUAL, bf)

    w_in = prm["w_in"][l]
    c_rkv = 3 * D_RWKV
    c_w = c_rkv + DECAY_RANK
    c_a = c_w + ICLR_RANK
    c_g = c_a + GATE_RANK
    w_small = jnp.concatenate([_pad_cols(w_in[:, c_rkv:c_w], RANK_PAD),
                               _pad_cols(w_in[:, c_w:c_a], RANK_PAD),
                               w_in[:, c_a:c_g]], axis=1)
    w_rwkv = jnp.concatenate([w_in[:, :c_rkv], w_small], axis=1).astype(bf)
    proj = matmul(xn, w_rwkv, f32, 1024, 512)
    q_scale = DIFF_D ** -0.5 * math.log2(math.e)
    col_scale = jnp.concatenate([jnp.full((1, D_DIFF), q_scale, f32), jnp.ones((1, 2 * D_DIFF), f32)], axis=1)
    qkv = matmul_col_scaled(xn, w_in[:, c_g:].astype(bf), col_scale, bf, 1024, 512)

    def small_vec(a):
        return jnp.concatenate([_pad_cols(a[None, c_rkv:c_w], RANK_PAD),
                                _pad_cols(a[None, c_w:c_a], RANK_PAD),
                                a[None, c_a:c_g]], axis=1)

    def pad_rows(w):
        return jnp.pad(w, ((0, RANK_PAD - w.shape[0]), (0, 0))).astype(bf)

    mu_p, mu_n = prm["mu_prev"][l], prm["mu_next"][l]
    vecs = [prm[n][l].reshape(1, D_RWKV) for n in ("w0_f", "w0_b", "a0_f", "a0_b", "k_k", "k_a", "r_k")]
    mats = [pad_rows(prm["w2_f"][l]), pad_rows(prm["w2_b"][l]),
            pad_rows(prm["a2_f"][l]), pad_rows(prm["a2_b"][l]), prm["g2"][l].astype(bf)]
    prep = rwkv_prep(proj, [mu_p[None, :c_rkv], mu_n[None, :c_rkv], small_vec(mu_p), small_vec(mu_n)]
                     + vecs + mats)
    fwd, bwd, (v_r, gate, bonus) = prep[0:5], prep[5:10], prep[10:13]
    yf, yb = rwkv_scan(fwd, bwd, v_r)
    y_a = rwkv_post(yf, yb, bonus, gate, prm["gn_w"][l], prm["gn_b"][l])

    lam = (jnp.exp(jnp.sum(prm["lq1"][l] * prm["lk1"][l]))
           - jnp.exp(jnp.sum(prm["lq2"][l] * prm["lk2"][l])) + lam_init).reshape(1)
    y_b = diff_attention(qkv[:, :D_DIFF].T, qkv[:, D_DIFF:2 * D_DIFF], qkv[:, 2 * D_DIFF:].T,
                         lam, prm["subln_g"][l], lam_init)

    w_out = prm["w_out"][l]
    mix = matmul_two(y_a, y_b, w_out[:D_RWKV].astype(bf), w_out[D_RWKV:].astype(bf), f32, 1024, 512)
    h, xn = resid_norm(h, mix, prm["mix_post_g"][l], prm["ffn2_pre_g"][l], 1.0, bf)

    f2 = swiglu_ffn(xn, prm["ffn2_w_gate"][l], prm["ffn2_w_up"][l],
                    prm["ffn2_w_down"][l].astype(bf), 1024, 256, 512, 512)
    return resid_norm(h, f2, prm["ffn2_post_g"][l], prm["final_g"][l], FFN_RESIDUAL, f32)


def kernel(x, ffn1_pre_g, ffn1_w_gate, ffn1_w_up, ffn1_w_down, ffn1_post_g, mix_pre_g, w_in, mu_prev, mu_next, w0_f, w2_f, w0_b, w2_b, a0_f, a2_f, a0_b, a2_b, g2, k_k, k_a, r_k, gn_w, gn_b, lq1, lk1, lq2, lk2, subln_g, w_out, mix_post_g, ffn2_pre_g, ffn2_w_gate, ffn2_w_up, ffn2_w_down, ffn2_post_g, final_g):
    prm = dict(locals())
    bsz, t, d = x.shape
    depth = ffn1_pre_g.shape[0]
    outs = []
    for b in range(bsz):
        h = x[b]
        for l in range(depth):
            lam_init = 0.8 - 0.6 * math.exp(-0.3 * l)
            x_norm = rms_norm_cast(h, ffn1_pre_g[l], jnp.bfloat16)
            _, h = _layer(h, l, lam_init, x_norm, prm)
        outs.append(h)
    return jnp.stack(outs, axis=0)
```

```python
import functools
import math

import jax
import jax.numpy as jnp
import numpy as np
from jax import lax
from jax.experimental import pallas as pl
from jax.experimental.pallas import tpu as pltpu

D_MODEL = 4096
D_RWKV = 2048
D_DIFF = 2048
RWKV_HEAD = 64
DECAY_RANK = 96
ICLR_RANK = 96
GATE_RANK = 256
DIFF_D = 64
DIFF_HEAD = 128
N_DIFF_HEADS = 16
ALIBI_MAX_EXP = 8.0
NORM_EPS = 1e-6
RWKV_GN_EPS = 64e-5
SUBLN_EPS = 1e-5
FFN_RESIDUAL = 0.5

LANES = 128
SUBLANES = 8
VMEM_LIMIT = 56 * 1024 * 1024

RANK_PAD = 128
SMALL_COLS = 2 * RANK_PAD + GATE_RANK
CHUNK = 64
PAIR = 2 * RWKV_HEAD
HI = lax.Precision.HIGHEST


def _cparams(n_axes):
    return pltpu.CompilerParams(
        dimension_semantics=("arbitrary",) * n_axes, vmem_limit_bytes=VMEM_LIMIT)


def _dot(a, b, dims=(((1,), (0,)), ((), ())), precision=None):
    return lax.dot_general(a, b, dims, precision=precision,
                           preferred_element_type=jnp.float32)


_NT = (((1,), (1,)), ((), ()))
_TN = (((0,), (0,)), ((), ()))


def _rms(x, g, eps):
    return x * lax.rsqrt(jnp.mean(x * x, axis=-1, keepdims=True) + eps) * g


def _norm_kernel(x_ref, g_ref, o_ref):
    o_ref[...] = _rms(x_ref[...], g_ref[...], NORM_EPS).astype(o_ref.dtype)


def rms_norm_cast(x, g, out_dtype, tm=256):
    m, d = x.shape
    return pl.pallas_call(
        _norm_kernel,
        grid=(m // tm,),
        in_specs=[pl.BlockSpec((tm, d), lambda i: (i, 0)),
                  pl.BlockSpec((1, d), lambda i: (0, 0))],
        out_specs=pl.BlockSpec((tm, d), lambda i: (i, 0)),
        out_shape=jax.ShapeDtypeStruct((m, d), out_dtype),
        compiler_params=_cparams(1),
        name="rms_norm",
    )(x, g.reshape(1, d))


def _resid_norm_kernel(h_ref, f_ref, gp_ref, gn_ref, h_out_ref, n_out_ref, *, scale):
    h = h_ref[...] + scale * _rms(f_ref[...], gp_ref[...], NORM_EPS)
    h_out_ref[...] = h
    n_out_ref[...] = _rms(h, gn_ref[...], NORM_EPS).astype(n_out_ref.dtype)


def resid_norm(h, f, g_post, g_next, scale, out_dtype, tm=256):
    m, d = h.shape
    row = pl.BlockSpec((tm, d), lambda i: (i, 0))
    vec = pl.BlockSpec((1, d), lambda i: (0, 0))
    return pl.pallas_call(
        functools.partial(_resid_norm_kernel, scale=scale),
        grid=(m // tm,),
        in_specs=[row, row, vec, vec],
        out_specs=[row, row],
        out_shape=[jax.ShapeDtypeStruct((m, d), jnp.float32),
                   jax.ShapeDtypeStruct((m, d), out_dtype)],
        compiler_params=_cparams(1),
        name="resid_norm",
    )(h, f, g_post.reshape(1, d), g_next.reshape(1, d))


def _mm_kernel(x_ref, w_ref, o_ref):
    o_ref[...] = _dot(x_ref[...], w_ref[...]).astype(o_ref.dtype)


def matmul(x, w, out_dtype, tm, tn):
    m, k = x.shape
    _, n = w.shape
    return pl.pallas_call(
        _mm_kernel,
        grid=(n // tn, m // tm),
        in_specs=[pl.BlockSpec((tm, k), lambda j, i: (i, 0)),
                  pl.BlockSpec((k, tn), lambda j, i: (0, j))],
        out_specs=pl.BlockSpec((tm, tn), lambda j, i: (i, j)),
        out_shape=jax.ShapeDtypeStruct((m, n), out_dtype),
        compiler_params=_cparams(2),
        name="matmul",
    )(x, w)


def _mm_two_kernel(xa_ref, xb_ref, wa_ref, wb_ref, o_ref):
    o_ref[...] = (_dot(xa_ref[...], wa_ref[...]) + _dot(xb_ref[...], wb_ref[...])).astype(o_ref.dtype)


def matmul_two(xa, xb, wa, wb, out_dtype, tm, tn):
    m, ka = xa.shape
    _, kb = xb.shape
    _, n = wa.shape
    return pl.pallas_call(
        _mm_two_kernel,
        grid=(n // tn, m // tm),
        in_specs=[pl.BlockSpec((tm, ka), lambda j, i: (i, 0)),
                  pl.BlockSpec((tm, kb), lambda j, i: (i, 0)),
                  pl.BlockSpec((ka, tn), lambda j, i: (0, j)),
                  pl.BlockSpec((kb, tn), lambda j, i: (0, j))],
        out_specs=pl.BlockSpec((tm, tn), lambda j, i: (i, j)),
        out_shape=jax.ShapeDtypeStruct((m, n), out_dtype),
        compiler_params=_cparams(2),
        name="matmul_two",
    )(xa, xb, wa, wb)


def _mm_scaled_kernel(x_ref, w_ref, cs_ref, o_ref):
    o_ref[...] = (_dot(x_ref[...], w_ref[...]) * cs_ref[...]).astype(o_ref.dtype)


def matmul_col_scaled(x, w, col_scale, out_dtype, tm, tn):
    m, k = x.shape
    _, n = w.shape
    return pl.pallas_call(
        _mm_scaled_kernel,
        grid=(n // tn, m // tm),
        in_specs=[pl.BlockSpec((tm, k), lambda j, i: (i, 0)),
                  pl.BlockSpec((k, tn), lambda j, i: (0, j)),
                  pl.BlockSpec((1, tn), lambda j, i: (0, j))],
        out_specs=pl.BlockSpec((tm, tn), lambda j, i: (i, j)),
        out_shape=jax.ShapeDtypeStruct((m, n), out_dtype),
        compiler_params=_cparams(2),
        name="matmul_col_scaled",
    )(x, w, col_scale)


def _gate_up_kernel(x_ref, wg_ref, wu_ref, o_ref, wg_bf_ref, wu_bf_ref):
    @pl.when(pl.program_id(1) == 0)
    def _():
        wg_bf_ref[...] = wg_ref[...].astype(jnp.bfloat16)
        wu_bf_ref[...] = wu_ref[...].astype(jnp.bfloat16)

    x = x_ref[...]
    g = _dot(x, wg_bf_ref[...])
    u = _dot(x, wu_bf_ref[...])
    o_ref[...] = (g * jax.nn.sigmoid(g) * u).astype(o_ref.dtype)


def gate_up(x, wg, wu, tm, tn):
    m, k = x.shape
    _, n = wg.shape
    wspec = pl.BlockSpec((k, tn), lambda j, i: (0, j))
    return pl.pallas_call(
        _gate_up_kernel,
        grid=(n // tn, m // tm),
        in_specs=[pl.BlockSpec((tm, k), lambda j, i: (i, 0)), wspec, wspec],
        out_specs=pl.BlockSpec((tm, tn), lambda j, i: (i, j)),
        out_shape=jax.ShapeDtypeStruct((m, n), jnp.bfloat16),
        scratch_shapes=[pltpu.VMEM((k, tn), jnp.bfloat16)] * 2,
        compiler_params=_cparams(2),
        name="gate_up",
    )(x, wg, wu)


def swiglu_ffn(xn, wg, wu, wd, tm_gu, tn_gu, tm_d, tn_d):
    a = gate_up(xn, wg, wu, tm_gu, tn_gu)
    return matmul(a, wd, jnp.float32, tm_d, tn_d)


def _head_sum(x, ones_bd):
    parts = []
    for s in range(x.shape[1] // LANES):
        parts.append(_dot(x[:, s * LANES:(s + 1) * LANES], ones_bd, precision=HI))
    return jnp.concatenate(parts, axis=1)


def _token_shift(p_ref, pp_ref, pn_ref, mup_ref, mun_ref, first, last):
    p = p_ref[...]
    rows = p.shape[0]
    rid = lax.broadcasted_iota(jnp.int32, p.shape, 0)
    prev_row = jnp.where(first, 0.0, pp_ref[SUBLANES - 1:SUBLANES, :])
    next_row = jnp.where(last, 0.0, pn_ref[0:1, :])
    p_prev = jnp.where(rid == 0, prev_row, pltpu.roll(p, 1, 0))
    p_next = jnp.where(rid == rows - 1, next_row, pltpu.roll(p, rows - 1, 0))
    return p + mup_ref[...] * (p_prev - p) + mun_ref[...] * (p_next - p)


def _rwkv_prep_kernel(
        p_ref, pp_ref, pn_ref, s_ref, sp_ref, sn_ref,
        mup_ref, mun_ref, mups_ref, muns_ref,
        w0f_ref, w0b_ref, a0f_ref, a0b_ref, kk_ref, ka_ref, rk_ref,
        w2f_ref, w2b_ref, a2f_ref, a2b_ref, g2_ref,
        atf_ref, btf_ref, ktf_ref, rtf_ref, glf_ref,
        atb_ref, btb_ref, ktb_ref, rtb_ref, glb_ref,
        v_ref, gate_ref, bonus_ref):
    i = pl.program_id(0)
    first = i == 0
    last = i == pl.num_programs(0) - 1
    tm = p_ref.shape[0]
    n_chunks = tm // CHUNK

    p = _token_shift(p_ref, pp_ref, pn_ref, mup_ref, mun_ref, first, last)
    s = _token_shift(s_ref, sp_ref, sn_ref, mups_ref, muns_ref, first, last)
    r = p[:, 0:D_RWKV]
    k = p[:, D_RWKV:2 * D_RWKV]
    v = p[:, 2 * D_RWKV:3 * D_RWKV]
    hw = jnp.tanh(s[:, 0:RANK_PAD]).astype(jnp.bfloat16)
    xa = s[:, RANK_PAD:2 * RANK_PAD].astype(jnp.bfloat16)
    sg = jax.nn.sigmoid(s[:, 2 * RANK_PAD:]).astype(jnp.bfloat16)

    li = lax.broadcasted_iota(jnp.int32, (LANES, LANES), 0) // RWKV_HEAD
    lj = lax.broadcasted_iota(jnp.int32, (LANES, LANES), 1) // RWKV_HEAD
    ones_bd = (li == lj).astype(jnp.float32)

    kk = k * kk_ref[...]
    kk = kk / jnp.maximum(jnp.sqrt(_head_sum(kk * kk, ones_bd)), 1e-12)

    ti = lax.broadcasted_iota(jnp.int32, (tm, tm), 0)
    tj = lax.broadcasted_iota(jnp.int32, (tm, tm), 1)
    same_chunk = (ti // CHUNK) == (tj // CHUNK)

    k_sum = jnp.zeros_like(k)
    dirs = ((w0f_ref, w2f_ref, a0f_ref, a2f_ref, atf_ref, btf_ref, ktf_ref, rtf_ref, glf_ref, False),
            (w0b_ref, w2b_ref, a0b_ref, a2b_ref, atb_ref, btb_ref, ktb_ref, rtb_ref, glb_ref, True))
    for w0_ref, w2_ref, a0_ref, a2_ref, at_ref, bt_ref, kt_ref, rt_ref, gl_ref, rev in dirs:
        z = w0_ref[...] + _dot(hw, w2_ref[...])
        lw = -math.exp(-0.5) * jax.nn.sigmoid(z)
        a = jax.nn.sigmoid(a0_ref[...] + _dot(xa, a2_ref[...]))
        k_dir = k * (1.0 + (a - 1.0) * ka_ref[...])
        k_sum = k_sum + k_dir
        order = (tj >= ti) if rev else (tj <= ti)
        tri = jnp.where(same_chunk & order, 1.0, 0.0)
        c = _dot(tri, lw, precision=HI)
        e_neg = jnp.exp(-c)
        at_ref[...] = (-kk * jnp.exp(c - lw)).astype(at_ref.dtype)
        bt_ref[...] = (kk * a * e_neg).astype(bt_ref.dtype)
        kt_ref[...] = (k_dir * e_neg).astype(kt_ref.dtype)
        e_pos = jnp.exp(c)
        rt_ref[...] = (r * e_pos).astype(rt_ref.dtype)
        for j in range(n_chunks):
            end = j * CHUNK if rev else (j + 1) * CHUNK - 1
            gl_ref[j] = e_pos[end:end + 1, :]

    v_ref[...] = v.astype(v_ref.dtype)
    gate_ref[...] = _dot(sg, g2_ref[...])
    bonus_ref[...] = _head_sum(r * k_sum * rk_ref[...], ones_bd) * v


def rwkv_prep(proj, params, tm=128):
    t = proj.shape[0]
    nb = t // tm
    hb = tm // SUBLANES
    n_halo = t // SUBLANES
    wide = 3 * D_RWKV
    small_blk = wide // SMALL_COLS

    def row(c, cb=0):
        return pl.BlockSpec((tm, c), lambda i: (i, cb))

    def halo_prev(c, cb=0):
        return pl.BlockSpec((SUBLANES, c), lambda i: (jnp.maximum(i * hb - 1, 0), cb))

    def halo_next(c, cb=0):
        return pl.BlockSpec((SUBLANES, c), lambda i: (jnp.minimum((i + 1) * hb, n_halo - 1), cb))

    def vec(c):
        return pl.BlockSpec((1, c), lambda i: (0, 0))

    def mat(r_, c):
        return pl.BlockSpec((r_, c), lambda i: (0, 0))

    gl_spec = pl.BlockSpec((tm // CHUNK, 1, D_RWKV), lambda i: (i, 0, 0))
    big = jax.ShapeDtypeStruct((t, D_RWKV), jnp.float32)
    gl = jax.ShapeDtypeStruct((t // CHUNK, 1, D_RWKV), jnp.float32)
    in_specs = [row(wide), halo_prev(wide), halo_next(wide),
                row(SMALL_COLS, small_blk), halo_prev(SMALL_COLS, small_blk), halo_next(SMALL_COLS, small_blk),
                vec(wide), vec(wide), vec(SMALL_COLS), vec(SMALL_COLS)]
    in_specs += [vec(D_RWKV)] * 7
    in_specs += [mat(RANK_PAD, D_RWKV)] * 4 + [mat(GATE_RANK, D_RWKV)]
    dir_specs = [row(D_RWKV)] * 4 + [gl_spec]
    big_bf = jax.ShapeDtypeStruct((t, D_RWKV), jnp.bfloat16)
    dir_shapes = [big_bf] * 4 + [gl]
    return pl.pallas_call(
        _rwkv_prep_kernel,
        grid=(nb,),
        in_specs=in_specs,
        out_specs=dir_specs + dir_specs + [row(D_RWKV)] * 3,
        out_shape=dir_shapes + dir_shapes + [big_bf, big, big],
        compiler_params=_cparams(1),
        name="rwkv_prep",
    )(proj, proj, proj, proj, proj, proj, *params)


def _stack(x, lane_head):
    zero = jnp.zeros_like(x)
    return jnp.concatenate([jnp.where(lane_head == 0, x, zero),
                            jnp.where(lane_head == 1, x, zero)], axis=0)


def _scan_masks(rev):
    n2 = 2 * CHUNK
    ri = lax.broadcasted_iota(jnp.int32, (n2, n2), 0)
    ci = lax.broadcasted_iota(jnp.int32, (n2, n2), 1)
    same = (ri // CHUNK) == (ci // CHUNK)
    before = (ci > ri) if rev else (ci < ri)
    strict = same & before
    incl = same & (before | (ri == ci))
    eye = jnp.where(ri == ci, 1.0, 0.0)
    return strict, jnp.concatenate([incl, incl], axis=1), eye


def _scan_prepare(units):
    bf = jnp.bfloat16
    n2 = 2 * CHUNK
    lane_head = lax.broadcasted_iota(jnp.int32, (CHUNK, PAIR), 1) // RWKV_HEAD
    stacked = [[_stack(x, lane_head) for x in u[:5]] for u in units]
    bks = [jnp.concatenate([s[1], s[2]], axis=0) for s in stacked]
    g_as = [_dot(s[0], bk, _NT) for s, bk in zip(stacked, bks)]
    g_rs = [_dot(s[3], bk, _NT) for s, bk in zip(stacked, bks)]
    xs = [jnp.where(u[6][0], g[:, :n2], 0.0) for u, g in zip(units, g_as)]
    a_aks = [jnp.where(u[6][0], g[:, n2:], 0.0).astype(bf) for u, g in zip(units, g_as)]
    a_rs = [jnp.where(u[6][1], g, 0.0).astype(bf) for u, g in zip(units, g_rs)]
    akv = [_dot(a, s[4]).astype(bf) for a, s in zip(a_aks, stacked)]

    tinvs = [u[6][2] + x for u, x in zip(units, xs)]
    for _ in range(int(math.log2(CHUNK)) - 1):
        xbs = [x.astype(bf) for x in xs]
        xs = [_dot(xb, xb) for xb in xbs]
        tinvs = [t + _dot(t.astype(bf), x.astype(bf)) for t, x in zip(tinvs, xs)]

    wqs = [_dot(t.astype(bf), jnp.concatenate([s[0], kv], axis=1))
           for t, s, kv in zip(tinvs, stacked, akv)]
    out = []
    for u, s, bk, wq, a_r in zip(units, stacked, bks, wqs, a_rs):
        gl = u[5]
        w = wq[:, :PAIR].astype(bf)
        q = wq[:, PAIR:]
        bkg = (bk * gl).astype(bf)
        m = _dot(w, bkg[:n2], _TN)
        n = _dot(jnp.concatenate([q.astype(bf), s[4]], axis=0), bkg, _TN)
        out.append((w, q, s[3], a_r, s[4], m.astype(bf), n, gl))
    return out


def _scan_output(prep, s0b):
    w, q, r2, a_r, v2 = prep[:5]
    u = (_dot(w, s0b, _NT) + q).astype(jnp.bfloat16)
    y2 = _dot(r2, s0b, _NT) + _dot(a_r, jnp.concatenate([u, v2], axis=0))
    return y2[:CHUNK] + y2[CHUNK:]


def _rwkv_scan_kernel(atf_ref, btf_ref, ktf_ref, rtf_ref, glf_ref, vf_ref,
                      atb_ref, btb_ref, ktb_ref, rtb_ref, glb_ref, vb_ref,
                      yf_ref, yb_ref, sf_ref, sb_ref, *, n_blk, n_par):
    @pl.when(pl.program_id(1) == 0)
    def _():
        sf_ref[...] = jnp.zeros_like(sf_ref)
        sb_ref[...] = jnp.zeros_like(sb_ref)

    def tile(j, p):
        return pl.ds(j * CHUNK, CHUNK), pl.ds(p * PAIR, PAIR)

    masks_f = _scan_masks(False)
    masks_b = _scan_masks(True)
    units = []
    for p in range(n_par):
        for j in range(n_blk):
            units.append(tuple(ref[tile(j, p)] for ref in (atf_ref, btf_ref, ktf_ref, rtf_ref, vf_ref))
                         + (glf_ref[j, :, pl.ds(p * PAIR, PAIR)], masks_f))
            units.append(tuple(ref[tile(j, p)] for ref in (atb_ref, btb_ref, ktb_ref, rtb_ref, vb_ref))
                         + (glb_ref[j, :, pl.ds(p * PAIR, PAIR)], masks_b))
    prep = _scan_prepare(units)

    sf = [sf_ref[p] for p in range(n_par)]
    sb = [sb_ref[p] for p in range(n_par)]
    for j in range(n_blk):
        jb = n_blk - 1 - j
        for p in range(n_par):
            pf = prep[2 * (p * n_blk + j)]
            pb = prep[2 * (p * n_blk + jb) + 1]
            sfb = sf[p].astype(jnp.bfloat16)
            sbb = sb[p].astype(jnp.bfloat16)
            sf[p] = sf[p] * pf[7] + _dot(sfb, pf[5]) + pf[6]
            sb[p] = sb[p] * pb[7] + _dot(sbb, pb[5]) + pb[6]
            yf_ref[tile(j, p)] = _scan_output(pf, sfb)
            yb_ref[tile(jb, p)] = _scan_output(pb, sbb)
    for p in range(n_par):
        sf_ref[p] = sf[p]
        sb_ref[p] = sb[p]


def rwkv_scan(fwd, bwd, v, n_blk=4, n_par=2):
    t = v.shape[0]
    nb = t // (CHUNK * n_blk)
    n_grp = D_RWKV // (PAIR * n_par)
    width = PAIR * n_par
    f_blk = pl.BlockSpec((CHUNK * n_blk, width), lambda p, c: (c, p))
    b_blk = pl.BlockSpec((CHUNK * n_blk, width), lambda p, c: (nb - 1 - c, p))
    f_gl = pl.BlockSpec((n_blk, 1, width), lambda p, c: (c, 0, p))
    b_gl = pl.BlockSpec((n_blk, 1, width), lambda p, c: (nb - 1 - c, 0, p))
    out = jax.ShapeDtypeStruct((t, D_RWKV), jnp.float32)
    return pl.pallas_call(
        functools.partial(_rwkv_scan_kernel, n_blk=n_blk, n_par=n_par),
        grid=(n_grp, nb),
        in_specs=[f_blk] * 4 + [f_gl, f_blk] + [b_blk] * 4 + [b_gl, b_blk],
        out_specs=[f_blk, b_blk],
        out_shape=[out, out],
        scratch_shapes=[pltpu.VMEM((n_par, PAIR, PAIR), jnp.float32)] * 2,
        compiler_params=_cparams(2),
        name="rwkv_scan",
    )(*fwd, v, *bwd, v)


def _rwkv_post_kernel(yf_ref, yb_ref, bonus_ref, gate_ref, gw_ref, gb_ref, o_ref):
    li = lax.broadcasted_iota(jnp.int32, (LANES, LANES), 0) // RWKV_HEAD
    lj = lax.broadcasted_iota(jnp.int32, (LANES, LANES), 1) // RWKV_HEAD
    mean_bd = jnp.where(li == lj, 1.0 / RWKV_HEAD, 0.0)
    y = yf_ref[...] + yb_ref[...]
    d = y - _head_sum(y, mean_bd)
    var = _head_sum(d * d, mean_bd)
    yn = d * lax.rsqrt(var + RWKV_GN_EPS) * gw_ref[...] + gb_ref[...]
    o_ref[...] = ((yn + bonus_ref[...]) * gate_ref[...]).astype(o_ref.dtype)


def rwkv_post(yf, yb, bonus, gate, gn_w, gn_b, tm=256):
    t = yf.shape[0]
    row = pl.BlockSpec((tm, D_RWKV), lambda i: (i, 0))
    vec = pl.BlockSpec((1, D_RWKV), lambda i: (0, 0))
    return pl.pallas_call(
        _rwkv_post_kernel,
        grid=(t // tm,),
        in_specs=[row, row, row, row, vec, vec],
        out_specs=row,
        out_shape=jax.ShapeDtypeStruct((t, D_RWKV), jnp.bfloat16),
        compiler_params=_cparams(1),
        name="rwkv_post",
    )(yf, yb, bonus, gate, gn_w.reshape(1, -1), gn_b.reshape(1, -1))


SLOPE_PIECES = 3
SC_PER_HEAD = 1 + SLOPE_PIECES
ONES_ROWS = 16
V_ROWS = DIFF_HEAD + ONES_ROWS


def _diff_attn_kernel(sc_ref, qt_ref, k_ref, vt_ref, g_ref, o_ref, s_ref, *, tq, tk, lam_init):
    bf = jnp.bfloat16
    h = pl.program_id(0)
    qi = pl.program_id(1)
    n_kv = k_ref.shape[0] // tk
    lam = sc_ref[0]
    base = 1 + h * SC_PER_HEAD
    slope = sc_ref[base]
    pieces = [sc_ref[base + 1 + i] for i in range(SLOPE_PIECES)]

    qt = qt_ref[...]
    qrow = lax.broadcasted_iota(jnp.int32, (DIFF_HEAD, tq), 0)
    feat = jnp.zeros((DIFF_HEAD, tq), jnp.float32)
    for i, pc in enumerate(pieces):
        feat = jnp.where(qrow % DIFF_D == i, 2.0 * pc, feat)
        feat = jnp.where(qrow % DIFF_D == SLOPE_PIECES + i, pc, feat)
    feat = feat.astype(bf)
    own_rows = [qrow < DIFF_D, qrow >= DIFF_D]
    qts = [jnp.where(own, qt, feat) for own in own_rows]

    kl = lax.broadcasted_iota(jnp.int32, (tk, LANES), 1)
    kr = lax.broadcasted_iota(jnp.int32, (tk, LANES), 0)
    kfeat = jnp.where(kl % DIFF_D < SLOPE_PIECES, kr >> 1,
                      jnp.where(kl % DIFF_D < 2 * SLOPE_PIECES, kr & 1, 0)).astype(jnp.float32)
    kfeat_before = kfeat.astype(bf)
    kfeat_after = (-kfeat).astype(bf)
    kzero = jnp.zeros((tk, LANES), bf)
    own_lanes = [kl < DIFF_D, kl >= DIFF_D]

    qpos = qi * tq + lax.broadcasted_iota(jnp.int32, (1, tq), 1)
    q_bias = slope * qpos.astype(jnp.float32)

    def online(carry, s, t_q, vt):
        m, acc = carry
        m_new = jnp.maximum(m, jnp.max(s, axis=0, keepdims=True) + t_q)
        alpha = jnp.exp2(m - m_new)
        p = jnp.exp2(s - (m_new - t_q))
        return m_new, alpha * acc + _dot(vt, p.astype(bf))

    jd = (qi * tq) // tk

    def tile_of(n):
        return n + (n >= jd).astype(jnp.int32)

    def scores(n, slot):
        j = tile_of(n)
        start = pl.multiple_of(j * tk, tk)
        kb = k_ref[pl.ds(start, tk), :]
        kf = jnp.where(j < jd, kfeat_before, kfeat_after)
        for mp, (own, q_m) in enumerate(zip(own_lanes, qts)):
            s_ref[slot, mp] = _dot(jnp.where(own, kb, kf), q_m)

    def consume(n, slot, carries):
        j = tile_of(n)
        start = pl.multiple_of(j * tk, tk)
        vt = vt_ref[:, pl.ds(start, tk)]
        sign = jnp.where(j < jd, 1.0, -1.0)
        t_q = sign * (slope * (j * tk).astype(jnp.float32) - q_bias)
        return tuple(online(c, s_ref[slot, mp], t_q, vt) for mp, c in enumerate(carries))

    start_d = pl.multiple_of(jd * tk, tk)
    kb_d = k_ref[pl.ds(start_d, tk), :]
    vt_d = vt_ref[:, pl.ds(start_d, tk)]
    kpos = jd * tk + lax.broadcasted_iota(jnp.int32, (tk, tq), 0)
    bias = slope * jnp.abs(kpos - qpos).astype(jnp.float32)
    init = (jnp.full((1, tq), -jnp.inf, jnp.float32), jnp.zeros((V_ROWS, tq), jnp.float32))
    for mp, (own, q_m) in enumerate(zip(own_lanes, qts)):
        s_ref[1, mp] = _dot(jnp.where(own, kb_d, kzero), q_m) - bias
    scores(jnp.int32(0), 0)
    carries = tuple(online(init, s_ref[1, mp], 0.0, vt_d) for mp in range(2))

    def body(i, carries):
        n = 2 * i
        scores(n + 1, 1)
        carries = consume(n, 0, carries)
        scores(n + 2, 0)
        return consume(n + 1, 1, carries)

    n_off = n_kv - 1
    carries = lax.fori_loop(0, n_off // 2, body, carries)
    (_, acc0), (_, acc1) = consume(jnp.int32(n_off - 1), 0, carries)
    o0 = acc0[:DIFF_HEAD] / acc0[DIFF_HEAD:DIFF_HEAD + 1]
    o1 = acc1[:DIFF_HEAD] / acc1[DIFF_HEAD:DIFF_HEAD + 1]
    o = (o0 - lam * o1).T
    o_ref[...] = (_rms(o, g_ref[...], SUBLN_EPS) * (1.0 - lam_init)).astype(o_ref.dtype)


def _bf16_pieces(x, n):
    out = []
    for _ in range(n):
        p = (x.view(np.uint32) & np.uint32(0xFFFF0000)).view(np.float32)
        out.append(p)
        x = (x - p).astype(np.float32)
    return out


def diff_attention(qt, k, vt, lam, subln_g, lam_init, tq=512, tk=512):
    t = k.shape[0]
    assert tk % tq == 0 and tk // 2 <= 256
    assert t % (2 * tk) == 0
    slopes = 2.0 ** (-ALIBI_MAX_EXP * np.arange(1, N_DIFF_HEADS + 1, dtype=np.float64) / N_DIFF_HEADS)
    slopes = (slopes * math.log2(math.e)).astype(np.float32)
    pieces = _bf16_pieces(slopes, SLOPE_PIECES)
    per_head = np.stack([sum(pieces)] + pieces, axis=1)
    lam = jnp.concatenate([lam, jnp.asarray(per_head.reshape(-1))])
    return pl.pallas_call(
        functools.partial(_diff_attn_kernel, tq=tq, tk=tk, lam_init=lam_init),
        grid=(N_DIFF_HEADS, t // tq),
        in_specs=[pl.BlockSpec(memory_space=pltpu.SMEM),
                  pl.BlockSpec((DIFF_HEAD, tq), lambda h, i: (h, i)),
                  pl.BlockSpec((t, DIFF_HEAD), lambda h, i: (0, h)),
                  pl.BlockSpec((V_ROWS, t), lambda h, i: (h, 0)),
                  pl.BlockSpec((1, DIFF_HEAD), lambda h, i: (0, 0))],
        out_specs=pl.BlockSpec((tq, DIFF_HEAD), lambda h, i: (i, h)),
        out_shape=jax.ShapeDtypeStruct((t, D_DIFF), jnp.bfloat16),
        scratch_shapes=[pltpu.VMEM((2, 2, tk, tq), jnp.float32)],
        compiler_params=_cparams(2),
        name="diff_attn",
    )(lam, qt, k, vt, subln_g.reshape(1, DIFF_HEAD))


def _pad_cols(w, to):
    return jnp.pad(w, ((0, 0), (0, to - w.shape[1])))


def _layer(h, l, lam_init, x_norm, prm):
    bf = jnp.bfloat16
    f32 = jnp.float32
    f1 = swiglu_ffn(x_norm, prm["ffn1_w_gate"][l], prm["ffn1_w_up"][l],
                    prm["ffn1_w_down"][l].astype(bf), 1024, 256, 512, 512)
    h, xn = resid_norm(h, f1, prm["ffn1_post_g"][l], prm["mix_pre_g"][l], FFN_RESIDUAL, bf)

    w_in = prm["w_in"][l]
    c_rkv = 3 * D_RWKV
    c_w = c_rkv + DECAY_RANK
    c_a = c_w + ICLR_RANK
    c_g = c_a + GATE_RANK
    w_small = jnp.concatenate([_pad_cols(w_in[:, c_rkv:c_w], RANK_PAD),
                               _pad_cols(w_in[:, c_w:c_a], RANK_PAD),
                               w_in[:, c_a:c_g]], axis=1)
    w_rwkv = jnp.concatenate([w_in[:, :c_rkv], w_small], axis=1).astype(bf)
    proj = matmul(xn, w_rwkv, f32, 1024, 512)
    q_scale = DIFF_D ** -0.5 * math.log2(math.e)
    col_scale = jnp.concatenate([jnp.full((1, D_DIFF), q_scale, f32), jnp.ones((1, 2 * D_DIFF), f32)], axis=1)
    qkv = matmul_col_scaled(xn, w_in[:, c_g:].astype(bf), col_scale, bf, 1024, 512)

    def small_vec(a):
        return jnp.concatenate([_pad_cols(a[None, c_rkv:c_w], RANK_PAD),
                                _pad_cols(a[None, c_w:c_a], RANK_PAD),
                                a[None, c_a:c_g]], axis=1)

    def pad_rows(w):
        return jnp.pad(w, ((0, RANK_PAD - w.shape[0]), (0, 0))).astype(bf)

    mu_p, mu_n = prm["mu_prev"][l], prm["mu_next"][l]
    vecs = [prm[n][l].reshape(1, D_RWKV) for n in ("w0_f", "w0_b", "a0_f", "a0_b", "k_k", "k_a", "r_k")]
    mats = [pad_rows(prm["w2_f"][l]), pad_rows(prm["w2_b"][l]),
            pad_rows(prm["a2_f"][l]), pad_rows(prm["a2_b"][l]), prm["g2"][l].astype(bf)]
    prep = rwkv_prep(proj, [mu_p[None, :c_rkv], mu_n[None, :c_rkv], small_vec(mu_p), small_vec(mu_n)]
                     + vecs + mats)
    fwd, bwd, (v_r, gate, bonus) = prep[0:5], prep[5:10], prep[10:13]
    yf, yb = rwkv_scan(fwd, bwd, v_r)
    y_a = rwkv_post(yf, yb, bonus, gate, prm["gn_w"][l], prm["gn_b"][l])

    lam = (jnp.exp(jnp.sum(prm["lq1"][l] * prm["lk1"][l]))
           - jnp.exp(jnp.sum(prm["lq2"][l] * prm["lk2"][l])) + lam_init).reshape(1)
    t = qkv.shape[0]
    vt = qkv[:, 2 * D_DIFF:].T.reshape(N_DIFF_HEADS, DIFF_HEAD, t)
    vt = jnp.concatenate([vt, jnp.ones((N_DIFF_HEADS, ONES_ROWS, t), bf)], axis=1).reshape(N_DIFF_HEADS * V_ROWS, t)
    y_b = diff_attention(qkv[:, :D_DIFF].T, qkv[:, D_DIFF:2 * D_DIFF], vt, lam, prm["subln_g"][l], lam_init)

    w_out = prm["w_out"][l]
    mix = matmul_two(y_a, y_b, w_out[:D_RWKV].astype(bf), w_out[D_RWKV:].astype(bf), f32, 1024, 512)
    h, xn = resid_norm(h, mix, prm["mix_post_g"][l], prm["ffn2_pre_g"][l], 1.0, bf)

    f2 = swiglu_ffn(xn, prm["ffn2_w_gate"][l], prm["ffn2_w_up"][l],
                    prm["ffn2_w_down"][l].astype(bf), 1024, 256, 512, 512)
    return resid_norm(h, f2, prm["ffn2_post_g"][l], prm["final_g"][l], FFN_RESIDUAL, f32)


def kernel(x, ffn1_pre_g, ffn1_w_gate, ffn1_w_up, ffn1_w_down, ffn1_post_g, mix_pre_g, w_in, mu_prev, mu_next, w0_f, w2_f, w0_b, w2_b, a0_f, a2_f, a0_b, a2_b, g2, k_k, k_a, r_k, gn_w, gn_b, lq1, lk1, lq2, lk2, subln_g, w_out, mix_post_g, ffn2_pre_g, ffn2_w_gate, ffn2_w_up, ffn2_w_down, ffn2_post_g, final_g):
    prm = dict(locals())
    bsz, t, d = x.shape
    assert bsz == 1
    depth = ffn1_pre_g.shape[0]
    h = x.reshape(t, d)
    for l in range(depth):
        lam_init = 0.8 - 0.6 * math.exp(-0.3 * l)
        x_norm = rms_norm_cast(h, ffn1_pre_g[l], jnp.bfloat16)
        _, h = _layer(h, l, lam_init, x_norm, prm)
    return h.reshape(bsz, t, d)
```

```python
import functools
import math

import jax
import jax.numpy as jnp
import numpy as np
from jax import lax
from jax.experimental import pallas as pl
from jax.experimental.pallas import tpu as pltpu

D_MODEL = 4096
D_RWKV = 2048
D_DIFF = 2048
RWKV_HEAD = 64
DECAY_RANK = 96
ICLR_RANK = 96
GATE_RANK = 256
DIFF_D = 64
DIFF_HEAD = 128
N_DIFF_HEADS = 16
ALIBI_MAX_EXP = 8.0
NORM_EPS = 1e-6
RWKV_GN_EPS = 64e-5
SUBLN_EPS = 1e-5
FFN_RESIDUAL = 0.5

LANES = 128
SUBLANES = 8
VMEM_LIMIT = 56 * 1024 * 1024

RANK_PAD = 128
SMALL_COLS = 2 * RANK_PAD + GATE_RANK
CHUNK = 64
PAIR = 2 * RWKV_HEAD
HI = lax.Precision.HIGHEST


def _cparams(n_axes):
    return pltpu.CompilerParams(
        dimension_semantics=("arbitrary",) * n_axes, vmem_limit_bytes=VMEM_LIMIT)


def _dot(a, b, dims=(((1,), (0,)), ((), ())), precision=None):
    return lax.dot_general(a, b, dims, precision=precision,
                           preferred_element_type=jnp.float32)


_NT = (((1,), (1,)), ((), ()))
_TN = (((0,), (0,)), ((), ()))


def _rms(x, g, eps):
    return x * lax.rsqrt(jnp.mean(x * x, axis=-1, keepdims=True) + eps) * g


def _norm_kernel(x_ref, g_ref, o_ref):
    o_ref[...] = _rms(x_ref[...], g_ref[...], NORM_EPS).astype(o_ref.dtype)


def rms_norm_cast(x, g, out_dtype, tm=256):
    m, d = x.shape
    return pl.pallas_call(
        _norm_kernel,
        grid=(m // tm,),
        in_specs=[pl.BlockSpec((tm, d), lambda i: (i, 0)),
                  pl.BlockSpec((1, d), lambda i: (0, 0))],
        out_specs=pl.BlockSpec((tm, d), lambda i: (i, 0)),
        out_shape=jax.ShapeDtypeStruct((m, d), out_dtype),
        compiler_params=_cparams(1),
        name="rms_norm",
    )(x, g.reshape(1, d))


def _resid_norm_kernel(h_ref, f_ref, gp_ref, gn_ref, h_out_ref, n_out_ref, *, scale):
    h = h_ref[...] + scale * _rms(f_ref[...], gp_ref[...], NORM_EPS)
    h_out_ref[...] = h
    n_out_ref[...] = _rms(h, gn_ref[...], NORM_EPS).astype(n_out_ref.dtype)


def resid_norm(h, f, g_post, g_next, scale, out_dtype, tm=256):
    m, d = h.shape
    row = pl.BlockSpec((tm, d), lambda i: (i, 0))
    vec = pl.BlockSpec((1, d), lambda i: (0, 0))
    return pl.pallas_call(
        functools.partial(_resid_norm_kernel, scale=scale),
        grid=(m // tm,),
        in_specs=[row, row, vec, vec],
        out_specs=[row, row],
        out_shape=[jax.ShapeDtypeStruct((m, d), jnp.float32),
                   jax.ShapeDtypeStruct((m, d), out_dtype)],
        compiler_params=_cparams(1),
        name="resid_norm",
    )(h, f, g_post.reshape(1, d), g_next.reshape(1, d))


def _mm_kernel(x_ref, w_ref, o_ref):
    o_ref[...] = _dot(x_ref[...], w_ref[...]).astype(o_ref.dtype)


def matmul(x, w, out_dtype, tm, tn):
    m, k = x.shape
    _, n = w.shape
    return pl.pallas_call(
        _mm_kernel,
        grid=(n // tn, m // tm),
        in_specs=[pl.BlockSpec((tm, k), lambda j, i: (i, 0)),
                  pl.BlockSpec((k, tn), lambda j, i: (0, j))],
        out_specs=pl.BlockSpec((tm, tn), lambda j, i: (i, j)),
        out_shape=jax.ShapeDtypeStruct((m, n), out_dtype),
        compiler_params=_cparams(2),
        name="matmul",
    )(x, w)


def _mm_two_kernel(xa_ref, xb_ref, wa_ref, wb_ref, o_ref):
    o_ref[...] = (_dot(xa_ref[...], wa_ref[...]) + _dot(xb_ref[...], wb_ref[...])).astype(o_ref.dtype)


def matmul_two(xa, xb, wa, wb, out_dtype, tm, tn):
    m, ka = xa.shape
    _, kb = xb.shape
    _, n = wa.shape
    return pl.pallas_call(
        _mm_two_kernel,
        grid=(n // tn, m // tm),
        in_specs=[pl.BlockSpec((tm, ka), lambda j, i: (i, 0)),
                  pl.BlockSpec((tm, kb), lambda j, i: (i, 0)),
                  pl.BlockSpec((ka, tn), lambda j, i: (0, j)),
                  pl.BlockSpec((kb, tn), lambda j, i: (0, j))],
        out_specs=pl.BlockSpec((tm, tn), lambda j, i: (i, j)),
        out_shape=jax.ShapeDtypeStruct((m, n), out_dtype),
        compiler_params=_cparams(2),
        name="matmul_two",
    )(xa, xb, wa, wb)


def _mm_scaled_kernel(x_ref, w_ref, cs_ref, o_ref):
    o_ref[...] = (_dot(x_ref[...], w_ref[...]) * cs_ref[...]).astype(o_ref.dtype)


def matmul_col_scaled(x, w, col_scale, out_dtype, tm, tn):
    m, k = x.shape
    _, n = w.shape
    return pl.pallas_call(
        _mm_scaled_kernel,
        grid=(n // tn, m // tm),
        in_specs=[pl.BlockSpec((tm, k), lambda j, i: (i, 0)),
                  pl.BlockSpec((k, tn), lambda j, i: (0, j)),
                  pl.BlockSpec((1, tn), lambda j, i: (0, j))],
        out_specs=pl.BlockSpec((tm, tn), lambda j, i: (i, j)),
        out_shape=jax.ShapeDtypeStruct((m, n), out_dtype),
        compiler_params=_cparams(2),
        name="matmul_col_scaled",
    )(x, w, col_scale)


def _gate_up_kernel(x_ref, wg_ref, wu_ref, o_ref, wg_bf_ref, wu_bf_ref):
    @pl.when(pl.program_id(1) == 0)
    def _():
        wg_bf_ref[...] = wg_ref[...].astype(jnp.bfloat16)
        wu_bf_ref[...] = wu_ref[...].astype(jnp.bfloat16)

    x = x_ref[...]
    g = _dot(x, wg_bf_ref[...])
    u = _dot(x, wu_bf_ref[...])
    o_ref[...] = (g * jax.nn.sigmoid(g) * u).astype(o_ref.dtype)


def gate_up(x, wg, wu, tm, tn):
    m, k = x.shape
    _, n = wg.shape
    wspec = pl.BlockSpec((k, tn), lambda j, i: (0, j))
    return pl.pallas_call(
        _gate_up_kernel,
        grid=(n // tn, m // tm),
        in_specs=[pl.BlockSpec((tm, k), lambda j, i: (i, 0)), wspec, wspec],
        out_specs=pl.BlockSpec((tm, tn), lambda j, i: (i, j)),
        out_shape=jax.ShapeDtypeStruct((m, n), jnp.bfloat16),
        scratch_shapes=[pltpu.VMEM((k, tn), jnp.bfloat16)] * 2,
        compiler_params=_cparams(2),
        name="gate_up",
    )(x, wg, wu)


def swiglu_ffn(xn, wg, wu, wd, tm_gu, tn_gu, tm_d, tn_d):
    a = gate_up(xn, wg, wu, tm_gu, tn_gu)
    return matmul(a, wd, jnp.float32, tm_d, tn_d)


def _head_sum(x, ones_bd):
    parts = []
    for s in range(x.shape[1] // LANES):
        parts.append(_dot(x[:, s * LANES:(s + 1) * LANES], ones_bd, precision=HI))
    return jnp.concatenate(parts, axis=1)


def _token_shift(p_ref, pp_ref, pn_ref, mup_ref, mun_ref, first, last):
    p = p_ref[...]
    rows = p.shape[0]
    rid = lax.broadcasted_iota(jnp.int32, p.shape, 0)
    prev_row = jnp.where(first, 0.0, pp_ref[SUBLANES - 1:SUBLANES, :])
    next_row = jnp.where(last, 0.0, pn_ref[0:1, :])
    p_prev = jnp.where(rid == 0, prev_row, pltpu.roll(p, 1, 0))
    p_next = jnp.where(rid == rows - 1, next_row, pltpu.roll(p, rows - 1, 0))
    return p + mup_ref[...] * (p_prev - p) + mun_ref[...] * (p_next - p)


def _rwkv_prep_kernel(
        p_ref, pp_ref, pn_ref, s_ref, sp_ref, sn_ref,
        mup_ref, mun_ref, mups_ref, muns_ref,
        w0f_ref, w0b_ref, a0f_ref, a0b_ref, kk_ref, ka_ref, rk_ref,
        w2f_ref, w2b_ref, a2f_ref, a2b_ref, g2_ref,
        atf_ref, btf_ref, ktf_ref, rtf_ref, glf_ref,
        atb_ref, btb_ref, ktb_ref, rtb_ref, glb_ref,
        v_ref, gate_ref, bonus_ref):
    i = pl.program_id(0)
    first = i == 0
    last = i == pl.num_programs(0) - 1
    tm = p_ref.shape[0]
    n_chunks = tm // CHUNK

    p = _token_shift(p_ref, pp_ref, pn_ref, mup_ref, mun_ref, first, last)
    s = _token_shift(s_ref, sp_ref, sn_ref, mups_ref, muns_ref, first, last)
    r = p[:, 0:D_RWKV]
    k = p[:, D_RWKV:2 * D_RWKV]
    v = p[:, 2 * D_RWKV:3 * D_RWKV]
    hw = jnp.tanh(s[:, 0:RANK_PAD]).astype(jnp.bfloat16)
    xa = s[:, RANK_PAD:2 * RANK_PAD].astype(jnp.bfloat16)
    sg = jax.nn.sigmoid(s[:, 2 * RANK_PAD:]).astype(jnp.bfloat16)

    li = lax.broadcasted_iota(jnp.int32, (LANES, LANES), 0) // RWKV_HEAD
    lj = lax.broadcasted_iota(jnp.int32, (LANES, LANES), 1) // RWKV_HEAD
    ones_bd = (li == lj).astype(jnp.float32)

    kk = k * kk_ref[...]
    kk = kk / jnp.maximum(jnp.sqrt(_head_sum(kk * kk, ones_bd)), 1e-12)

    ti = lax.broadcasted_iota(jnp.int32, (tm, tm), 0)
    tj = lax.broadcasted_iota(jnp.int32, (tm, tm), 1)
    same_chunk = (ti // CHUNK) == (tj // CHUNK)

    k_sum = jnp.zeros_like(k)
    dirs = ((w0f_ref, w2f_ref, a0f_ref, a2f_ref, atf_ref, btf_ref, ktf_ref, rtf_ref, glf_ref, False),
            (w0b_ref, w2b_ref, a0b_ref, a2b_ref, atb_ref, btb_ref, ktb_ref, rtb_ref, glb_ref, True))
    for w0_ref, w2_ref, a0_ref, a2_ref, at_ref, bt_ref, kt_ref, rt_ref, gl_ref, rev in dirs:
        z = w0_ref[...] + _dot(hw, w2_ref[...])
        lw = -math.exp(-0.5) * jax.nn.sigmoid(z)
        a = jax.nn.sigmoid(a0_ref[...] + _dot(xa, a2_ref[...]))
        k_dir = k * (1.0 + (a - 1.0) * ka_ref[...])
        k_sum = k_sum + k_dir
        order = (tj >= ti) if rev else (tj <= ti)
        tri = jnp.where(same_chunk & order, 1.0, 0.0)
        c = _dot(tri, lw, precision=HI)
        e_neg = jnp.exp(-c)
        at_ref[...] = (-kk * jnp.exp(c - lw)).astype(at_ref.dtype)
        bt_ref[...] = (kk * a * e_neg).astype(bt_ref.dtype)
        kt_ref[...] = (k_dir * e_neg).astype(kt_ref.dtype)
        e_pos = jnp.exp(c)
        rt_ref[...] = (r * e_pos).astype(rt_ref.dtype)
        for j in range(n_chunks):
            end = j * CHUNK if rev else (j + 1) * CHUNK - 1
            gl_ref[j] = e_pos[end:end + 1, :]

    v_ref[...] = v.astype(v_ref.dtype)
    gate_ref[...] = _dot(sg, g2_ref[...])
    bonus_ref[...] = _head_sum(r * k_sum * rk_ref[...], ones_bd) * v


def rwkv_prep(proj, params, tm=128):
    t = proj.shape[0]
    nb = t // tm
    hb = tm // SUBLANES
    n_halo = t // SUBLANES
    wide = 3 * D_RWKV
    small_blk = wide // SMALL_COLS

    def row(c, cb=0):
        return pl.BlockSpec((tm, c), lambda i: (i, cb))

    def halo_prev(c, cb=0):
        return pl.BlockSpec((SUBLANES, c), lambda i: (jnp.maximum(i * hb - 1, 0), cb))

    def halo_next(c, cb=0):
        return pl.BlockSpec((SUBLANES, c), lambda i: (jnp.minimum((i + 1) * hb, n_halo - 1), cb))

    def vec(c):
        return pl.BlockSpec((1, c), lambda i: (0, 0))

    def mat(r_, c):
        return pl.BlockSpec((r_, c), lambda i: (0, 0))

    gl_spec = pl.BlockSpec((tm // CHUNK, 1, D_RWKV), lambda i: (i, 0, 0))
    big = jax.ShapeDtypeStruct((t, D_RWKV), jnp.float32)
    gl = jax.ShapeDtypeStruct((t // CHUNK, 1, D_RWKV), jnp.float32)
    in_specs = [row(wide), halo_prev(wide), halo_next(wide),
                row(SMALL_COLS, small_blk), halo_prev(SMALL_COLS, small_blk), halo_next(SMALL_COLS, small_blk),
                vec(wide), vec(wide), vec(SMALL_COLS), vec(SMALL_COLS)]
    in_specs += [vec(D_RWKV)] * 7
    in_specs += [mat(RANK_PAD, D_RWKV)] * 4 + [mat(GATE_RANK, D_RWKV)]
    dir_specs = [row(D_RWKV)] * 4 + [gl_spec]
    big_bf = jax.ShapeDtypeStruct((t, D_RWKV), jnp.bfloat16)
    dir_shapes = [big_bf] * 4 + [gl]
    return pl.pallas_call(
        _rwkv_prep_kernel,
        grid=(nb,),
        in_specs=in_specs,
        out_specs=dir_specs + dir_specs + [row(D_RWKV)] * 3,
        out_shape=dir_shapes + dir_shapes + [big_bf, big, big],
        compiler_params=_cparams(1),
        name="rwkv_prep",
    )(proj, proj, proj, proj, proj, proj, *params)


def _stack(x, lane_head):
    zero = jnp.zeros_like(x)
    return jnp.concatenate([jnp.where(lane_head == 0, x, zero),
                            jnp.where(lane_head == 1, x, zero)], axis=0)


def _scan_masks(rev):
    n2 = 2 * CHUNK
    ri = lax.broadcasted_iota(jnp.int32, (n2, n2), 0)
    ci = lax.broadcasted_iota(jnp.int32, (n2, n2), 1)
    same = (ri // CHUNK) == (ci // CHUNK)
    before = (ci > ri) if rev else (ci < ri)
    strict = same & before
    incl = same & (before | (ri == ci))
    eye = jnp.where(ri == ci, 1.0, 0.0)
    return strict, jnp.concatenate([incl, incl], axis=1), eye


def _scan_prepare(units):
    bf = jnp.bfloat16
    n2 = 2 * CHUNK
    lane_head = lax.broadcasted_iota(jnp.int32, (CHUNK, PAIR), 1) // RWKV_HEAD
    stacked = [[_stack(x, lane_head) for x in u[:5]] for u in units]
    bks = [jnp.concatenate([s[1], s[2]], axis=0) for s in stacked]
    g_as = [_dot(s[0], bk, _NT) for s, bk in zip(stacked, bks)]
    g_rs = [_dot(s[3], bk, _NT) for s, bk in zip(stacked, bks)]
    xs = [jnp.where(u[6][0], g[:, :n2], 0.0) for u, g in zip(units, g_as)]
    a_aks = [jnp.where(u[6][0], g[:, n2:], 0.0).astype(bf) for u, g in zip(units, g_as)]
    a_rs = [jnp.where(u[6][1], g, 0.0).astype(bf) for u, g in zip(units, g_rs)]
    akv = [_dot(a, s[4]).astype(bf) for a, s in zip(a_aks, stacked)]

    tinvs = [u[6][2] + x for u, x in zip(units, xs)]
    for _ in range(int(math.log2(CHUNK)) - 1):
        xbs = [x.astype(bf) for x in xs]
        xs = [_dot(xb, xb) for xb in xbs]
        tinvs = [t + _dot(t.astype(bf), x.astype(bf)) for t, x in zip(tinvs, xs)]

    wqs = [_dot(t.astype(bf), jnp.concatenate([s[0], kv], axis=1))
           for t, s, kv in zip(tinvs, stacked, akv)]
    out = []
    for u, s, bk, wq, a_r in zip(units, stacked, bks, wqs, a_rs):
        gl = u[5]
        w = wq[:, :PAIR].astype(bf)
        q = wq[:, PAIR:]
        bkg = (bk * gl).astype(bf)
        m = _dot(w, bkg[:n2], _TN)
        n = _dot(jnp.concatenate([q.astype(bf), s[4]], axis=0), bkg, _TN)
        out.append((w, q, s[3], a_r, s[4], m.astype(bf), n, gl))
    return out


def _scan_output(prep, s0b):
    w, q, r2, a_r, v2 = prep[:5]
    u = (_dot(w, s0b, _NT) + q).astype(jnp.bfloat16)
    y2 = _dot(r2, s0b, _NT) + _dot(a_r, jnp.concatenate([u, v2], axis=0))
    return y2[:CHUNK] + y2[CHUNK:]


def _rwkv_scan_kernel(atf_ref, btf_ref, ktf_ref, rtf_ref, glf_ref, vf_ref,
                      atb_ref, btb_ref, ktb_ref, rtb_ref, glb_ref, vb_ref,
                      yf_ref, yb_ref, sf_ref, sb_ref, *, n_blk, n_par):
    @pl.when(pl.program_id(1) == 0)
    def _():
        sf_ref[...] = jnp.zeros_like(sf_ref)
        sb_ref[...] = jnp.zeros_like(sb_ref)

    def tile(j, p):
        return pl.ds(j * CHUNK, CHUNK), pl.ds(p * PAIR, PAIR)

    masks_f = _scan_masks(False)
    masks_b = _scan_masks(True)
    units = []
    for p in range(n_par):
        for j in range(n_blk):
            units.append(tuple(ref[tile(j, p)] for ref in (atf_ref, btf_ref, ktf_ref, rtf_ref, vf_ref))
                         + (glf_ref[j, :, pl.ds(p * PAIR, PAIR)], masks_f))
            units.append(tuple(ref[tile(j, p)] for ref in (atb_ref, btb_ref, ktb_ref, rtb_ref, vb_ref))
                         + (glb_ref[j, :, pl.ds(p * PAIR, PAIR)], masks_b))
    prep = _scan_prepare(units)

    sf = [sf_ref[p] for p in range(n_par)]
    sb = [sb_ref[p] for p in range(n_par)]
    for j in range(n_blk):
        jb = n_blk - 1 - j
        for p in range(n_par):
            pf = prep[2 * (p * n_blk + j)]
            pb = prep[2 * (p * n_blk + jb) + 1]
            sfb = sf[p].astype(jnp.bfloat16)
            sbb = sb[p].astype(jnp.bfloat16)
            sf[p] = sf[p] * pf[7] + _dot(sfb, pf[5]) + pf[6]
            sb[p] = sb[p] * pb[7] + _dot(sbb, pb[5]) + pb[6]
            yf_ref[tile(j, p)] = _scan_output(pf, sfb)
            yb_ref[tile(jb, p)] = _scan_output(pb, sbb)
    for p in range(n_par):
        sf_ref[p] = sf[p]
        sb_ref[p] = sb[p]


def rwkv_scan(fwd, bwd, v, n_blk=4, n_par=2):
    t = v.shape[0]
    nb = t // (CHUNK * n_blk)
    n_grp = D_RWKV // (PAIR * n_par)
    width = PAIR * n_par
    f_blk = pl.BlockSpec((CHUNK * n_blk, width), lambda p, c: (c, p))
    b_blk = pl.BlockSpec((CHUNK * n_blk, width), lambda p, c: (nb - 1 - c, p))
    f_gl = pl.BlockSpec((n_blk, 1, width), lambda p, c: (c, 0, p))
    b_gl = pl.BlockSpec((n_blk, 1, width), lambda p, c: (nb - 1 - c, 0, p))
    out = jax.ShapeDtypeStruct((t, D_RWKV), jnp.float32)
    return pl.pallas_call(
        functools.partial(_rwkv_scan_kernel, n_blk=n_blk, n_par=n_par),
        grid=(n_grp, nb),
        in_specs=[f_blk] * 4 + [f_gl, f_blk] + [b_blk] * 4 + [b_gl, b_blk],
        out_specs=[f_blk, b_blk],
        out_shape=[out, out],
        scratch_shapes=[pltpu.VMEM((n_par, PAIR, PAIR), jnp.float32)] * 2,
        compiler_params=_cparams(2),
        name="rwkv_scan",
    )(*fwd, v, *bwd, v)


def _rwkv_post_kernel(yf_ref, yb_ref, bonus_ref, gate_ref, gw_ref, gb_ref, o_ref):
    li = lax.broadcasted_iota(jnp.int32, (LANES, LANES), 0) // RWKV_HEAD
    lj = lax.broadcasted_iota(jnp.int32, (LANES, LANES), 1) // RWKV_HEAD
    mean_bd = jnp.where(li == lj, 1.0 / RWKV_HEAD, 0.0)
    y = yf_ref[...] + yb_ref[...]
    d = y - _head_sum(y, mean_bd)
    var = _head_sum(d * d, mean_bd)
    yn = d * lax.rsqrt(var + RWKV_GN_EPS) * gw_ref[...] + gb_ref[...]
    o_ref[...] = ((yn + bonus_ref[...]) * gate_ref[...]).astype(o_ref.dtype)


def rwkv_post(yf, yb, bonus, gate, gn_w, gn_b, tm=256):
    t = yf.shape[0]
    row = pl.BlockSpec((tm, D_RWKV), lambda i: (i, 0))
    vec = pl.BlockSpec((1, D_RWKV), lambda i: (0, 0))
    return pl.pallas_call(
        _rwkv_post_kernel,
        grid=(t // tm,),
        in_specs=[row, row, row, row, vec, vec],
        out_specs=row,
        out_shape=jax.ShapeDtypeStruct((t, D_RWKV), jnp.bfloat16),
        compiler_params=_cparams(1),
        name="rwkv_post",
    )(yf, yb, bonus, gate, gn_w.reshape(1, -1), gn_b.reshape(1, -1))


SLOPE_PIECES = 3
SC_PER_HEAD = 2 + SLOPE_PIECES
ONES_ROWS = 16
V_ROWS = DIFF_HEAD + ONES_ROWS
UNDERFLOW_LOG2 = 160.0
BOUND_SLACK = 1.01


def _diff_attn_kernel(sc_ref, qt_ref, k_ref, vt_ref, g_ref, o_ref, s_ref, kmax_ref, *, tq, tk, lam_init):
    bf = jnp.bfloat16
    h = pl.program_id(0)
    qi = pl.program_id(1)
    n_kv = k_ref.shape[0] // tk
    lam = sc_ref[0]
    base = 1 + h * SC_PER_HEAD
    slope = sc_ref[base]
    pieces = [sc_ref[base + 1 + i] for i in range(SLOPE_PIECES)]
    inv_slope = sc_ref[base + 1 + SLOPE_PIECES]

    @pl.when(qi == 0)
    def _():
        kmax_ref[0] = jnp.max(jnp.abs(k_ref[...].astype(jnp.float32)))

    qt = qt_ref[...]
    q_abs = jnp.abs(qt.astype(jnp.float32))
    q_l1 = jnp.maximum(jnp.sum(q_abs[:DIFF_D], axis=0, keepdims=True),
                       jnp.sum(q_abs[DIFF_D:], axis=0, keepdims=True))
    bound = jnp.max(q_l1) * kmax_ref[0]
    reach = (2.0 * BOUND_SLACK * bound + UNDERFLOW_LOG2) * inv_slope
    width = jnp.int32(1)
    for d in range(1, n_kv):
        width = width + jnp.where(reach < float(d * tk), 0, 1).astype(jnp.int32)
    qrow = lax.broadcasted_iota(jnp.int32, (DIFF_HEAD, tq), 0)
    feat = jnp.zeros((DIFF_HEAD, tq), jnp.float32)
    for i, pc in enumerate(pieces):
        feat = jnp.where(qrow % DIFF_D == i, 2.0 * pc, feat)
        feat = jnp.where(qrow % DIFF_D == SLOPE_PIECES + i, pc, feat)
    feat = feat.astype(bf)
    own_rows = [qrow < DIFF_D, qrow >= DIFF_D]
    qts = [jnp.where(own, qt, feat) for own in own_rows]

    kl = lax.broadcasted_iota(jnp.int32, (tk, LANES), 1)
    kr = lax.broadcasted_iota(jnp.int32, (tk, LANES), 0)
    kfeat = jnp.where(kl % DIFF_D < SLOPE_PIECES, kr >> 1,
                      jnp.where(kl % DIFF_D < 2 * SLOPE_PIECES, kr & 1, 0)).astype(jnp.float32)
    kfeat_before = kfeat.astype(bf)
    kfeat_after = (-kfeat).astype(bf)
    kzero = jnp.zeros((tk, LANES), bf)
    own_lanes = [kl < DIFF_D, kl >= DIFF_D]

    qpos = qi * tq + lax.broadcasted_iota(jnp.int32, (1, tq), 1)
    q_bias = slope * qpos.astype(jnp.float32)

    def online(carry, s, t_q, vt):
        m, acc = carry
        m_new = jnp.maximum(m, jnp.max(s, axis=0, keepdims=True) + t_q)
        alpha = jnp.exp2(m - m_new)
        p = jnp.exp2(s - (m_new - t_q))
        return m_new, alpha * acc + _dot(vt, p.astype(bf))

    jd = (qi * tq) // tk
    lo = jnp.maximum(jd - width, 0)
    hi = jnp.minimum(jd + width, n_kv - 1)
    n_off = hi - lo

    def tile_of(n):
        j = lo + n
        return j + (j >= jd).astype(jnp.int32)

    def scores(n, slot, mp):
        j = tile_of(n)
        start = pl.multiple_of(j * tk, tk)
        kb = k_ref[pl.ds(start, tk), :]
        kf = jnp.where(j < jd, kfeat_before, kfeat_after)
        s_ref[slot, mp] = _dot(jnp.where(own_lanes[mp], kb, kf), qts[mp])

    def consume(n, slot, mp, carry):
        j = tile_of(n)
        start = pl.multiple_of(j * tk, tk)
        vt = vt_ref[:, pl.ds(start, tk)]
        sign = jnp.where(j < jd, 1.0, -1.0)
        t_q = sign * (slope * (j * tk).astype(jnp.float32) - q_bias)
        return online(carry, s_ref[slot, mp], t_q, vt)

    def step(n_next, n_cur, slot_next, slot_cur, carries):
        out = []
        for mp in range(2):
            scores(n_next, slot_next, mp)
            out.append(consume(n_cur, slot_cur, mp, carries[mp]))
        return tuple(out)

    start_d = pl.multiple_of(jd * tk, tk)
    kb_d = k_ref[pl.ds(start_d, tk), :]
    vt_d = vt_ref[:, pl.ds(start_d, tk)]
    kpos = jd * tk + lax.broadcasted_iota(jnp.int32, (tk, tq), 0)
    bias = slope * jnp.abs(kpos - qpos).astype(jnp.float32)
    init = (jnp.full((1, tq), -jnp.inf, jnp.float32), jnp.zeros((V_ROWS, tq), jnp.float32))
    for mp, (own, q_m) in enumerate(zip(own_lanes, qts)):
        s_ref[1, mp] = _dot(jnp.where(own, kb_d, kzero), q_m) - bias
    carries = []
    for mp in range(2):
        scores(jnp.int32(0), 0, mp)
        carries.append(online(init, s_ref[1, mp], 0.0, vt_d))

    def body(i, carries):
        n = 2 * i
        carries = step(n + 1, n, 1, 0, carries)
        return step(n + 2, n + 1, 0, 1, carries)

    n_pairs = (n_off - 1) // 2
    carries = lax.fori_loop(0, n_pairs, body, tuple(carries))
    last = n_off - 1

    def tail_two(carries):
        carries = step(last, last - 1, 1, 0, carries)
        return tuple(consume(last, 1, mp, carries[mp]) for mp in range(2))

    def tail_one(carries):
        return tuple(consume(last, 0, mp, carries[mp]) for mp in range(2))

    (_, acc0), (_, acc1) = lax.cond(last == 2 * n_pairs + 1, tail_two, tail_one, carries)
    o0 = acc0[:DIFF_HEAD] / acc0[DIFF_HEAD:DIFF_HEAD + 1]
    o1 = acc1[:DIFF_HEAD] / acc1[DIFF_HEAD:DIFF_HEAD + 1]
    o = (o0 - lam * o1).T
    o_ref[...] = (_rms(o, g_ref[...], SUBLN_EPS) * (1.0 - lam_init)).astype(o_ref.dtype)


def _bf16_pieces(x, n):
    out = []
    for _ in range(n):
        p = (x.view(np.uint32) & np.uint32(0xFFFF0000)).view(np.float32)
        out.append(p)
        x = (x - p).astype(np.float32)
    return out


def diff_attention(qt, k, vt, lam, subln_g, lam_init, tq=512, tk=512):
    t = k.shape[0]
    assert tk % tq == 0 and tk // 2 <= 256
    assert t // tk >= 2
    slopes = 2.0 ** (-ALIBI_MAX_EXP * np.arange(1, N_DIFF_HEADS + 1, dtype=np.float64) / N_DIFF_HEADS)
    slopes = (slopes * math.log2(math.e)).astype(np.float32)
    pieces = _bf16_pieces(slopes, SLOPE_PIECES)
    slope_used = sum(pieces)
    per_head = np.stack([slope_used] + pieces + [(1.0 / slope_used).astype(np.float32)], axis=1)
    lam = jnp.concatenate([lam, jnp.asarray(per_head.reshape(-1))])
    return pl.pallas_call(
        functools.partial(_diff_attn_kernel, tq=tq, tk=tk, lam_init=lam_init),
        grid=(N_DIFF_HEADS, t // tq),
        in_specs=[pl.BlockSpec(memory_space=pltpu.SMEM),
                  pl.BlockSpec((DIFF_HEAD, tq), lambda h, i: (h, i)),
                  pl.BlockSpec((t, DIFF_HEAD), lambda h, i: (0, h)),
                  pl.BlockSpec((V_ROWS, t), lambda h, i: (h, 0)),
                  pl.BlockSpec((1, DIFF_HEAD), lambda h, i: (0, 0))],
        out_specs=pl.BlockSpec((tq, DIFF_HEAD), lambda h, i: (i, h)),
        out_shape=jax.ShapeDtypeStruct((t, D_DIFF), jnp.bfloat16),
        scratch_shapes=[pltpu.VMEM((2, 2, tk, tq), jnp.float32),
                        pltpu.SMEM((1,), jnp.float32)],
        compiler_params=_cparams(2),
        name="diff_attn",
    )(lam, qt, k, vt, subln_g.reshape(1, DIFF_HEAD))


def _pad_cols(w, to):
    return jnp.pad(w, ((0, 0), (0, to - w.shape[1])))


def _layer(h, l, lam_init, x_norm, prm):
    bf = jnp.bfloat16
    f32 = jnp.float32
    f1 = swiglu_ffn(x_norm, prm["ffn1_w_gate"][l], prm["ffn1_w_up"][l],
                    prm["ffn1_w_down"][l].astype(bf), 1024, 256, 512, 512)
    h, xn = resid_norm(h, f1, prm["ffn1_post_g"][l], prm["mix_pre_g"][l], FFN_RESIDUAL, bf)

    w_in = prm["w_in"][l]
    c_rkv = 3 * D_RWKV
    c_w = c_rkv + DECAY_RANK
    c_a = c_w + ICLR_RANK
    c_g = c_a + GATE_RANK
    w_small = jnp.concatenate([_pad_cols(w_in[:, c_rkv:c_w], RANK_PAD),
                               _pad_cols(w_in[:, c_w:c_a], RANK_PAD),
                               w_in[:, c_a:c_g]], axis=1)
    w_rwkv = jnp.concatenate([w_in[:, :c_rkv], w_small], axis=1).astype(bf)
    proj = matmul(xn, w_rwkv, f32, 1024, 512)
    q_scale = DIFF_D ** -0.5 * math.log2(math.e)
    col_scale = jnp.concatenate([jnp.full((1, D_DIFF), q_scale, f32), jnp.ones((1, 2 * D_DIFF), f32)], axis=1)
    qkv = matmul_col_scaled(xn, w_in[:, c_g:].astype(bf), col_scale, bf, 1024, 512)

    def small_vec(a):
        return jnp.concatenate([_pad_cols(a[None, c_rkv:c_w], RANK_PAD),
                                _pad_cols(a[None, c_w:c_a], RANK_PAD),
                                a[None, c_a:c_g]], axis=1)

    def pad_rows(w):
        return jnp.pad(w, ((0, RANK_PAD - w.shape[0]), (0, 0))).astype(bf)

    mu_p, mu_n = prm["mu_prev"][l], prm["mu_next"][l]
    vecs = [prm[n][l].reshape(1, D_RWKV) for n in ("w0_f", "w0_b", "a0_f", "a0_b", "k_k", "k_a", "r_k")]
    mats = [pad_rows(prm["w2_f"][l]), pad_rows(prm["w2_b"][l]),
            pad_rows(prm["a2_f"][l]), pad_rows(prm["a2_b"][l]), prm["g2"][l].astype(bf)]
    prep = rwkv_prep(proj, [mu_p[None, :c_rkv], mu_n[None, :c_rkv], small_vec(mu_p), small_vec(mu_n)]
                     + vecs + mats)
    fwd, bwd, (v_r, gate, bonus) = prep[0:5], prep[5:10], prep[10:13]
    yf, yb = rwkv_scan(fwd, bwd, v_r)
    y_a = rwkv_post(yf, yb, bonus, gate, prm["gn_w"][l], prm["gn_b"][l])

    lam = (jnp.exp(jnp.sum(prm["lq1"][l] * prm["lk1"][l]))
           - jnp.exp(jnp.sum(prm["lq2"][l] * prm["lk2"][l])) + lam_init).reshape(1)
    t = qkv.shape[0]
    vt = qkv[:, 2 * D_DIFF:].T.reshape(N_DIFF_HEADS, DIFF_HEAD, t)
    vt = jnp.concatenate([vt, jnp.ones((N_DIFF_HEADS, ONES_ROWS, t), bf)], axis=1).reshape(N_DIFF_HEADS * V_ROWS, t)
    y_b = diff_attention(qkv[:, :D_DIFF].T, qkv[:, D_DIFF:2 * D_DIFF], vt, lam, prm["subln_g"][l], lam_init)

    w_out = prm["w_out"][l]
    mix = matmul_two(y_a, y_b, w_out[:D_RWKV].astype(bf), w_out[D_RWKV:].astype(bf), f32, 1024, 512)
    h, xn = resid_norm(h, mix, prm["mix_post_g"][l], prm["ffn2_pre_g"][l], 1.0, bf)

    f2 = swiglu_ffn(xn, prm["ffn2_w_gate"][l], prm["ffn2_w_up"][l],
                    prm["ffn2_w_down"][l].astype(bf), 1024, 256, 512, 512)
    return resid_norm(h, f2, prm["ffn2_post_g"][l], prm["final_g"][l], FFN_RESIDUAL, f32)


def kernel(x, ffn1_pre_g, ffn1_w_gate, ffn1_w_up, ffn1_w_down, ffn1_post_g, mix_pre_g, w_in, mu_prev, mu_next, w0_f, w2_f, w0_b, w2_b, a0_f, a2_f, a0_b, a2_b, g2, k_k, k_a, r_k, gn_w, gn_b, lq1, lk1, lq2, lk2, subln_g, w_out, mix_post_g, ffn2_pre_g, ffn2_w_gate, ffn2_w_up, ffn2_w_down, ffn2_post_g, final_g):
    prm = dict(locals())
    bsz, t, d = x.shape
    assert bsz == 1
    depth = ffn1_pre_g.shape[0]
    h = x.reshape(t, d)
    for l in range(depth):
        lam_init = 0.8 - 0.6 * math.exp(-0.3 * l)
        x_norm = rms_norm_cast(h, ffn1_pre_g[l], jnp.bfloat16)
        _, h = _layer(h, l, lam_init, x_norm, prm)
    return h.reshape(bsz, t, d)
```

```python
import functools
import math

import jax
import jax.numpy as jnp
import numpy as np
from jax import lax
from jax.experimental import pallas as pl
from jax.experimental.pallas import tpu as pltpu

D_MODEL = 4096
D_RWKV = 2048
D_DIFF = 2048
RWKV_HEAD = 64
DECAY_RANK = 96
ICLR_RANK = 96
GATE_RANK = 256
DIFF_D = 64
DIFF_HEAD = 128
N_DIFF_HEADS = 16
ALIBI_MAX_EXP = 8.0
NORM_EPS = 1e-6
RWKV_GN_EPS = 64e-5
SUBLN_EPS = 1e-5
FFN_RESIDUAL = 0.5

LANES = 128
SUBLANES = 8
VMEM_LIMIT = 56 * 1024 * 1024

RANK_PAD = 128
SMALL_COLS = 2 * RANK_PAD + GATE_RANK
CHUNK = 64
PAIR = 2 * RWKV_HEAD


def _cparams(n_axes):
    return pltpu.CompilerParams(
        dimension_semantics=("arbitrary",) * n_axes, vmem_limit_bytes=VMEM_LIMIT)


def _dot(a, b, dims=(((1,), (0,)), ((), ()))):
    return lax.dot_general(a, b, dims, preferred_element_type=jnp.float32)


_NT = (((1,), (1,)), ((), ()))
_TN = (((0,), (0,)), ((), ()))


def _rms(x, g, eps):
    return x * lax.rsqrt(jnp.mean(x * x, axis=-1, keepdims=True) + eps) * g


def _norm_kernel(x_ref, g_ref, o_ref):
    o_ref[...] = _rms(x_ref[...], g_ref[...], NORM_EPS).astype(o_ref.dtype)


def rms_norm_cast(x, g, out_dtype, tm=256):
    m, d = x.shape
    return pl.pallas_call(
        _norm_kernel,
        grid=(m // tm,),
        in_specs=[pl.BlockSpec((tm, d), lambda i: (i, 0)),
                  pl.BlockSpec((1, d), lambda i: (0, 0))],
        out_specs=pl.BlockSpec((tm, d), lambda i: (i, 0)),
        out_shape=jax.ShapeDtypeStruct((m, d), out_dtype),
        compiler_params=_cparams(1),
        name="rms_norm",
    )(x, g.reshape(1, d))


def _resid_norm_kernel(h_ref, f_ref, gp_ref, gn_ref, h_out_ref, n_out_ref, *, scale):
    h = h_ref[...] + scale * _rms(f_ref[...], gp_ref[...], NORM_EPS)
    h_out_ref[...] = h
    n_out_ref[...] = _rms(h, gn_ref[...], NORM_EPS).astype(n_out_ref.dtype)


def resid_norm(h, f, g_post, g_next, scale, out_dtype, tm=256):
    m, d = h.shape
    row = pl.BlockSpec((tm, d), lambda i: (i, 0))
    vec = pl.BlockSpec((1, d), lambda i: (0, 0))
    return pl.pallas_call(
        functools.partial(_resid_norm_kernel, scale=scale),
        grid=(m // tm,),
        in_specs=[row, row, vec, vec],
        out_specs=[row, row],
        out_shape=[jax.ShapeDtypeStruct((m, d), jnp.float32),
                   jax.ShapeDtypeStruct((m, d), out_dtype)],
        compiler_params=_cparams(1),
        name="resid_norm",
    )(h, f, g_post.reshape(1, d), g_next.reshape(1, d))


def _mm_kernel(x_ref, w_ref, o_ref):
    o_ref[...] = _dot(x_ref[...], w_ref[...]).astype(o_ref.dtype)


def matmul(x, w, out_dtype, tm, tn):
    m, k = x.shape
    _, n = w.shape
    return pl.pallas_call(
        _mm_kernel,
        grid=(n // tn, m // tm),
        in_specs=[pl.BlockSpec((tm, k), lambda j, i: (i, 0)),
                  pl.BlockSpec((k, tn), lambda j, i: (0, j))],
        out_specs=pl.BlockSpec((tm, tn), lambda j, i: (i, j)),
        out_shape=jax.ShapeDtypeStruct((m, n), out_dtype),
        compiler_params=_cparams(2),
        name="matmul",
    )(x, w)


def _mm_two_kernel(xa_ref, xb_ref, wa_ref, wb_ref, o_ref):
    o_ref[...] = (_dot(xa_ref[...], wa_ref[...]) + _dot(xb_ref[...], wb_ref[...])).astype(o_ref.dtype)


def matmul_two(xa, xb, wa, wb, out_dtype, tm, tn):
    m, ka = xa.shape
    _, kb = xb.shape
    _, n = wa.shape
    return pl.pallas_call(
        _mm_two_kernel,
        grid=(n // tn, m // tm),
        in_specs=[pl.BlockSpec((tm, ka), lambda j, i: (i, 0)),
                  pl.BlockSpec((tm, kb), lambda j, i: (i, 0)),
                  pl.BlockSpec((ka, tn), lambda j, i: (0, j)),
                  pl.BlockSpec((kb, tn), lambda j, i: (0, j))],
        out_specs=pl.BlockSpec((tm, tn), lambda j, i: (i, j)),
        out_shape=jax.ShapeDtypeStruct((m, n), out_dtype),
        compiler_params=_cparams(2),
        name="matmul_two",
    )(xa, xb, wa, wb)


def _mm_scaled_kernel(x_ref, w_ref, cs_ref, o_ref):
    o_ref[...] = (_dot(x_ref[...], w_ref[...]) * cs_ref[...]).astype(o_ref.dtype)


def matmul_col_scaled(x, w, col_scale, out_dtype, tm, tn):
    m, k = x.shape
    _, n = w.shape
    return pl.pallas_call(
        _mm_scaled_kernel,
        grid=(n // tn, m // tm),
        in_specs=[pl.BlockSpec((tm, k), lambda j, i: (i, 0)),
                  pl.BlockSpec((k, tn), lambda j, i: (0, j)),
                  pl.BlockSpec((1, tn), lambda j, i: (0, j))],
        out_specs=pl.BlockSpec((tm, tn), lambda j, i: (i, j)),
        out_shape=jax.ShapeDtypeStruct((m, n), out_dtype),
        compiler_params=_cparams(2),
        name="matmul_col_scaled",
    )(x, w, col_scale)


def _gate_up_kernel(x_ref, wg_ref, wu_ref, o_ref, wg_bf_ref, wu_bf_ref):
    @pl.when(pl.program_id(1) == 0)
    def _():
        wg_bf_ref[...] = wg_ref[...].astype(jnp.bfloat16)
        wu_bf_ref[...] = wu_ref[...].astype(jnp.bfloat16)

    x = x_ref[...]
    g = _dot(x, wg_bf_ref[...])
    u = _dot(x, wu_bf_ref[...])
    o_ref[...] = (g * jax.nn.sigmoid(g) * u).astype(o_ref.dtype)


def gate_up(x, wg, wu, tm, tn):
    m, k = x.shape
    _, n = wg.shape
    wspec = pl.BlockSpec((k, tn), lambda j, i: (0, j))
    return pl.pallas_call(
        _gate_up_kernel,
        grid=(n // tn, m // tm),
        in_specs=[pl.BlockSpec((tm, k), lambda j, i: (i, 0)), wspec, wspec],
        out_specs=pl.BlockSpec((tm, tn), lambda j, i: (i, j)),
        out_shape=jax.ShapeDtypeStruct((m, n), jnp.bfloat16),
        scratch_shapes=[pltpu.VMEM((k, tn), jnp.bfloat16)] * 2,
        compiler_params=_cparams(2),
        name="gate_up",
    )(x, wg, wu)


def swiglu_ffn(xn, wg, wu, wd, tm_gu, tn_gu, tm_d, tn_d):
    a = gate_up(xn, wg, wu, tm_gu, tn_gu)
    return matmul(a, wd, jnp.float32, tm_d, tn_d)


def _bf16_split(x, n):
    pieces = []
    for _ in range(n - 1):
        p = x.astype(jnp.bfloat16)
        pieces.append(p)
        x = x - p.astype(jnp.float32)
    pieces.append(x.astype(jnp.bfloat16))
    return pieces


HEAD_SUM_PIECES = 2
CUMSUM_PIECES = 3


def _head_sum(x, w_bd):
    parts = []
    for s in range(x.shape[1] // LANES):
        pieces = _bf16_split(x[:, s * LANES:(s + 1) * LANES], HEAD_SUM_PIECES)
        parts.append(sum(_dot(p, w_bd) for p in pieces))
    return jnp.concatenate(parts, axis=1)


def _token_shift(p_ref, pp_ref, pn_ref, mup_ref, mun_ref, first, last):
    p = p_ref[...]
    rows = p.shape[0]
    rid = lax.broadcasted_iota(jnp.int32, p.shape, 0)
    prev_row = jnp.where(first, 0.0, pp_ref[SUBLANES - 1:SUBLANES, :])
    next_row = jnp.where(last, 0.0, pn_ref[0:1, :])
    p_prev = jnp.where(rid == 0, prev_row, pltpu.roll(p, 1, 0))
    p_next = jnp.where(rid == rows - 1, next_row, pltpu.roll(p, rows - 1, 0))
    return p + mup_ref[...] * (p_prev - p) + mun_ref[...] * (p_next - p)


def _rwkv_prep_kernel(
        p_ref, pp_ref, pn_ref, s_ref, sp_ref, sn_ref,
        mup_ref, mun_ref, mups_ref, muns_ref,
        w0f_ref, w0b_ref, a0f_ref, a0b_ref, kk_ref, ka_ref, rk_ref,
        w2f_ref, w2b_ref, a2f_ref, a2b_ref, g2_ref,
        atf_ref, btf_ref, ktf_ref, rtf_ref, glf_ref,
        atb_ref, btb_ref, ktb_ref, rtb_ref, glb_ref,
        v_ref, gate_ref, bonus_ref):
    i = pl.program_id(0)
    first = i == 0
    last = i == pl.num_programs(0) - 1
    tm = p_ref.shape[0]
    n_chunks = tm // CHUNK

    p = _token_shift(p_ref, pp_ref, pn_ref, mup_ref, mun_ref, first, last)
    s = _token_shift(s_ref, sp_ref, sn_ref, mups_ref, muns_ref, first, last)
    r = p[:, 0:D_RWKV]
    k = p[:, D_RWKV:2 * D_RWKV]
    v = p[:, 2 * D_RWKV:3 * D_RWKV]
    hw = jnp.tanh(s[:, 0:RANK_PAD]).astype(jnp.bfloat16)
    xa = s[:, RANK_PAD:2 * RANK_PAD].astype(jnp.bfloat16)
    sg = jax.nn.sigmoid(s[:, 2 * RANK_PAD:]).astype(jnp.bfloat16)

    li = lax.broadcasted_iota(jnp.int32, (LANES, LANES), 0) // RWKV_HEAD
    lj = lax.broadcasted_iota(jnp.int32, (LANES, LANES), 1) // RWKV_HEAD
    ones_bd = (li == lj).astype(jnp.bfloat16)

    kk = k * kk_ref[...]
    kk = kk / jnp.maximum(jnp.sqrt(_head_sum(kk * kk, ones_bd)), 1e-12)

    ti = lax.broadcasted_iota(jnp.int32, (tm, tm), 0)
    tj = lax.broadcasted_iota(jnp.int32, (tm, tm), 1)
    same_chunk = (ti // CHUNK) == (tj // CHUNK)

    k_sum = jnp.zeros_like(k)
    dirs = ((w0f_ref, w2f_ref, a0f_ref, a2f_ref, atf_ref, btf_ref, ktf_ref, rtf_ref, glf_ref, False),
            (w0b_ref, w2b_ref, a0b_ref, a2b_ref, atb_ref, btb_ref, ktb_ref, rtb_ref, glb_ref, True))
    for w0_ref, w2_ref, a0_ref, a2_ref, at_ref, bt_ref, kt_ref, rt_ref, gl_ref, rev in dirs:
        z = w0_ref[...] + _dot(hw, w2_ref[...])
        lw = -math.exp(-0.5) * jax.nn.sigmoid(z)
        a = jax.nn.sigmoid(a0_ref[...] + _dot(xa, a2_ref[...]))
        k_dir = k * (1.0 + (a - 1.0) * ka_ref[...])
        k_sum = k_sum + k_dir
        order = (tj >= ti) if rev else (tj <= ti)
        tri = jnp.where(same_chunk & order, 1.0, 0.0).astype(jnp.bfloat16)
        c = sum(_dot(tri, piece) for piece in _bf16_split(lw, CUMSUM_PIECES))
        e_neg = jnp.exp(-c)
        at_ref[...] = (-kk * jnp.exp(c - lw)).astype(at_ref.dtype)
        bt_ref[...] = (kk * a * e_neg).astype(bt_ref.dtype)
        kt_ref[...] = (k_dir * e_neg).astype(kt_ref.dtype)
        e_pos = jnp.exp(c)
        rt_ref[...] = (r * e_pos).astype(rt_ref.dtype)
        for j in range(n_chunks):
            end = j * CHUNK if rev else (j + 1) * CHUNK - 1
            gl_ref[j] = e_pos[end:end + 1, :]

    v_ref[...] = v.astype(v_ref.dtype)
    gate_ref[...] = _dot(sg, g2_ref[...])
    bonus_ref[...] = _head_sum(r * k_sum * rk_ref[...], ones_bd) * v


def rwkv_prep(proj, params, tm=128):
    t = proj.shape[0]
    nb = t // tm
    hb = tm // SUBLANES
    n_halo = t // SUBLANES
    wide = 3 * D_RWKV
    small_blk = wide // SMALL_COLS

    def row(c, cb=0):
        return pl.BlockSpec((tm, c), lambda i: (i, cb))

    def halo_prev(c, cb=0):
        return pl.BlockSpec((SUBLANES, c), lambda i: (jnp.maximum(i * hb - 1, 0), cb))

    def halo_next(c, cb=0):
        return pl.BlockSpec((SUBLANES, c), lambda i: (jnp.minimum((i + 1) * hb, n_halo - 1), cb))

    def vec(c):
        return pl.BlockSpec((1, c), lambda i: (0, 0))

    def mat(r_, c):
        return pl.BlockSpec((r_, c), lambda i: (0, 0))

    gl_spec = pl.BlockSpec((tm // CHUNK, 1, D_RWKV), lambda i: (i, 0, 0))
    big = jax.ShapeDtypeStruct((t, D_RWKV), jnp.float32)
    gl = jax.ShapeDtypeStruct((t // CHUNK, 1, D_RWKV), jnp.float32)
    in_specs = [row(wide), halo_prev(wide), halo_next(wide),
                row(SMALL_COLS, small_blk), halo_prev(SMALL_COLS, small_blk), halo_next(SMALL_COLS, small_blk),
                vec(wide), vec(wide), vec(SMALL_COLS), vec(SMALL_COLS)]
    in_specs += [vec(D_RWKV)] * 7
    in_specs += [mat(RANK_PAD, D_RWKV)] * 4 + [mat(GATE_RANK, D_RWKV)]
    dir_specs = [row(D_RWKV)] * 4 + [gl_spec]
    big_bf = jax.ShapeDtypeStruct((t, D_RWKV), jnp.bfloat16)
    dir_shapes = [big_bf] * 4 + [gl]
    return pl.pallas_call(
        _rwkv_prep_kernel,
        grid=(nb,),
        in_specs=in_specs,
        out_specs=dir_specs + dir_specs + [row(D_RWKV)] * 3,
        out_shape=dir_shapes + dir_shapes + [big_bf, big, big],
        compiler_params=_cparams(1),
        name="rwkv_prep",
    )(proj, proj, proj, proj, proj, proj, *params)


def _stack(x, lane_head):
    zero = jnp.zeros_like(x)
    return jnp.concatenate([jnp.where(lane_head == 0, x, zero),
                            jnp.where(lane_head == 1, x, zero)], axis=0)


def _scan_masks(rev):
    n2 = 2 * CHUNK
    ri = lax.broadcasted_iota(jnp.int32, (n2, n2), 0)
    ci = lax.broadcasted_iota(jnp.int32, (n2, n2), 1)
    same = (ri // CHUNK) == (ci // CHUNK)
    before = (ci > ri) if rev else (ci < ri)
    strict = same & before
    incl = same & (before | (ri == ci))
    eye = jnp.where(ri == ci, 1.0, 0.0)
    return strict, jnp.concatenate([incl, incl], axis=1), eye


def _scan_prepare(units):
    bf = jnp.bfloat16
    n2 = 2 * CHUNK
    lane_head = lax.broadcasted_iota(jnp.int32, (CHUNK, PAIR), 1) // RWKV_HEAD
    stacked = [[_stack(x, lane_head) for x in u[:5]] for u in units]
    bks = [jnp.concatenate([s[1], s[2]], axis=0) for s in stacked]
    g_as = [_dot(s[0], bk, _NT) for s, bk in zip(stacked, bks)]
    g_rs = [_dot(s[3], bk, _NT) for s, bk in zip(stacked, bks)]
    xs = [jnp.where(u[6][0], g[:, :n2], 0.0) for u, g in zip(units, g_as)]
    a_aks = [jnp.where(u[6][0], g[:, n2:], 0.0).astype(bf) for u, g in zip(units, g_as)]
    a_rs = [jnp.where(u[6][1], g, 0.0).astype(bf) for u, g in zip(units, g_rs)]
    akv = [_dot(a, s[4]).astype(bf) for a, s in zip(a_aks, stacked)]

    tinvs = [u[6][2] + x for u, x in zip(units, xs)]
    for _ in range(int(math.log2(CHUNK)) - 1):
        xbs = [x.astype(bf) for x in xs]
        xs = [_dot(xb, xb) for xb in xbs]
        tinvs = [t + _dot(t.astype(bf), x.astype(bf)) for t, x in zip(tinvs, xs)]

    wqs = [_dot(t.astype(bf), jnp.concatenate([s[0], kv], axis=1))
           for t, s, kv in zip(tinvs, stacked, akv)]
    out = []
    for u, s, bk, wq, a_r in zip(units, stacked, bks, wqs, a_rs):
        gl = u[5]
        w = wq[:, :PAIR].astype(bf)
        q = wq[:, PAIR:]
        bkg = (bk * gl).astype(bf)
        m = _dot(w, bkg[:n2], _TN)
        n = _dot(jnp.concatenate([q.astype(bf), s[4]], axis=0), bkg, _TN)
        out.append((w, q, s[3], a_r, s[4], m.astype(bf), n, gl))
    return out


def _scan_output(prep, s0b):
    w, q, r2, a_r, v2 = prep[:5]
    u = (_dot(w, s0b, _NT) + q).astype(jnp.bfloat16)
    y2 = _dot(r2, s0b, _NT) + _dot(a_r, jnp.concatenate([u, v2], axis=0))
    return y2[:CHUNK] + y2[CHUNK:]


def _rwkv_scan_kernel(atf_ref, btf_ref, ktf_ref, rtf_ref, glf_ref, vf_ref,
                      atb_ref, btb_ref, ktb_ref, rtb_ref, glb_ref, vb_ref,
                      yf_ref, yb_ref, sf_ref, sb_ref, *, n_blk, n_par):
    @pl.when(pl.program_id(1) == 0)
    def _():
        sf_ref[...] = jnp.zeros_like(sf_ref)
        sb_ref[...] = jnp.zeros_like(sb_ref)

    def tile(j, p):
        return pl.ds(j * CHUNK, CHUNK), pl.ds(p * PAIR, PAIR)

    masks_f = _scan_masks(False)
    masks_b = _scan_masks(True)
    units = []
    for p in range(n_par):
        for j in range(n_blk):
            units.append(tuple(ref[tile(j, p)] for ref in (atf_ref, btf_ref, ktf_ref, rtf_ref, vf_ref))
                         + (glf_ref[j, :, pl.ds(p * PAIR, PAIR)], masks_f))
            units.append(tuple(ref[tile(j, p)] for ref in (atb_ref, btb_ref, ktb_ref, rtb_ref, vb_ref))
                         + (glb_ref[j, :, pl.ds(p * PAIR, PAIR)], masks_b))
    prep = _scan_prepare(units)

    sf = [sf_ref[p] for p in range(n_par)]
    sb = [sb_ref[p] for p in range(n_par)]
    for j in range(n_blk):
        jb = n_blk - 1 - j
        for p in range(n_par):
            pf = prep[2 * (p * n_blk + j)]
            pb = prep[2 * (p * n_blk + jb) + 1]
            sfb = sf[p].astype(jnp.bfloat16)
            sbb = sb[p].astype(jnp.bfloat16)
            sf[p] = sf[p] * pf[7] + _dot(sfb, pf[5]) + pf[6]
            sb[p] = sb[p] * pb[7] + _dot(sbb, pb[5]) + pb[6]
            yf_ref[tile(j, p)] = _scan_output(pf, sfb)
            yb_ref[tile(jb, p)] = _scan_output(pb, sbb)
    for p in range(n_par):
        sf_ref[p] = sf[p]
        sb_ref[p] = sb[p]


def rwkv_scan(fwd, bwd, v, n_blk=4, n_par=2):
    t = v.shape[0]
    nb = t // (CHUNK * n_blk)
    n_grp = D_RWKV // (PAIR * n_par)
    width = PAIR * n_par
    f_blk = pl.BlockSpec((CHUNK * n_blk, width), lambda p, c: (c, p))
    b_blk = pl.BlockSpec((CHUNK * n_blk, width), lambda p, c: (nb - 1 - c, p))
    f_gl = pl.BlockSpec((n_blk, 1, width), lambda p, c: (c, 0, p))
    b_gl = pl.BlockSpec((n_blk, 1, width), lambda p, c: (nb - 1 - c, 0, p))
    out = jax.ShapeDtypeStruct((t, D_RWKV), jnp.float32)
    return pl.pallas_call(
        functools.partial(_rwkv_scan_kernel, n_blk=n_blk, n_par=n_par),
        grid=(n_grp, nb),
        in_specs=[f_blk] * 4 + [f_gl, f_blk] + [b_blk] * 4 + [b_gl, b_blk],
        out_specs=[f_blk, b_blk],
        out_shape=[out, out],
        scratch_shapes=[pltpu.VMEM((n_par, PAIR, PAIR), jnp.float32)] * 2,
        compiler_params=_cparams(2),
        name="rwkv_scan",
    )(*fwd, v, *bwd, v)


def _rwkv_post_kernel(yf_ref, yb_ref, bonus_ref, gate_ref, gw_ref, gb_ref, o_ref):
    li = lax.broadcasted_iota(jnp.int32, (LANES, LANES), 0) // RWKV_HEAD
    lj = lax.broadcasted_iota(jnp.int32, (LANES, LANES), 1) // RWKV_HEAD
    mean_bd = jnp.where(li == lj, 1.0 / RWKV_HEAD, 0.0).astype(jnp.bfloat16)
    y = yf_ref[...] + yb_ref[...]
    d = y - _head_sum(y, mean_bd)
    var = _head_sum(d * d, mean_bd)
    yn = d * lax.rsqrt(var + RWKV_GN_EPS) * gw_ref[...] + gb_ref[...]
    o_ref[...] = ((yn + bonus_ref[...]) * gate_ref[...]).astype(o_ref.dtype)


def rwkv_post(yf, yb, bonus, gate, gn_w, gn_b, tm=256):
    t = yf.shape[0]
    row = pl.BlockSpec((tm, D_RWKV), lambda i: (i, 0))
    vec = pl.BlockSpec((1, D_RWKV), lambda i: (0, 0))
    return pl.pallas_call(
        _rwkv_post_kernel,
        grid=(t // tm,),
        in_specs=[row, row, row, row, vec, vec],
        out_specs=row,
        out_shape=jax.ShapeDtypeStruct((t, D_RWKV), jnp.bfloat16),
        compiler_params=_cparams(1),
        name="rwkv_post",
    )(yf, yb, bonus, gate, gn_w.reshape(1, -1), gn_b.reshape(1, -1))


SLOPE_PIECES = 3
SC_PER_HEAD = 2 + SLOPE_PIECES
ONES_ROWS = 16
V_ROWS = DIFF_HEAD + ONES_ROWS
UNDERFLOW_LOG2 = 160.0
BOUND_SLACK = 1.01


def _diff_attn_kernel(sc_ref, qt_ref, k_ref, vt_ref, g_ref, o_ref, s_ref, kmax_ref, *, tq, tk, lam_init):
    bf = jnp.bfloat16
    h = pl.program_id(0)
    qi = pl.program_id(1)
    n_kv = k_ref.shape[0] // tk
    lam = sc_ref[0]
    base = 1 + h * SC_PER_HEAD
    slope = sc_ref[base]
    pieces = [sc_ref[base + 1 + i] for i in range(SLOPE_PIECES)]
    inv_slope = sc_ref[base + 1 + SLOPE_PIECES]

    @pl.when(qi == 0)
    def _():
        kmax_ref[0] = jnp.max(jnp.abs(k_ref[...].astype(jnp.float32)))

    qt = qt_ref[...]
    q_abs = jnp.abs(qt.astype(jnp.float32))
    q_l1 = jnp.maximum(jnp.sum(q_abs[:DIFF_D], axis=0, keepdims=True),
                       jnp.sum(q_abs[DIFF_D:], axis=0, keepdims=True))
    bound = jnp.max(q_l1) * kmax_ref[0]
    reach = (2.0 * BOUND_SLACK * bound + UNDERFLOW_LOG2) * inv_slope
    width = jnp.int32(1)
    for d in range(1, n_kv):
        width = width + jnp.where(reach < float(d * tk), 0, 1).astype(jnp.int32)
    qrow = lax.broadcasted_iota(jnp.int32, (DIFF_HEAD, tq), 0)
    feat = jnp.zeros((DIFF_HEAD, tq), jnp.float32)
    for i, pc in enumerate(pieces):
        feat = jnp.where(qrow % DIFF_D == i, 2.0 * pc, feat)
        feat = jnp.where(qrow % DIFF_D == SLOPE_PIECES + i, pc, feat)
    feat = feat.astype(bf)
    own_rows = [qrow < DIFF_D, qrow >= DIFF_D]
    qts = [jnp.where(own, qt, feat) for own in own_rows]

    kl = lax.broadcasted_iota(jnp.int32, (tk, LANES), 1)
    kr = lax.broadcasted_iota(jnp.int32, (tk, LANES), 0)
    kfeat = jnp.where(kl % DIFF_D < SLOPE_PIECES, kr >> 1,
                      jnp.where(kl % DIFF_D < 2 * SLOPE_PIECES, kr & 1, 0)).astype(jnp.float32)
    kfeat_before = kfeat.astype(bf)
    kfeat_after = (-kfeat).astype(bf)
    kzero = jnp.zeros((tk, LANES), bf)
    own_lanes = [kl < DIFF_D, kl >= DIFF_D]

    qpos = qi * tq + lax.broadcasted_iota(jnp.int32, (1, tq), 1)
    q_bias = slope * qpos.astype(jnp.float32)

    def online(carry, s, t_q, vt):
        m, acc = carry
        m_new = jnp.maximum(m, jnp.max(s, axis=0, keepdims=True) + t_q)
        alpha = jnp.exp2(m - m_new)
        p = jnp.exp2(s - (m_new - t_q))
        return m_new, alpha * acc + _dot(vt, p.astype(bf))

    jd = (qi * tq) // tk
    lo = jnp.maximum(jd - width, 0)
    hi = jnp.minimum(jd + width, n_kv - 1)
    n_off = hi - lo

    def tile_of(n):
        j = lo + n
        return j + (j >= jd).astype(jnp.int32)

    def scores(n, slot, mp):
        j = tile_of(n)
        start = pl.multiple_of(j * tk, tk)
        kb = k_ref[pl.ds(start, tk), :]
        kf = jnp.where(j < jd, kfeat_before, kfeat_after)
        s_ref[slot, mp] = _dot(jnp.where(own_lanes[mp], kb, kf), qts[mp])

    def consume(n, slot, mp, carry):
        j = tile_of(n)
        start = pl.multiple_of(j * tk, tk)
        vt = vt_ref[:, pl.ds(start, tk)]
        sign = jnp.where(j < jd, 1.0, -1.0)
        t_q = sign * (slope * (j * tk).astype(jnp.float32) - q_bias)
        return online(carry, s_ref[slot, mp], t_q, vt)

    def step(n_next, n_cur, slot_next, slot_cur, carries):
        out = []
        for mp in range(2):
            scores(n_next, slot_next, mp)
            out.append(consume(n_cur, slot_cur, mp, carries[mp]))
        return tuple(out)

    start_d = pl.multiple_of(jd * tk, tk)
    kb_d = k_ref[pl.ds(start_d, tk), :]
    vt_d = vt_ref[:, pl.ds(start_d, tk)]
    kpos = jd * tk + lax.broadcasted_iota(jnp.int32, (tk, tq), 0)
    bias = slope * jnp.abs(kpos - qpos).astype(jnp.float32)
    init = (jnp.full((1, tq), -jnp.inf, jnp.float32), jnp.zeros((V_ROWS, tq), jnp.float32))
    for mp, (own, q_m) in enumerate(zip(own_lanes, qts)):
        s_ref[1, mp] = _dot(jnp.where(own, kb_d, kzero), q_m) - bias
    carries = []
    for mp in range(2):
        scores(jnp.int32(0), 0, mp)
        carries.append(online(init, s_ref[1, mp], 0.0, vt_d))

    def body(i, carries):
        n = 2 * i
        carries = step(n + 1, n, 1, 0, carries)
        return step(n + 2, n + 1, 0, 1, carries)

    n_pairs = (n_off - 1) // 2
    carries = lax.fori_loop(0, n_pairs, body, tuple(carries))
    last = n_off - 1

    def tail_two(carries):
        carries = step(last, last - 1, 1, 0, carries)
        return tuple(consume(last, 1, mp, carries[mp]) for mp in range(2))

    def tail_one(carries):
        return tuple(consume(last, 0, mp, carries[mp]) for mp in range(2))

    (_, acc0), (_, acc1) = lax.cond(last == 2 * n_pairs + 1, tail_two, tail_one, carries)
    o0 = acc0[:DIFF_HEAD] / acc0[DIFF_HEAD:DIFF_HEAD + 1]
    o1 = acc1[:DIFF_HEAD] / acc1[DIFF_HEAD:DIFF_HEAD + 1]
    o = (o0 - lam * o1).T
    o_ref[...] = (_rms(o, g_ref[...], SUBLN_EPS) * (1.0 - lam_init)).astype(o_ref.dtype)


def _bf16_pieces(x, n):
    out = []
    for _ in range(n):
        p = (x.view(np.uint32) & np.uint32(0xFFFF0000)).view(np.float32)
        out.append(p)
        x = (x - p).astype(np.float32)
    return out


def diff_attention(qt, k, vt, lam, subln_g, lam_init, tq=512, tk=512):
    t = k.shape[0]
    assert tk % tq == 0 and tk // 2 <= 256
    assert t // tk >= 2
    slopes = 2.0 ** (-ALIBI_MAX_EXP * np.arange(1, N_DIFF_HEADS + 1, dtype=np.float64) / N_DIFF_HEADS)
    slopes = (slopes * math.log2(math.e)).astype(np.float32)
    pieces = _bf16_pieces(slopes, SLOPE_PIECES)
    slope_used = sum(pieces)
    per_head = np.stack([slope_used] + pieces + [(1.0 / slope_used).astype(np.float32)], axis=1)
    lam = jnp.concatenate([lam, jnp.asarray(per_head.reshape(-1))])
    return pl.pallas_call(
        functools.partial(_diff_attn_kernel, tq=tq, tk=tk, lam_init=lam_init),
        grid=(N_DIFF_HEADS, t // tq),
        in_specs=[pl.BlockSpec(memory_space=pltpu.SMEM),
                  pl.BlockSpec((DIFF_HEAD, tq), lambda h, i: (h, i)),
                  pl.BlockSpec((t, DIFF_HEAD), lambda h, i: (0, h)),
                  pl.BlockSpec((V_ROWS, t), lambda h, i: (h, 0)),
                  pl.BlockSpec((1, DIFF_HEAD), lambda h, i: (0, 0))],
        out_specs=pl.BlockSpec((tq, DIFF_HEAD), lambda h, i: (i, h)),
        out_shape=jax.ShapeDtypeStruct((t, D_DIFF), jnp.bfloat16),
        scratch_shapes=[pltpu.VMEM((2, 2, tk, tq), jnp.float32),
                        pltpu.SMEM((1,), jnp.float32)],
        compiler_params=_cparams(2),
        name="diff_attn",
    )(lam, qt, k, vt, subln_g.reshape(1, DIFF_HEAD))


def _pad_cols(w, to):
    return jnp.pad(w, ((0, 0), (0, to - w.shape[1])))


def _layer(h, l, lam_init, x_norm, prm):
    bf = jnp.bfloat16
    f32 = jnp.float32
    f1 = swiglu_ffn(x_norm, prm["ffn1_w_gate"][l], prm["ffn1_w_up"][l],
                    prm["ffn1_w_down"][l].astype(bf), 1024, 256, 512, 512)
    h, xn = resid_norm(h, f1, prm["ffn1_post_g"][l], prm["mix_pre_g"][l], FFN_RESIDUAL, bf)

    w_in = prm["w_in"][l]
    c_rkv = 3 * D_RWKV
    c_w = c_rkv + DECAY_RANK
    c_a = c_w + ICLR_RANK
    c_g = c_a + GATE_RANK
    w_small = jnp.concatenate([_pad_cols(w_in[:, c_rkv:c_w], RANK_PAD),
                               _pad_cols(w_in[:, c_w:c_a], RANK_PAD),
                               w_in[:, c_a:c_g]], axis=1)
    w_rwkv = jnp.concatenate([w_in[:, :c_rkv], w_small], axis=1).astype(bf)
    proj = matmul(xn, w_rwkv, f32, 1024, 512)
    q_scale = DIFF_D ** -0.5 * math.log2(math.e)
    col_scale = jnp.concatenate([jnp.full((1, D_DIFF), q_scale, f32), jnp.ones((1, 2 * D_DIFF), f32)], axis=1)
    qkv = matmul_col_scaled(xn, w_in[:, c_g:].astype(bf), col_scale, bf, 1024, 512)

    def small_vec(a):
        return jnp.concatenate([_pad_cols(a[None, c_rkv:c_w], RANK_PAD),
                                _pad_cols(a[None, c_w:c_a], RANK_PAD),
                                a[None, c_a:c_g]], axis=1)

    def pad_rows(w):
        return jnp.pad(w, ((0, RANK_PAD - w.shape[0]), (0, 0))).astype(bf)

    mu_p, mu_n = prm["mu_prev"][l], prm["mu_next"][l]
    vecs = [prm[n][l].reshape(1, D_RWKV) for n in ("w0_f", "w0_b", "a0_f", "a0_b", "k_k", "k_a", "r_k")]
    mats = [pad_rows(prm["w2_f"][l]), pad_rows(prm["w2_b"][l]),
            pad_rows(prm["a2_f"][l]), pad_rows(prm["a2_b"][l]), prm["g2"][l].astype(bf)]
    prep = rwkv_prep(proj, [mu_p[None, :c_rkv], mu_n[None, :c_rkv], small_vec(mu_p), small_vec(mu_n)]
                     + vecs + mats)
    fwd, bwd, (v_r, gate, bonus) = prep[0:5], prep[5:10], prep[10:13]
    yf, yb = rwkv_scan(fwd, bwd, v_r)
    y_a = rwkv_post(yf, yb, bonus, gate, prm["gn_w"][l], prm["gn_b"][l])

    lam = (jnp.exp(jnp.sum(prm["lq1"][l] * prm["lk1"][l]))
           - jnp.exp(jnp.sum(prm["lq2"][l] * prm["lk2"][l])) + lam_init).reshape(1)
    t = qkv.shape[0]
    vt = qkv[:, 2 * D_DIFF:].T.reshape(N_DIFF_HEADS, DIFF_HEAD, t)
    vt = jnp.concatenate([vt, jnp.ones((N_DIFF_HEADS, ONES_ROWS, t), bf)], axis=1).reshape(N_DIFF_HEADS * V_ROWS, t)
    y_b = diff_attention(qkv[:, :D_DIFF].T, qkv[:, D_DIFF:2 * D_DIFF], vt, lam, prm["subln_g"][l], lam_init)

    w_out = prm["w_out"][l]
    mix = matmul_two(y_a, y_b, w_out[:D_RWKV].astype(bf), w_out[D_RWKV:].astype(bf), f32, 1024, 512)
    h, xn = resid_norm(h, mix, prm["mix_post_g"][l], prm["ffn2_pre_g"][l], 1.0, bf)

    f2 = swiglu_ffn(xn, prm["ffn2_w_gate"][l], prm["ffn2_w_up"][l],
                    prm["ffn2_w_down"][l].astype(bf), 1024, 256, 512, 512)
    return resid_norm(h, f2, prm["ffn2_post_g"][l], prm["final_g"][l], FFN_RESIDUAL, f32)


def kernel(x, ffn1_pre_g, ffn1_w_gate, ffn1_w_up, ffn1_w_down, ffn1_post_g, mix_pre_g, w_in, mu_prev, mu_next, w0_f, w2_f, w0_b, w2_b, a0_f, a2_f, a0_b, a2_b, g2, k_k, k_a, r_k, gn_w, gn_b, lq1, lk1, lq2, lk2, subln_g, w_out, mix_post_g, ffn2_pre_g, ffn2_w_gate, ffn2_w_up, ffn2_w_down, ffn2_post_g, final_g):
    prm = dict(locals())
    bsz, t, d = x.shape
    assert bsz == 1
    depth = ffn1_pre_g.shape[0]
    h = x.reshape(t, d)
    for l in range(depth):
        lam_init = 0.8 - 0.6 * math.exp(-0.3 * l)
        x_norm = rms_norm_cast(h, ffn1_pre_g[l], jnp.bfloat16)
        _, h = _layer(h, l, lam_init, x_norm, prm)
    return h.reshape(bsz, t, d)
```

```python
import functools
import math

import jax
import jax.numpy as jnp
import numpy as np
from jax import lax
from jax.experimental import pallas as pl
from jax.experimental.pallas import tpu as pltpu

D_MODEL = 4096
D_RWKV = 2048
D_DIFF = 2048
RWKV_HEAD = 64
DECAY_RANK = 96
ICLR_RANK = 96
GATE_RANK = 256
DIFF_D = 64
DIFF_HEAD = 128
N_DIFF_HEADS = 16
ALIBI_MAX_EXP = 8.0
NORM_EPS = 1e-6
RWKV_GN_EPS = 64e-5
SUBLN_EPS = 1e-5
FFN_RESIDUAL = 0.5

LANES = 128
SUBLANES = 8
VMEM_LIMIT = 56 * 1024 * 1024

RANK_PAD = 128
SMALL_COLS = 2 * RANK_PAD + GATE_RANK
CHUNK = 64
PAIR = 2 * RWKV_HEAD


def _cparams(n_axes):
    return pltpu.CompilerParams(
        dimension_semantics=("arbitrary",) * n_axes, vmem_limit_bytes=VMEM_LIMIT)


def _dot(a, b, dims=(((1,), (0,)), ((), ()))):
    return lax.dot_general(a, b, dims, preferred_element_type=jnp.float32)


_NT = (((1,), (1,)), ((), ()))
_TN = (((0,), (0,)), ((), ()))


def _rms(x, g, eps):
    return x * lax.rsqrt(jnp.mean(x * x, axis=-1, keepdims=True) + eps) * g


def _norm_kernel(x_ref, g_ref, o_ref):
    o_ref[...] = _rms(x_ref[...], g_ref[...], NORM_EPS).astype(o_ref.dtype)


def rms_norm_cast(x, g, out_dtype, tm=256):
    m, d = x.shape
    return pl.pallas_call(
        _norm_kernel,
        grid=(m // tm,),
        in_specs=[pl.BlockSpec((tm, d), lambda i: (i, 0)),
                  pl.BlockSpec((1, d), lambda i: (0, 0))],
        out_specs=pl.BlockSpec((tm, d), lambda i: (i, 0)),
        out_shape=jax.ShapeDtypeStruct((m, d), out_dtype),
        compiler_params=_cparams(1),
        name="rms_norm",
    )(x, g.reshape(1, d))


def _resid_norm_kernel(h_ref, f_ref, gp_ref, gn_ref, h_out_ref, n_out_ref, *, scale):
    h = h_ref[...] + scale * _rms(f_ref[...], gp_ref[...], NORM_EPS)
    h_out_ref[...] = h
    n_out_ref[...] = _rms(h, gn_ref[...], NORM_EPS).astype(n_out_ref.dtype)


def resid_norm(h, f, g_post, g_next, scale, out_dtype, tm=256):
    m, d = h.shape
    row = pl.BlockSpec((tm, d), lambda i: (i, 0))
    vec = pl.BlockSpec((1, d), lambda i: (0, 0))
    return pl.pallas_call(
        functools.partial(_resid_norm_kernel, scale=scale),
        grid=(m // tm,),
        in_specs=[row, row, vec, vec],
        out_specs=[row, row],
        out_shape=[jax.ShapeDtypeStruct((m, d), jnp.float32),
                   jax.ShapeDtypeStruct((m, d), out_dtype)],
        compiler_params=_cparams(1),
        name="resid_norm",
    )(h, f, g_post.reshape(1, d), g_next.reshape(1, d))


def _resid_norm_out_kernel(h_ref, f_ref, gp_ref, gn_ref, n_out_ref, *, scale):
    h = h_ref[...] + scale * _rms(f_ref[...], gp_ref[...], NORM_EPS)
    n_out_ref[...] = _rms(h, gn_ref[...], NORM_EPS).astype(n_out_ref.dtype)


def resid_norm_out(h, f, g_post, g_next, scale, tm=256):
    m, d = h.shape
    row = pl.BlockSpec((tm, d), lambda i: (i, 0))
    vec = pl.BlockSpec((1, d), lambda i: (0, 0))
    return pl.pallas_call(
        functools.partial(_resid_norm_out_kernel, scale=scale),
        grid=(m // tm,),
        in_specs=[row, row, vec, vec],
        out_specs=row,
        out_shape=jax.ShapeDtypeStruct((m, d), jnp.float32),
        compiler_params=_cparams(1),
        name="resid_norm_out",
    )(h, f, g_post.reshape(1, d), g_next.reshape(1, d))


def _mm_kernel(x_ref, w_ref, o_ref):
    o_ref[...] = _dot(x_ref[...], w_ref[...]).astype(o_ref.dtype)


def matmul(x, w, out_dtype, tm, tn):
    m, k = x.shape
    _, n = w.shape
    return pl.pallas_call(
        _mm_kernel,
        grid=(n // tn, m // tm),
        in_specs=[pl.BlockSpec((tm, k), lambda j, i: (i, 0)),
                  pl.BlockSpec((k, tn), lambda j, i: (0, j))],
        out_specs=pl.BlockSpec((tm, tn), lambda j, i: (i, j)),
        out_shape=jax.ShapeDtypeStruct((m, n), out_dtype),
        compiler_params=_cparams(2),
        name="matmul",
    )(x, w)


def _mm_two_kernel(xa_ref, xb_ref, wa_ref, wb_ref, o_ref):
    o_ref[...] = (_dot(xa_ref[...], wa_ref[...]) + _dot(xb_ref[...], wb_ref[...])).astype(o_ref.dtype)


def matmul_two(xa, xb, wa, wb, out_dtype, tm, tn):
    m, ka = xa.shape
    _, kb = xb.shape
    _, n = wa.shape
    return pl.pallas_call(
        _mm_two_kernel,
        grid=(n // tn, m // tm),
        in_specs=[pl.BlockSpec((tm, ka), lambda j, i: (i, 0)),
                  pl.BlockSpec((tm, kb), lambda j, i: (i, 0)),
                  pl.BlockSpec((ka, tn), lambda j, i: (0, j)),
                  pl.BlockSpec((kb, tn), lambda j, i: (0, j))],
        out_specs=pl.BlockSpec((tm, tn), lambda j, i: (i, j)),
        out_shape=jax.ShapeDtypeStruct((m, n), out_dtype),
        compiler_params=_cparams(2),
        name="matmul_two",
    )(xa, xb, wa, wb)


def _mm_scaled_kernel(x_ref, w_ref, cs_ref, o_ref):
    o_ref[...] = (_dot(x_ref[...], w_ref[...]) * cs_ref[...]).astype(o_ref.dtype)


def matmul_col_scaled(x, w, col_scale, out_dtype, tm, tn):
    m, k = x.shape
    _, n = w.shape
    return pl.pallas_call(
        _mm_scaled_kernel,
        grid=(n // tn, m // tm),
        in_specs=[pl.BlockSpec((tm, k), lambda j, i: (i, 0)),
                  pl.BlockSpec((k, tn), lambda j, i: (0, j)),
                  pl.BlockSpec((1, tn), lambda j, i: (0, j))],
        out_specs=pl.BlockSpec((tm, tn), lambda j, i: (i, j)),
        out_shape=jax.ShapeDtypeStruct((m, n), out_dtype),
        compiler_params=_cparams(2),
        name="matmul_col_scaled",
    )(x, w, col_scale)


def _gate_up_kernel(x_ref, wg_ref, wu_ref, o_ref, wg_bf_ref, wu_bf_ref):
    @pl.when(pl.program_id(1) == 0)
    def _():
        wg_bf_ref[...] = wg_ref[...].astype(jnp.bfloat16)
        wu_bf_ref[...] = wu_ref[...].astype(jnp.bfloat16)

    x = x_ref[...]
    g = _dot(x, wg_bf_ref[...])
    u = _dot(x, wu_bf_ref[...])
    o_ref[...] = (g * jax.nn.sigmoid(g) * u).astype(o_ref.dtype)


def gate_up(x, wg, wu, tm, tn):
    m, k = x.shape
    _, n = wg.shape
    wspec = pl.BlockSpec((k, tn), lambda j, i: (0, j))
    return pl.pallas_call(
        _gate_up_kernel,
        grid=(n // tn, m // tm),
        in_specs=[pl.BlockSpec((tm, k), lambda j, i: (i, 0)), wspec, wspec],
        out_specs=pl.BlockSpec((tm, tn), lambda j, i: (i, j)),
        out_shape=jax.ShapeDtypeStruct((m, n), jnp.bfloat16),
        scratch_shapes=[pltpu.VMEM((k, tn), jnp.bfloat16)] * 2,
        compiler_params=_cparams(2),
        name="gate_up",
    )(x, wg, wu)


def swiglu_ffn(xn, wg, wu, wd, tm_gu, tn_gu, tm_d, tn_d):
    a = gate_up(xn, wg, wu, tm_gu, tn_gu)
    return matmul(a, wd, jnp.float32, tm_d, tn_d)


def _bf16_split(x, n):
    pieces = []
    for _ in range(n - 1):
        p = x.astype(jnp.bfloat16)
        pieces.append(p)
        x = x - p.astype(jnp.float32)
    pieces.append(x.astype(jnp.bfloat16))
    return pieces


HEAD_SUM_PIECES = 2
CUMSUM_PIECES = 3


def _head_sum(x, w_bd):
    parts = []
    for s in range(x.shape[1] // LANES):
        pieces = _bf16_split(x[:, s * LANES:(s + 1) * LANES], HEAD_SUM_PIECES)
        parts.append(sum(_dot(p, w_bd) for p in pieces))
    return jnp.concatenate(parts, axis=1)


def _token_shift(p_ref, pp_ref, pn_ref, mup_ref, mun_ref, first, last):
    p = p_ref[...]
    rows = p.shape[0]
    rid = lax.broadcasted_iota(jnp.int32, p.shape, 0)
    prev_row = jnp.where(first, 0.0, pp_ref[SUBLANES - 1:SUBLANES, :])
    next_row = jnp.where(last, 0.0, pn_ref[0:1, :])
    p_prev = jnp.where(rid == 0, prev_row, pltpu.roll(p, 1, 0))
    p_next = jnp.where(rid == rows - 1, next_row, pltpu.roll(p, rows - 1, 0))
    return p + mup_ref[...] * (p_prev - p) + mun_ref[...] * (p_next - p)


def _rwkv_prep_kernel(
        p_ref, pp_ref, pn_ref, s_ref, sp_ref, sn_ref,
        mup_ref, mun_ref, mups_ref, muns_ref,
        w0f_ref, w0b_ref, a0f_ref, a0b_ref, kk_ref, ka_ref, rk_ref,
        w2f_ref, w2b_ref, a2f_ref, a2b_ref, g2_ref,
        atf_ref, btf_ref, ktf_ref, rtf_ref, glf_ref,
        atb_ref, btb_ref, ktb_ref, rtb_ref, glb_ref,
        v_ref, gate_ref, bonus_ref):
    i = pl.program_id(0)
    first = i == 0
    last = i == pl.num_programs(0) - 1
    tm = p_ref.shape[0]
    n_chunks = tm // CHUNK

    p = _token_shift(p_ref, pp_ref, pn_ref, mup_ref, mun_ref, first, last)
    s = _token_shift(s_ref, sp_ref, sn_ref, mups_ref, muns_ref, first, last)
    r = p[:, 0:D_RWKV]
    k = p[:, D_RWKV:2 * D_RWKV]
    v = p[:, 2 * D_RWKV:3 * D_RWKV]
    hw = jnp.tanh(s[:, 0:RANK_PAD]).astype(jnp.bfloat16)
    xa = s[:, RANK_PAD:2 * RANK_PAD].astype(jnp.bfloat16)
    sg = jax.nn.sigmoid(s[:, 2 * RANK_PAD:]).astype(jnp.bfloat16)

    li = lax.broadcasted_iota(jnp.int32, (LANES, LANES), 0) // RWKV_HEAD
    lj = lax.broadcasted_iota(jnp.int32, (LANES, LANES), 1) // RWKV_HEAD
    ones_bd = (li == lj).astype(jnp.bfloat16)

    kk = k * kk_ref[...]
    kk = kk / jnp.maximum(jnp.sqrt(_head_sum(kk * kk, ones_bd)), 1e-12)

    ti = lax.broadcasted_iota(jnp.int32, (tm, tm), 0)
    tj = lax.broadcasted_iota(jnp.int32, (tm, tm), 1)
    same_chunk = (ti // CHUNK) == (tj // CHUNK)

    k_sum = jnp.zeros_like(k)
    dirs = ((w0f_ref, w2f_ref, a0f_ref, a2f_ref, atf_ref, btf_ref, ktf_ref, rtf_ref, glf_ref, False),
            (w0b_ref, w2b_ref, a0b_ref, a2b_ref, atb_ref, btb_ref, ktb_ref, rtb_ref, glb_ref, True))
    for w0_ref, w2_ref, a0_ref, a2_ref, at_ref, bt_ref, kt_ref, rt_ref, gl_ref, rev in dirs:
        z = w0_ref[...] + _dot(hw, w2_ref[...])
        lw = -math.exp(-0.5) * jax.nn.sigmoid(z)
        a = jax.nn.sigmoid(a0_ref[...] + _dot(xa, a2_ref[...]))
        k_dir = k * (1.0 + (a - 1.0) * ka_ref[...])
        k_sum = k_sum + k_dir
        order = (tj >= ti) if rev else (tj <= ti)
        tri = jnp.where(same_chunk & order, 1.0, 0.0).astype(jnp.bfloat16)
        c = sum(_dot(tri, piece) for piece in _bf16_split(lw, CUMSUM_PIECES))
        e_neg = jnp.exp(-c)
        at_ref[...] = (-kk * jnp.exp(c - lw)).astype(at_ref.dtype)
        bt_ref[...] = (kk * a * e_neg).astype(bt_ref.dtype)
        kt_ref[...] = (k_dir * e_neg).astype(kt_ref.dtype)
        e_pos = jnp.exp(c)
        rt_ref[...] = (r * e_pos).astype(rt_ref.dtype)
        for j in range(n_chunks):
            end = j * CHUNK if rev else (j + 1) * CHUNK - 1
            gl_ref[j] = e_pos[end:end + 1, :]

    v_ref[...] = v.astype(v_ref.dtype)
    gate_ref[...] = _dot(sg, g2_ref[...])
    bonus_ref[...] = _head_sum(r * k_sum * rk_ref[...], ones_bd) * v


def rwkv_prep(proj, params, tm=128):
    t = proj.shape[0]
    nb = t // tm
    hb = tm // SUBLANES
    n_halo = t // SUBLANES
    wide = 3 * D_RWKV
    small_blk = wide // SMALL_COLS

    def row(c, cb=0):
        return pl.BlockSpec((tm, c), lambda i: (i, cb))

    def halo_prev(c, cb=0):
        return pl.BlockSpec((SUBLANES, c), lambda i: (jnp.maximum(i * hb - 1, 0), cb))

    def halo_next(c, cb=0):
        return pl.BlockSpec((SUBLANES, c), lambda i: (jnp.minimum((i + 1) * hb, n_halo - 1), cb))

    def vec(c):
        return pl.BlockSpec((1, c), lambda i: (0, 0))

    def mat(r_, c):
        return pl.BlockSpec((r_, c), lambda i: (0, 0))

    gl_spec = pl.BlockSpec((tm // CHUNK, 1, D_RWKV), lambda i: (i, 0, 0))
    big = jax.ShapeDtypeStruct((t, D_RWKV), jnp.float32)
    gl = jax.ShapeDtypeStruct((t // CHUNK, 1, D_RWKV), jnp.float32)
    in_specs = [row(wide), halo_prev(wide), halo_next(wide),
                row(SMALL_COLS, small_blk), halo_prev(SMALL_COLS, small_blk), halo_next(SMALL_COLS, small_blk),
                vec(wide), vec(wide), vec(SMALL_COLS), vec(SMALL_COLS)]
    in_specs += [vec(D_RWKV)] * 7
    in_specs += [mat(RANK_PAD, D_RWKV)] * 4 + [mat(GATE_RANK, D_RWKV)]
    dir_specs = [row(D_RWKV)] * 4 + [gl_spec]
    big_bf = jax.ShapeDtypeStruct((t, D_RWKV), jnp.bfloat16)
    dir_shapes = [big_bf] * 4 + [gl]
    return pl.pallas_call(
        _rwkv_prep_kernel,
        grid=(nb,),
        in_specs=in_specs,
        out_specs=dir_specs + dir_specs + [row(D_RWKV)] * 3,
        out_shape=dir_shapes + dir_shapes + [big_bf, big, big],
        compiler_params=_cparams(1),
        name="rwkv_prep",
    )(proj, proj, proj, proj, proj, proj, *params)


def _stack(x, lane_head):
    zero = jnp.zeros_like(x)
    return jnp.concatenate([jnp.where(lane_head == 0, x, zero),
                            jnp.where(lane_head == 1, x, zero)], axis=0)


def _scan_masks(rev):
    n2 = 2 * CHUNK
    ri = lax.broadcasted_iota(jnp.int32, (n2, n2), 0)
    ci = lax.broadcasted_iota(jnp.int32, (n2, n2), 1)
    same = (ri // CHUNK) == (ci // CHUNK)
    before = (ci > ri) if rev else (ci < ri)
    strict = same & before
    incl = same & (before | (ri == ci))
    eye = jnp.where(ri == ci, 1.0, 0.0)
    return strict, jnp.concatenate([incl, incl], axis=1), eye


def _scan_prepare(units):
    bf = jnp.bfloat16
    n2 = 2 * CHUNK
    lane_head = lax.broadcasted_iota(jnp.int32, (CHUNK, PAIR), 1) // RWKV_HEAD
    stacked = [[_stack(x, lane_head) for x in u[:5]] for u in units]
    bks = [jnp.concatenate([s[1], s[2]], axis=0) for s in stacked]
    g_as = [_dot(s[0], bk, _NT) for s, bk in zip(stacked, bks)]
    g_rs = [_dot(s[3], bk, _NT) for s, bk in zip(stacked, bks)]
    xs = [jnp.where(u[6][0], g[:, :n2], 0.0) for u, g in zip(units, g_as)]
    a_aks = [jnp.where(u[6][0], g[:, n2:], 0.0).astype(bf) for u, g in zip(units, g_as)]
    a_rs = [jnp.where(u[6][1], g, 0.0).astype(bf) for u, g in zip(units, g_rs)]
    akv = [_dot(a, s[4]).astype(bf) for a, s in zip(a_aks, stacked)]

    tinvs = [u[6][2] + x for u, x in zip(units, xs)]
    for _ in range(int(math.log2(CHUNK)) - 1):
        xbs = [x.astype(bf) for x in xs]
        xs = [_dot(xb, xb) for xb in xbs]
        tinvs = [t + _dot(t.astype(bf), x.astype(bf)) for t, x in zip(tinvs, xs)]

    wqs = [_dot(t.astype(bf), jnp.concatenate([s[0], kv], axis=1))
           for t, s, kv in zip(tinvs, stacked, akv)]
    out = []
    for u, s, bk, wq, a_r in zip(units, stacked, bks, wqs, a_rs):
        gl = u[5]
        w = wq[:, :PAIR].astype(bf)
        q = wq[:, PAIR:]
        bkg = (bk * gl).astype(bf)
        m = _dot(w, bkg[:n2], _TN)
        n = _dot(jnp.concatenate([q.astype(bf), s[4]], axis=0), bkg, _TN)
        out.append((w, q, s[3], a_r, s[4], m.astype(bf), n, gl))
    return out


def _scan_output(prep, s0b):
    w, q, r2, a_r, v2 = prep[:5]
    u = (_dot(w, s0b, _NT) + q).astype(jnp.bfloat16)
    y2 = _dot(r2, s0b, _NT) + _dot(a_r, jnp.concatenate([u, v2], axis=0))
    return y2[:CHUNK] + y2[CHUNK:]


def _rwkv_scan_kernel(atf_ref, btf_ref, ktf_ref, rtf_ref, glf_ref, vf_ref,
                      atb_ref, btb_ref, ktb_ref, rtb_ref, glb_ref, vb_ref,
                      yf_ref, yb_ref, sf_ref, sb_ref, *, n_blk, n_par):
    @pl.when(pl.program_id(1) == 0)
    def _():
        sf_ref[...] = jnp.zeros_like(sf_ref)
        sb_ref[...] = jnp.zeros_like(sb_ref)

    def tile(j, p):
        return pl.ds(j * CHUNK, CHUNK), pl.ds(p * PAIR, PAIR)

    masks_f = _scan_masks(False)
    masks_b = _scan_masks(True)
    units = []
    for p in range(n_par):
        for j in range(n_blk):
            units.append(tuple(ref[tile(j, p)] for ref in (atf_ref, btf_ref, ktf_ref, rtf_ref, vf_ref))
                         + (glf_ref[j, :, pl.ds(p * PAIR, PAIR)], masks_f))
            units.append(tuple(ref[tile(j, p)] for ref in (atb_ref, btb_ref, ktb_ref, rtb_ref, vb_ref))
                         + (glb_ref[j, :, pl.ds(p * PAIR, PAIR)], masks_b))
    prep = _scan_prepare(units)

    sf = [sf_ref[p] for p in range(n_par)]
    sb = [sb_ref[p] for p in range(n_par)]
    for j in range(n_blk):
        jb = n_blk - 1 - j
        for p in range(n_par):
            pf = prep[2 * (p * n_blk + j)]
            pb = prep[2 * (p * n_blk + jb) + 1]
            sfb = sf[p].astype(jnp.bfloat16)
            sbb = sb[p].astype(jnp.bfloat16)
            sf[p] = sf[p] * pf[7] + _dot(sfb, pf[5]) + pf[6]
            sb[p] = sb[p] * pb[7] + _dot(sbb, pb[5]) + pb[6]
            yf_ref[tile(j, p)] = _scan_output(pf, sfb)
            yb_ref[tile(jb, p)] = _scan_output(pb, sbb)
    for p in range(n_par):
        sf_ref[p] = sf[p]
        sb_ref[p] = sb[p]


def rwkv_scan(fwd, bwd, v, n_blk=4, n_par=2):
    t = v.shape[0]
    nb = t // (CHUNK * n_blk)
    n_grp = D_RWKV // (PAIR * n_par)
    width = PAIR * n_par
    f_blk = pl.BlockSpec((CHUNK * n_blk, width), lambda p, c: (c, p))
    b_blk = pl.BlockSpec((CHUNK * n_blk, width), lambda p, c: (nb - 1 - c, p))
    f_gl = pl.BlockSpec((n_blk, 1, width), lambda p, c: (c, 0, p))
    b_gl = pl.BlockSpec((n_blk, 1, width), lambda p, c: (nb - 1 - c, 0, p))
    out = jax.ShapeDtypeStruct((t, D_RWKV), jnp.float32)
    return pl.pallas_call(
        functools.partial(_rwkv_scan_kernel, n_blk=n_blk, n_par=n_par),
        grid=(n_grp, nb),
        in_specs=[f_blk] * 4 + [f_gl, f_blk] + [b_blk] * 4 + [b_gl, b_blk],
        out_specs=[f_blk, b_blk],
        out_shape=[out, out],
        scratch_shapes=[pltpu.VMEM((n_par, PAIR, PAIR), jnp.float32)] * 2,
        compiler_params=_cparams(2),
        name="rwkv_scan",
    )(*fwd, v, *bwd, v)


def _rwkv_post_kernel(yf_ref, yb_ref, bonus_ref, gate_ref, gw_ref, gb_ref, o_ref):
    li = lax.broadcasted_iota(jnp.int32, (LANES, LANES), 0) // RWKV_HEAD
    lj = lax.broadcasted_iota(jnp.int32, (LANES, LANES), 1) // RWKV_HEAD
    mean_bd = jnp.where(li == lj, 1.0 / RWKV_HEAD, 0.0).astype(jnp.bfloat16)
    y = yf_ref[...] + yb_ref[...]
    d = y - _head_sum(y, mean_bd)
    var = _head_sum(d * d, mean_bd)
    yn = d * lax.rsqrt(var + RWKV_GN_EPS) * gw_ref[...] + gb_ref[...]
    o_ref[...] = ((yn + bonus_ref[...]) * gate_ref[...]).astype(o_ref.dtype)


def rwkv_post(yf, yb, bonus, gate, gn_w, gn_b, tm=256):
    t = yf.shape[0]
    row = pl.BlockSpec((tm, D_RWKV), lambda i: (i, 0))
    vec = pl.BlockSpec((1, D_RWKV), lambda i: (0, 0))
    return pl.pallas_call(
        _rwkv_post_kernel,
        grid=(t // tm,),
        in_specs=[row, row, row, row, vec, vec],
        out_specs=row,
        out_shape=jax.ShapeDtypeStruct((t, D_RWKV), jnp.bfloat16),
        compiler_params=_cparams(1),
        name="rwkv_post",
    )(yf, yb, bonus, gate, gn_w.reshape(1, -1), gn_b.reshape(1, -1))


SLOPE_PIECES = 3
SC_PER_HEAD = 2 + SLOPE_PIECES
ONES_ROWS = 16
V_ROWS = DIFF_HEAD + ONES_ROWS
UNDERFLOW_LOG2 = 160.0
BOUND_SLACK = 1.01


def _diff_attn_kernel(sc_ref, qt_ref, k_ref, vt_ref, g_ref, o_ref, s_ref, kmax_ref, *, tq, tk, lam_init):
    bf = jnp.bfloat16
    h = pl.program_id(0)
    qi = pl.program_id(1)
    n_kv = k_ref.shape[0] // tk
    lam = sc_ref[0]
    base = 1 + h * SC_PER_HEAD
    slope = sc_ref[base]
    pieces = [sc_ref[base + 1 + i] for i in range(SLOPE_PIECES)]
    inv_slope = sc_ref[base + 1 + SLOPE_PIECES]

    @pl.when(qi == 0)
    def _():
        li = lax.broadcasted_iota(jnp.int32, (LANES, LANES), 0) // DIFF_D
        lj = lax.broadcasted_iota(jnp.int32, (LANES, LANES), 1) // DIFF_D
        half_ones = (li == lj).astype(bf)

        def tile_max(j, best):
            kf = k_ref[pl.ds(pl.multiple_of(j * tk, tk), tk), :].astype(jnp.float32)
            sq = sum(_dot(p, half_ones) for p in _bf16_split(kf * kf, HEAD_SUM_PIECES))
            return jnp.maximum(best, jnp.max(sq, axis=0, keepdims=True))

        best = lax.fori_loop(0, n_kv, tile_max, jnp.zeros((1, LANES), jnp.float32))
        kmax_ref[0] = jnp.max(best)

    qt = qt_ref[...]
    q_sq = jnp.square(qt.astype(jnp.float32))
    q_n2 = jnp.maximum(jnp.sum(q_sq[:DIFF_D], axis=0, keepdims=True),
                       jnp.sum(q_sq[DIFF_D:], axis=0, keepdims=True))
    bound = jnp.max(jnp.sqrt(q_n2 * kmax_ref[0]))
    reach = (2.0 * BOUND_SLACK * bound + UNDERFLOW_LOG2) * inv_slope
    width = jnp.int32(1)
    for d in range(1, n_kv):
        width = width + jnp.where(reach < float(d * tk), 0, 1).astype(jnp.int32)
    qrow = lax.broadcasted_iota(jnp.int32, (DIFF_HEAD, tq), 0)
    feat = jnp.zeros((DIFF_HEAD, tq), jnp.float32)
    for i, pc in enumerate(pieces):
        feat = jnp.where(qrow % DIFF_D == i, 2.0 * pc, feat)
        feat = jnp.where(qrow % DIFF_D == SLOPE_PIECES + i, pc, feat)
    feat = feat.astype(bf)
    own_rows = [qrow < DIFF_D, qrow >= DIFF_D]
    qts = [jnp.where(own, qt, feat) for own in own_rows]

    kl = lax.broadcasted_iota(jnp.int32, (tk, LANES), 1)
    kr = lax.broadcasted_iota(jnp.int32, (tk, LANES), 0)
    kfeat = jnp.where(kl % DIFF_D < SLOPE_PIECES, kr >> 1,
                      jnp.where(kl % DIFF_D < 2 * SLOPE_PIECES, kr & 1, 0)).astype(jnp.float32)
    kfeat_before = kfeat.astype(bf)
    kfeat_after = (-kfeat).astype(bf)
    kzero = jnp.zeros((tk, LANES), bf)
    own_lanes = [kl < DIFF_D, kl >= DIFF_D]

    qpos = qi * tq + lax.broadcasted_iota(jnp.int32, (1, tq), 1)
    q_bias = slope * qpos.astype(jnp.float32)

    def online(carry, s, t_q, vt):
        m, acc = carry
        m_new = jnp.maximum(m, jnp.max(s, axis=0, keepdims=True) + t_q)
        alpha = jnp.exp2(m - m_new)
        p = jnp.exp2(s - (m_new - t_q))
        return m_new, alpha * acc + _dot(vt, p.astype(bf))

    jd = (qi * tq) // tk
    lo = jnp.maximum(jd - width, 0)
    hi = jnp.minimum(jd + width, n_kv - 1)
    n_off = hi - lo

    def tile_of(n):
        j = lo + n
        return j + (j >= jd).astype(jnp.int32)

    def scores(n, slot, mp):
        j = tile_of(n)
        start = pl.multiple_of(j * tk, tk)
        kb = k_ref[pl.ds(start, tk), :]
        kf = jnp.where(j < jd, kfeat_before, kfeat_after)
        s_ref[slot, mp] = _dot(jnp.where(own_lanes[mp], kb, kf), qts[mp])

    def consume(n, slot, mp, carry):
        j = tile_of(n)
        start = pl.multiple_of(j * tk, tk)
        vt = vt_ref[:, pl.ds(start, tk)]
        sign = jnp.where(j < jd, 1.0, -1.0)
        t_q = sign * (slope * (j * tk).astype(jnp.float32) - q_bias)
        return online(carry, s_ref[slot, mp], t_q, vt)

    def step(n_next, n_cur, slot_next, slot_cur, carries):
        out = []
        for mp in range(2):
            scores(n_next, slot_next, mp)
            out.append(consume(n_cur, slot_cur, mp, carries[mp]))
        return tuple(out)

    start_d = pl.multiple_of(jd * tk, tk)
    kb_d = k_ref[pl.ds(start_d, tk), :]
    vt_d = vt_ref[:, pl.ds(start_d, tk)]
    kpos = jd * tk + lax.broadcasted_iota(jnp.int32, (tk, tq), 0)
    bias = slope * jnp.abs(kpos - qpos).astype(jnp.float32)
    init = (jnp.full((1, tq), -jnp.inf, jnp.float32), jnp.zeros((V_ROWS, tq), jnp.float32))
    for mp, (own, q_m) in enumerate(zip(own_lanes, qts)):
        s_ref[1, mp] = _dot(jnp.where(own, kb_d, kzero), q_m) - bias
    carries = []
    for mp in range(2):
        scores(jnp.int32(0), 0, mp)
        carries.append(online(init, s_ref[1, mp], 0.0, vt_d))

    def body(i, carries):
        n = 2 * i
        carries = step(n + 1, n, 1, 0, carries)
        return step(n + 2, n + 1, 0, 1, carries)

    n_pairs = (n_off - 1) // 2
    carries = lax.fori_loop(0, n_pairs, body, tuple(carries))
    last = n_off - 1

    def tail_two(carries):
        carries = step(last, last - 1, 1, 0, carries)
        return tuple(consume(last, 1, mp, carries[mp]) for mp in range(2))

    def tail_one(carries):
        return tuple(consume(last, 0, mp, carries[mp]) for mp in range(2))

    (_, acc0), (_, acc1) = lax.cond(last == 2 * n_pairs + 1, tail_two, tail_one, carries)
    o0 = acc0[:DIFF_HEAD] / acc0[DIFF_HEAD:DIFF_HEAD + 1]
    o1 = acc1[:DIFF_HEAD] / acc1[DIFF_HEAD:DIFF_HEAD + 1]
    o = (o0 - lam * o1).T
    o_ref[...] = (_rms(o, g_ref[...], SUBLN_EPS) * (1.0 - lam_init)).astype(o_ref.dtype)


def _bf16_pieces(x, n):
    out = []
    for _ in range(n):
        p = (x.view(np.uint32) & np.uint32(0xFFFF0000)).view(np.float32)
        out.append(p)
        x = (x - p).astype(np.float32)
    return out


def diff_attention(qt, k, vt, lam, subln_g, lam_init, tq=512, tk=512):
    t = k.shape[0]
    assert tk % tq == 0 and tk // 2 <= 256
    assert t // tk >= 2
    slopes = 2.0 ** (-ALIBI_MAX_EXP * np.arange(1, N_DIFF_HEADS + 1, dtype=np.float64) / N_DIFF_HEADS)
    slopes = (slopes * math.log2(math.e)).astype(np.float32)
    pieces = _bf16_pieces(slopes, SLOPE_PIECES)
    slope_used = sum(pieces)
    per_head = np.stack([slope_used] + pieces + [(1.0 / slope_used).astype(np.float32)], axis=1)
    lam = jnp.concatenate([lam, jnp.asarray(per_head.reshape(-1))])
    return pl.pallas_call(
        functools.partial(_diff_attn_kernel, tq=tq, tk=tk, lam_init=lam_init),
        grid=(N_DIFF_HEADS, t // tq),
        in_specs=[pl.BlockSpec(memory_space=pltpu.SMEM),
                  pl.BlockSpec((DIFF_HEAD, tq), lambda h, i: (h, i)),
                  pl.BlockSpec((t, DIFF_HEAD), lambda h, i: (0, h)),
                  pl.BlockSpec((V_ROWS, t), lambda h, i: (h, 0)),
                  pl.BlockSpec((1, DIFF_HEAD), lambda h, i: (0, 0))],
        out_specs=pl.BlockSpec((tq, DIFF_HEAD), lambda h, i: (i, h)),
        out_shape=jax.ShapeDtypeStruct((t, D_DIFF), jnp.bfloat16),
        scratch_shapes=[pltpu.VMEM((2, 2, tk, tq), jnp.float32),
                        pltpu.SMEM((1,), jnp.float32)],
        compiler_params=_cparams(2),
        name="diff_attn",
    )(lam, qt, k, vt, subln_g.reshape(1, DIFF_HEAD))


def _pad_cols(w, to):
    return jnp.pad(w, ((0, 0), (0, to - w.shape[1])))


def _layer(h, l, lam_init, x_norm, prm):
    bf = jnp.bfloat16
    f32 = jnp.float32
    f1 = swiglu_ffn(x_norm, prm["ffn1_w_gate"][l], prm["ffn1_w_up"][l],
                    prm["ffn1_w_down"][l].astype(bf), 1024, 256, 512, 512)
    h, xn = resid_norm(h, f1, prm["ffn1_post_g"][l], prm["mix_pre_g"][l], FFN_RESIDUAL, bf)

    w_in = prm["w_in"][l]
    c_rkv = 3 * D_RWKV
    c_w = c_rkv + DECAY_RANK
    c_a = c_w + ICLR_RANK
    c_g = c_a + GATE_RANK
    w_small = jnp.concatenate([_pad_cols(w_in[:, c_rkv:c_w], RANK_PAD),
                               _pad_cols(w_in[:, c_w:c_a], RANK_PAD),
                               w_in[:, c_a:c_g]], axis=1)
    w_rwkv = jnp.concatenate([w_in[:, :c_rkv], w_small], axis=1).astype(bf)
    proj = matmul(xn, w_rwkv, f32, 1024, 512)
    q_scale = DIFF_D ** -0.5 * math.log2(math.e)
    col_scale = jnp.concatenate([jnp.full((1, D_DIFF), q_scale, f32), jnp.ones((1, 2 * D_DIFF), f32)], axis=1)
    qkv = matmul_col_scaled(xn, w_in[:, c_g:].astype(bf), col_scale, bf, 1024, 512)

    def small_vec(a):
        return jnp.concatenate([_pad_cols(a[None, c_rkv:c_w], RANK_PAD),
                                _pad_cols(a[None, c_w:c_a], RANK_PAD),
                                a[None, c_a:c_g]], axis=1)

    def pad_rows(w):
        return jnp.pad(w, ((0, RANK_PAD - w.shape[0]), (0, 0))).astype(bf)

    mu_p, mu_n = prm["mu_prev"][l], prm["mu_next"][l]
    vecs = [prm[n][l].reshape(1, D_RWKV) for n in ("w0_f", "w0_b", "a0_f", "a0_b", "k_k", "k_a", "r_k")]
    mats = [pad_rows(prm["w2_f"][l]), pad_rows(prm["w2_b"][l]),
            pad_rows(prm["a2_f"][l]), pad_rows(prm["a2_b"][l]), prm["g2"][l].astype(bf)]
    prep = rwkv_prep(proj, [mu_p[None, :c_rkv], mu_n[None, :c_rkv], small_vec(mu_p), small_vec(mu_n)]
                     + vecs + mats)
    fwd, bwd, (v_r, gate, bonus) = prep[0:5], prep[5:10], prep[10:13]
    yf, yb = rwkv_scan(fwd, bwd, v_r)
    y_a = rwkv_post(yf, yb, bonus, gate, prm["gn_w"][l], prm["gn_b"][l])

    lam = (jnp.exp(jnp.sum(prm["lq1"][l] * prm["lk1"][l]))
           - jnp.exp(jnp.sum(prm["lq2"][l] * prm["lk2"][l])) + lam_init).reshape(1)
    t = qkv.shape[0]
    vt = qkv[:, 2 * D_DIFF:].T.reshape(N_DIFF_HEADS, DIFF_HEAD, t)
    vt = jnp.concatenate([vt, jnp.ones((N_DIFF_HEADS, ONES_ROWS, t), bf)], axis=1).reshape(N_DIFF_HEADS * V_ROWS, t)
    y_b = diff_attention(qkv[:, :D_DIFF].T, qkv[:, D_DIFF:2 * D_DIFF], vt, lam, prm["subln_g"][l], lam_init)

    w_out = prm["w_out"][l]
    mix = matmul_two(y_a, y_b, w_out[:D_RWKV].astype(bf), w_out[D_RWKV:].astype(bf), f32, 1024, 512)
    h, xn = resid_norm(h, mix, prm["mix_post_g"][l], prm["ffn2_pre_g"][l], 1.0, bf)

    f2 = swiglu_ffn(xn, prm["ffn2_w_gate"][l], prm["ffn2_w_up"][l],
                    prm["ffn2_w_down"][l].astype(bf), 1024, 256, 512, 512)
    return resid_norm_out(h, f2, prm["ffn2_post_g"][l], prm["final_g"][l], FFN_RESIDUAL)


def kernel(x, ffn1_pre_g, ffn1_w_gate, ffn1_w_up, ffn1_w_down, ffn1_post_g, mix_pre_g, w_in, mu_prev, mu_next, w0_f, w2_f, w0_b, w2_b, a0_f, a2_f, a0_b, a2_b, g2, k_k, k_a, r_k, gn_w, gn_b, lq1, lk1, lq2, lk2, subln_g, w_out, mix_post_g, ffn2_pre_g, ffn2_w_gate, ffn2_w_up, ffn2_w_down, ffn2_post_g, final_g):
    prm = dict(locals())
    bsz, t, d = x.shape
    assert bsz == 1
    depth = ffn1_pre_g.shape[0]
    h = x.reshape(t, d)
    for l in range(depth):
        lam_init = 0.8 - 0.6 * math.exp(-0.3 * l)
        x_norm = rms_norm_cast(h, ffn1_pre_g[l], jnp.bfloat16)
        h = _layer(h, l, lam_init, x_norm, prm)
    return h.reshape(bsz, t, d)
```

```python
import functools
import math

import jax
import jax.numpy as jnp
import numpy as np
from jax import lax
from jax.experimental import pallas as pl
from jax.experimental.pallas import tpu as pltpu

D_MODEL = 4096
D_RWKV = 2048
D_DIFF = 2048
RWKV_HEAD = 64
DECAY_RANK = 96
ICLR_RANK = 96
GATE_RANK = 256
DIFF_D = 64
DIFF_HEAD = 128
N_DIFF_HEADS = 16
ALIBI_MAX_EXP = 8.0
NORM_EPS = 1e-6
RWKV_GN_EPS = 64e-5
SUBLN_EPS = 1e-5
FFN_RESIDUAL = 0.5

LANES = 128
SUBLANES = 8
VMEM_LIMIT = 56 * 1024 * 1024

RANK_PAD = 128
SMALL_COLS = 2 * RANK_PAD + GATE_RANK
CHUNK = 64
PAIR = 2 * RWKV_HEAD


def _cparams(n_axes):
    return pltpu.CompilerParams(
        dimension_semantics=("arbitrary",) * n_axes, vmem_limit_bytes=VMEM_LIMIT)


def _dot(a, b, dims=(((1,), (0,)), ((), ()))):
    return lax.dot_general(a, b, dims, preferred_element_type=jnp.float32)


_NT = (((1,), (1,)), ((), ()))
_TN = (((0,), (0,)), ((), ()))


def _rms(x, g, eps):
    return x * lax.rsqrt(jnp.mean(x * x, axis=-1, keepdims=True) + eps) * g


def _norm_kernel(x_ref, g_ref, o_ref):
    o_ref[...] = _rms(x_ref[...], g_ref[...], NORM_EPS).astype(o_ref.dtype)


def rms_norm_cast(x, g, out_dtype, tm=256):
    m, d = x.shape
    return pl.pallas_call(
        _norm_kernel,
        grid=(m // tm,),
        in_specs=[pl.BlockSpec((tm, d), lambda i: (i, 0)),
                  pl.BlockSpec((1, d), lambda i: (0, 0))],
        out_specs=pl.BlockSpec((tm, d), lambda i: (i, 0)),
        out_shape=jax.ShapeDtypeStruct((m, d), out_dtype),
        compiler_params=_cparams(1),
        name="rms_norm",
    )(x, g.reshape(1, d))


def _resid_norm_kernel(h_ref, f_ref, gp_ref, gn_ref, h_out_ref, n_out_ref, *, scale):
    h = h_ref[...] + scale * _rms(f_ref[...], gp_ref[...], NORM_EPS)
    h_out_ref[...] = h
    n_out_ref[...] = _rms(h, gn_ref[...], NORM_EPS).astype(n_out_ref.dtype)


def resid_norm(h, f, g_post, g_next, scale, out_dtype, tm=256):
    m, d = h.shape
    row = pl.BlockSpec((tm, d), lambda i: (i, 0))
    vec = pl.BlockSpec((1, d), lambda i: (0, 0))
    return pl.pallas_call(
        functools.partial(_resid_norm_kernel, scale=scale),
        grid=(m // tm,),
        in_specs=[row, row, vec, vec],
        out_specs=[row, row],
        out_shape=[jax.ShapeDtypeStruct((m, d), jnp.float32),
                   jax.ShapeDtypeStruct((m, d), out_dtype)],
        compiler_params=_cparams(1),
        name="resid_norm",
    )(h, f, g_post.reshape(1, d), g_next.reshape(1, d))


def _resid_norm_out_kernel(h_ref, f_ref, gp_ref, gn_ref, n_out_ref, *, scale):
    h = h_ref[...] + scale * _rms(f_ref[...], gp_ref[...], NORM_EPS)
    n_out_ref[...] = _rms(h, gn_ref[...], NORM_EPS).astype(n_out_ref.dtype)


def resid_norm_out(h, f, g_post, g_next, scale, tm=256):
    m, d = h.shape
    row = pl.BlockSpec((tm, d), lambda i: (i, 0))
    vec = pl.BlockSpec((1, d), lambda i: (0, 0))
    return pl.pallas_call(
        functools.partial(_resid_norm_out_kernel, scale=scale),
        grid=(m // tm,),
        in_specs=[row, row, vec, vec],
        out_specs=row,
        out_shape=jax.ShapeDtypeStruct((m, d), jnp.float32),
        compiler_params=_cparams(1),
        name="resid_norm_out",
    )(h, f, g_post.reshape(1, d), g_next.reshape(1, d))


def _mm_kernel(x_ref, w_ref, o_ref):
    o_ref[...] = _dot(x_ref[...], w_ref[...]).astype(o_ref.dtype)


def matmul(x, w, out_dtype, tm, tn):
    m, k = x.shape
    _, n = w.shape
    return pl.pallas_call(
        _mm_kernel,
        grid=(n // tn, m // tm),
        in_specs=[pl.BlockSpec((tm, k), lambda j, i: (i, 0)),
                  pl.BlockSpec((k, tn), lambda j, i: (0, j))],
        out_specs=pl.BlockSpec((tm, tn), lambda j, i: (i, j)),
        out_shape=jax.ShapeDtypeStruct((m, n), out_dtype),
        compiler_params=_cparams(2),
        name="matmul",
    )(x, w)


def _mm_two_kernel(xa_ref, xb_ref, wa_ref, wb_ref, o_ref):
    o_ref[...] = (_dot(xa_ref[...], wa_ref[...]) + _dot(xb_ref[...], wb_ref[...])).astype(o_ref.dtype)


def matmul_two(xa, xb, wa, wb, out_dtype, tm, tn):
    m, ka = xa.shape
    _, kb = xb.shape
    _, n = wa.shape
    return pl.pallas_call(
        _mm_two_kernel,
        grid=(n // tn, m // tm),
        in_specs=[pl.BlockSpec((tm, ka), lambda j, i: (i, 0)),
                  pl.BlockSpec((tm, kb), lambda j, i: (i, 0)),
                  pl.BlockSpec((ka, tn), lambda j, i: (0, j)),
                  pl.BlockSpec((kb, tn), lambda j, i: (0, j))],
        out_specs=pl.BlockSpec((tm, tn), lambda j, i: (i, j)),
        out_shape=jax.ShapeDtypeStruct((m, n), out_dtype),
        compiler_params=_cparams(2),
        name="matmul_two",
    )(xa, xb, wa, wb)


def _mm_scaled_kernel(x_ref, w_ref, cs_ref, o_ref):
    o_ref[...] = (_dot(x_ref[...], w_ref[...]) * cs_ref[...]).astype(o_ref.dtype)


def matmul_col_scaled(x, w, col_scale, out_dtype, tm, tn):
    m, k = x.shape
    _, n = w.shape
    return pl.pallas_call(
        _mm_scaled_kernel,
        grid=(n // tn, m // tm),
        in_specs=[pl.BlockSpec((tm, k), lambda j, i: (i, 0)),
                  pl.BlockSpec((k, tn), lambda j, i: (0, j)),
                  pl.BlockSpec((1, tn), lambda j, i: (0, j))],
        out_specs=pl.BlockSpec((tm, tn), lambda j, i: (i, j)),
        out_shape=jax.ShapeDtypeStruct((m, n), out_dtype),
        compiler_params=_cparams(2),
        name="matmul_col_scaled",
    )(x, w, col_scale)


def _gate_up_kernel(x_ref, wg_ref, wu_ref, o_ref, wg_bf_ref, wu_bf_ref):
    @pl.when(pl.program_id(1) == 0)
    def _():
        wg_bf_ref[...] = wg_ref[...].astype(jnp.bfloat16)
        wu_bf_ref[...] = wu_ref[...].astype(jnp.bfloat16)

    x = x_ref[...]
    g = _dot(x, wg_bf_ref[...])
    u = _dot(x, wu_bf_ref[...])
    o_ref[...] = (g * jax.nn.sigmoid(g) * u).astype(o_ref.dtype)


def gate_up(x, wg, wu, tm, tn):
    m, k = x.shape
    _, n = wg.shape
    wspec = pl.BlockSpec((k, tn), lambda j, i: (0, j))
    return pl.pallas_call(
        _gate_up_kernel,
        grid=(n // tn, m // tm),
        in_specs=[pl.BlockSpec((tm, k), lambda j, i: (i, 0)), wspec, wspec],
        out_specs=pl.BlockSpec((tm, tn), lambda j, i: (i, j)),
        out_shape=jax.ShapeDtypeStruct((m, n), jnp.bfloat16),
        scratch_shapes=[pltpu.VMEM((k, tn), jnp.bfloat16)] * 2,
        compiler_params=_cparams(2),
        name="gate_up",
    )(x, wg, wu)


def swiglu_ffn(xn, wg, wu, wd, tm_gu, tn_gu, tm_d, tn_d):
    a = gate_up(xn, wg, wu, tm_gu, tn_gu)
    return matmul(a, wd, jnp.float32, tm_d, tn_d)


def _bf16_split(x, n):
    pieces = []
    for _ in range(n - 1):
        p = x.astype(jnp.bfloat16)
        pieces.append(p)
        x = x - p.astype(jnp.float32)
    pieces.append(x.astype(jnp.bfloat16))
    return pieces


HEAD_SUM_PIECES = 2
CUMSUM_PIECES = 3


def _head_sum(x, w_bd):
    parts = []
    for s in range(x.shape[1] // LANES):
        pieces = _bf16_split(x[:, s * LANES:(s + 1) * LANES], HEAD_SUM_PIECES)
        parts.append(sum(_dot(p, w_bd) for p in pieces))
    return jnp.concatenate(parts, axis=1)


def _token_shift(p_ref, pp_ref, pn_ref, mup_ref, mun_ref, first, last):
    p = p_ref[...]
    rows = p.shape[0]
    rid = lax.broadcasted_iota(jnp.int32, p.shape, 0)
    prev_row = jnp.where(first, 0.0, pp_ref[SUBLANES - 1:SUBLANES, :])
    next_row = jnp.where(last, 0.0, pn_ref[0:1, :])
    p_prev = jnp.where(rid == 0, prev_row, pltpu.roll(p, 1, 0))
    p_next = jnp.where(rid == rows - 1, next_row, pltpu.roll(p, rows - 1, 0))
    return p + mup_ref[...] * (p_prev - p) + mun_ref[...] * (p_next - p)


def _rwkv_prep_kernel(
        p_ref, pp_ref, pn_ref, s_ref, sp_ref, sn_ref,
        mup_ref, mun_ref, mups_ref, muns_ref,
        w0f_ref, w0b_ref, a0f_ref, a0b_ref, kk_ref, ka_ref, rk_ref,
        w2f_ref, w2b_ref, a2f_ref, a2b_ref, g2_ref,
        atf_ref, btf_ref, ktf_ref, rtf_ref, glf_ref,
        atb_ref, btb_ref, ktb_ref, rtb_ref, glb_ref,
        v_ref, gate_ref, bonus_ref):
    i = pl.program_id(0)
    first = i == 0
    last = i == pl.num_programs(0) - 1
    tm = p_ref.shape[0]
    n_chunks = tm // CHUNK

    p = _token_shift(p_ref, pp_ref, pn_ref, mup_ref, mun_ref, first, last)
    s = _token_shift(s_ref, sp_ref, sn_ref, mups_ref, muns_ref, first, last)
    r = p[:, 0:D_RWKV]
    k = p[:, D_RWKV:2 * D_RWKV]
    v = p[:, 2 * D_RWKV:3 * D_RWKV]
    hw = jnp.tanh(s[:, 0:RANK_PAD]).astype(jnp.bfloat16)
    xa = s[:, RANK_PAD:2 * RANK_PAD].astype(jnp.bfloat16)
    sg = jax.nn.sigmoid(s[:, 2 * RANK_PAD:]).astype(jnp.bfloat16)

    li = lax.broadcasted_iota(jnp.int32, (LANES, LANES), 0) // RWKV_HEAD
    lj = lax.broadcasted_iota(jnp.int32, (LANES, LANES), 1) // RWKV_HEAD
    ones_bd = (li == lj).astype(jnp.bfloat16)

    kk = k * kk_ref[...]
    kk = kk / jnp.maximum(jnp.sqrt(_head_sum(kk * kk, ones_bd)), 1e-12)

    ti = lax.broadcasted_iota(jnp.int32, (tm, tm), 0)
    tj = lax.broadcasted_iota(jnp.int32, (tm, tm), 1)
    same_chunk = (ti // CHUNK) == (tj // CHUNK)

    k_sum = jnp.zeros_like(k)
    dirs = ((w0f_ref, w2f_ref, a0f_ref, a2f_ref, atf_ref, btf_ref, ktf_ref, rtf_ref, glf_ref, False),
            (w0b_ref, w2b_ref, a0b_ref, a2b_ref, atb_ref, btb_ref, ktb_ref, rtb_ref, glb_ref, True))
    for w0_ref, w2_ref, a0_ref, a2_ref, at_ref, bt_ref, kt_ref, rt_ref, gl_ref, rev in dirs:
        z = w0_ref[...] + _dot(hw, w2_ref[...])
        lw = -math.exp(-0.5) * jax.nn.sigmoid(z)
        a = jax.nn.sigmoid(a0_ref[...] + _dot(xa, a2_ref[...]))
        k_dir = k * (1.0 + (a - 1.0) * ka_ref[...])
        k_sum = k_sum + k_dir
        order = (tj >= ti) if rev else (tj <= ti)
        tri = jnp.where(same_chunk & order, 1.0, 0.0).astype(jnp.bfloat16)
        c = sum(_dot(tri, piece) for piece in _bf16_split(lw, CUMSUM_PIECES))
        e_neg = jnp.exp(-c)
        at_ref[...] = (-kk * jnp.exp(c - lw)).astype(at_ref.dtype)
        bt_ref[...] = (kk * a * e_neg).astype(bt_ref.dtype)
        kt_ref[...] = (k_dir * e_neg).astype(kt_ref.dtype)
        e_pos = jnp.exp(c)
        rt_ref[...] = (r * e_pos).astype(rt_ref.dtype)
        for j in range(n_chunks):
            end = j * CHUNK if rev else (j + 1) * CHUNK - 1
            gl_ref[j] = e_pos[end:end + 1, :]

    v_ref[...] = v.astype(v_ref.dtype)
    gate_ref[...] = _dot(sg, g2_ref[...]).astype(gate_ref.dtype)
    bonus_ref[...] = (_head_sum(r * k_sum * rk_ref[...], ones_bd) * v).astype(bonus_ref.dtype)


def rwkv_prep(proj, params, tm=128):
    t = proj.shape[0]
    nb = t // tm
    hb = tm // SUBLANES
    n_halo = t // SUBLANES
    wide = 3 * D_RWKV
    small_blk = wide // SMALL_COLS

    def row(c, cb=0):
        return pl.BlockSpec((tm, c), lambda i: (i, cb))

    def halo_prev(c, cb=0):
        return pl.BlockSpec((SUBLANES, c), lambda i: (jnp.maximum(i * hb - 1, 0), cb))

    def halo_next(c, cb=0):
        return pl.BlockSpec((SUBLANES, c), lambda i: (jnp.minimum((i + 1) * hb, n_halo - 1), cb))

    def vec(c):
        return pl.BlockSpec((1, c), lambda i: (0, 0))

    def mat(r_, c):
        return pl.BlockSpec((r_, c), lambda i: (0, 0))

    gl_spec = pl.BlockSpec((tm // CHUNK, 1, D_RWKV), lambda i: (i, 0, 0))
    big = jax.ShapeDtypeStruct((t, D_RWKV), jnp.float32)
    gl = jax.ShapeDtypeStruct((t // CHUNK, 1, D_RWKV), jnp.float32)
    in_specs = [row(wide), halo_prev(wide), halo_next(wide),
                row(SMALL_COLS, small_blk), halo_prev(SMALL_COLS, small_blk), halo_next(SMALL_COLS, small_blk),
                vec(wide), vec(wide), vec(SMALL_COLS), vec(SMALL_COLS)]
    in_specs += [vec(D_RWKV)] * 7
    in_specs += [mat(RANK_PAD, D_RWKV)] * 4 + [mat(GATE_RANK, D_RWKV)]
    dir_specs = [row(D_RWKV)] * 4 + [gl_spec]
    big_bf = jax.ShapeDtypeStruct((t, D_RWKV), jnp.bfloat16)
    dir_shapes = [big_bf] * 4 + [gl]
    return pl.pallas_call(
        _rwkv_prep_kernel,
        grid=(nb,),
        in_specs=in_specs,
        out_specs=dir_specs + dir_specs + [row(D_RWKV)] * 3,
        out_shape=dir_shapes + dir_shapes + [big_bf, big_bf, big_bf],
        compiler_params=_cparams(1),
        name="rwkv_prep",
    )(proj, proj, proj, proj, proj, proj, *params)


def _stack(x, lane_head):
    zero = jnp.zeros_like(x)
    return jnp.concatenate([jnp.where(lane_head == 0, x, zero),
                            jnp.where(lane_head == 1, x, zero)], axis=0)


def _scan_masks(rev):
    n2 = 2 * CHUNK
    ri = lax.broadcasted_iota(jnp.int32, (n2, n2), 0)
    ci = lax.broadcasted_iota(jnp.int32, (n2, n2), 1)
    same = (ri // CHUNK) == (ci // CHUNK)
    before = (ci > ri) if rev else (ci < ri)
    strict = same & before
    incl = same & (before | (ri == ci))
    eye = jnp.where(ri == ci, 1.0, 0.0)
    return strict, jnp.concatenate([incl, incl], axis=1), eye


def _scan_prepare(units):
    bf = jnp.bfloat16
    n2 = 2 * CHUNK
    lane_head = lax.broadcasted_iota(jnp.int32, (CHUNK, PAIR), 1) // RWKV_HEAD
    stacked = [[_stack(x, lane_head) for x in u[:5]] for u in units]
    bks = [jnp.concatenate([s[1], s[2]], axis=0) for s in stacked]
    g_as = [_dot(s[0], bk, _NT) for s, bk in zip(stacked, bks)]
    g_rs = [_dot(s[3], bk, _NT) for s, bk in zip(stacked, bks)]
    xs = [jnp.where(u[6][0], g[:, :n2], 0.0) for u, g in zip(units, g_as)]
    a_aks = [jnp.where(u[6][0], g[:, n2:], 0.0).astype(bf) for u, g in zip(units, g_as)]
    a_rs = [jnp.where(u[6][1], g, 0.0).astype(bf) for u, g in zip(units, g_rs)]
    akv = [_dot(a, s[4]).astype(bf) for a, s in zip(a_aks, stacked)]

    tinvs = [u[6][2] + x for u, x in zip(units, xs)]
    for _ in range(int(math.log2(CHUNK)) - 1):
        xbs = [x.astype(bf) for x in xs]
        xs = [_dot(xb, xb) for xb in xbs]
        tinvs = [t + _dot(t.astype(bf), x.astype(bf)) for t, x in zip(tinvs, xs)]

    wqs = [_dot(t.astype(bf), jnp.concatenate([s[0], kv], axis=1))
           for t, s, kv in zip(tinvs, stacked, akv)]
    out = []
    for u, s, bk, wq, a_r in zip(units, stacked, bks, wqs, a_rs):
        gl = u[5]
        w = wq[:, :PAIR].astype(bf)
        q = wq[:, PAIR:]
        bkg = (bk * gl).astype(bf)
        m = _dot(w, bkg[:n2], _TN)
        n = _dot(jnp.concatenate([q.astype(bf), s[4]], axis=0), bkg, _TN)
        out.append((w, q, s[3], a_r, s[4], m.astype(bf), n, gl))
    return out


def _rwkv_scan_kernel(atf_ref, btf_ref, ktf_ref, rtf_ref, glf_ref, vf_ref,
                      atb_ref, btb_ref, ktb_ref, rtb_ref, glb_ref, vb_ref,
                      yf_ref, yb_ref, sf_ref, sb_ref, *, n_blk, n_par):
    @pl.when(pl.program_id(1) == 0)
    def _():
        sf_ref[...] = jnp.zeros_like(sf_ref)
        sb_ref[...] = jnp.zeros_like(sb_ref)

    def tile(j, p):
        return pl.ds(j * CHUNK, CHUNK), pl.ds(p * PAIR, PAIR)

    masks_f = _scan_masks(False)
    masks_b = _scan_masks(True)
    units = []
    for p in range(n_par):
        for j in range(n_blk):
            units.append(tuple(ref[tile(j, p)] for ref in (atf_ref, btf_ref, ktf_ref, rtf_ref, vf_ref))
                         + (glf_ref[j, :, pl.ds(p * PAIR, PAIR)], masks_f))
            units.append(tuple(ref[tile(j, p)] for ref in (atb_ref, btb_ref, ktb_ref, rtb_ref, vb_ref))
                         + (glb_ref[j, :, pl.ds(p * PAIR, PAIR)], masks_b))
    prep = _scan_prepare(units)

    sf = [sf_ref[p] for p in range(n_par)]
    sb = [sb_ref[p] for p in range(n_par)]
    pending = []
    for j in range(n_blk):
        jb = n_blk - 1 - j
        for p in range(n_par):
            pf = prep[2 * (p * n_blk + j)]
            pb = prep[2 * (p * n_blk + jb) + 1]
            sfb = sf[p].astype(jnp.bfloat16)
            sbb = sb[p].astype(jnp.bfloat16)
            sf[p] = sf[p] * pf[7] + _dot(sfb, pf[5]) + pf[6]
            sb[p] = sb[p] * pb[7] + _dot(sbb, pb[5]) + pb[6]
            pending.append((pf, sfb, yf_ref, tile(j, p)))
            pending.append((pb, sbb, yb_ref, tile(jb, p)))
    for p in range(n_par):
        sf_ref[p] = sf[p]
        sb_ref[p] = sb[p]
    us = [(_dot(u[0], s0b, _NT) + u[1]).astype(jnp.bfloat16) for u, s0b, _, _ in pending]
    for (u, s0b, y_ref, where), uu in zip(pending, us):
        y2 = _dot(u[2], s0b, _NT) + _dot(u[3], jnp.concatenate([uu, u[4]], axis=0))
        y_ref[where] = (y2[:CHUNK] + y2[CHUNK:]).astype(y_ref.dtype)


def rwkv_scan(fwd, bwd, v, n_blk=4, n_par=2):
    t = v.shape[0]
    nb = t // (CHUNK * n_blk)
    n_grp = D_RWKV // (PAIR * n_par)
    width = PAIR * n_par
    f_blk = pl.BlockSpec((CHUNK * n_blk, width), lambda p, c: (c, p))
    b_blk = pl.BlockSpec((CHUNK * n_blk, width), lambda p, c: (nb - 1 - c, p))
    f_gl = pl.BlockSpec((n_blk, 1, width), lambda p, c: (c, 0, p))
    b_gl = pl.BlockSpec((n_blk, 1, width), lambda p, c: (nb - 1 - c, 0, p))
    out = jax.ShapeDtypeStruct((t, D_RWKV), jnp.bfloat16)
    return pl.pallas_call(
        functools.partial(_rwkv_scan_kernel, n_blk=n_blk, n_par=n_par),
        grid=(n_grp, nb),
        in_specs=[f_blk] * 4 + [f_gl, f_blk] + [b_blk] * 4 + [b_gl, b_blk],
        out_specs=[f_blk, b_blk],
        out_shape=[out, out],
        scratch_shapes=[pltpu.VMEM((n_par, PAIR, PAIR), jnp.float32)] * 2,
        compiler_params=_cparams(2),
        name="rwkv_scan",
    )(*fwd, v, *bwd, v)


def _rwkv_post_kernel(yf_ref, yb_ref, bonus_ref, gate_ref, gw_ref, gb_ref, o_ref):
    li = lax.broadcasted_iota(jnp.int32, (LANES, LANES), 0) // RWKV_HEAD
    lj = lax.broadcasted_iota(jnp.int32, (LANES, LANES), 1) // RWKV_HEAD
    mean_bd = jnp.where(li == lj, 1.0 / RWKV_HEAD, 0.0).astype(jnp.bfloat16)
    y = yf_ref[...].astype(jnp.float32) + yb_ref[...].astype(jnp.float32)
    d = y - _head_sum(y, mean_bd)
    var = _head_sum(d * d, mean_bd)
    yn = d * lax.rsqrt(var + RWKV_GN_EPS) * gw_ref[...] + gb_ref[...]
    o_ref[...] = ((yn + bonus_ref[...].astype(jnp.float32))
                  * gate_ref[...].astype(jnp.float32)).astype(o_ref.dtype)


def rwkv_post(yf, yb, bonus, gate, gn_w, gn_b, tm=256):
    t = yf.shape[0]
    row = pl.BlockSpec((tm, D_RWKV), lambda i: (i, 0))
    vec = pl.BlockSpec((1, D_RWKV), lambda i: (0, 0))
    return pl.pallas_call(
        _rwkv_post_kernel,
        grid=(t // tm,),
        in_specs=[row, row, row, row, vec, vec],
        out_specs=row,
        out_shape=jax.ShapeDtypeStruct((t, D_RWKV), jnp.bfloat16),
        compiler_params=_cparams(1),
        name="rwkv_post",
    )(yf, yb, bonus, gate, gn_w.reshape(1, -1), gn_b.reshape(1, -1))


SLOPE_PIECES = 3
SC_PER_HEAD = 2 + SLOPE_PIECES
ONES_ROWS = 16
V_ROWS = DIFF_HEAD + ONES_ROWS
UNDERFLOW_LOG2 = 160.0
BOUND_SLACK = 1.01


def _diff_attn_kernel(sc_ref, qt_ref, k_ref, vt_ref, g_ref, o_ref, s_ref, kmax_ref, *, tq, tk, lam_init):
    bf = jnp.bfloat16
    h = pl.program_id(0)
    qi = pl.program_id(1)
    n_kv = k_ref.shape[0] // tk
    lam = sc_ref[0]
    base = 1 + h * SC_PER_HEAD
    slope = sc_ref[base]
    pieces = [sc_ref[base + 1 + i] for i in range(SLOPE_PIECES)]
    inv_slope = sc_ref[base + 1 + SLOPE_PIECES]

    @pl.when(qi == 0)
    def _():
        li = lax.broadcasted_iota(jnp.int32, (LANES, LANES), 0) // DIFF_D
        lj = lax.broadcasted_iota(jnp.int32, (LANES, LANES), 1) // DIFF_D
        half_ones = (li == lj).astype(bf)

        def tile_max(j, best):
            kf = k_ref[pl.ds(pl.multiple_of(j * tk, tk), tk), :].astype(jnp.float32)
            sq = sum(_dot(p, half_ones) for p in _bf16_split(kf * kf, HEAD_SUM_PIECES))
            return jnp.maximum(best, jnp.max(sq, axis=0, keepdims=True))

        best = lax.fori_loop(0, n_kv, tile_max, jnp.zeros((1, LANES), jnp.float32))
        kmax_ref[0] = jnp.max(best)

    qt = qt_ref[...]
    q_sq = jnp.square(qt.astype(jnp.float32))
    q_n2 = jnp.maximum(jnp.sum(q_sq[:DIFF_D], axis=0, keepdims=True),
                       jnp.sum(q_sq[DIFF_D:], axis=0, keepdims=True))
    bound = jnp.max(jnp.sqrt(q_n2 * kmax_ref[0]))
    reach = (2.0 * BOUND_SLACK * bound + UNDERFLOW_LOG2) * inv_slope
    width = jnp.int32(1)
    for d in range(1, n_kv):
        width = width + jnp.where(reach < float(d * tk), 0, 1).astype(jnp.int32)
    qrow = lax.broadcasted_iota(jnp.int32, (DIFF_HEAD, tq), 0)
    feat = jnp.zeros((DIFF_HEAD, tq), jnp.float32)
    for i, pc in enumerate(pieces):
        feat = jnp.where(qrow % DIFF_D == i, 2.0 * pc, feat)
        feat = jnp.where(qrow % DIFF_D == SLOPE_PIECES + i, pc, feat)
    feat = feat.astype(bf)
    own_rows = [qrow < DIFF_D, qrow >= DIFF_D]
    qts = [jnp.where(own, qt, feat) for own in own_rows]

    kl = lax.broadcasted_iota(jnp.int32, (tk, LANES), 1)
    kr = lax.broadcasted_iota(jnp.int32, (tk, LANES), 0)
    kfeat = jnp.where(kl % DIFF_D < SLOPE_PIECES, kr >> 1,
                      jnp.where(kl % DIFF_D < 2 * SLOPE_PIECES, kr & 1, 0)).astype(jnp.float32)
    kfeat_before = kfeat.astype(bf)
    kfeat_after = (-kfeat).astype(bf)
    kzero = jnp.zeros((tk, LANES), bf)
    own_lanes = [kl < DIFF_D, kl >= DIFF_D]

    qpos = qi * tq + lax.broadcasted_iota(jnp.int32, (1, tq), 1)
    q_bias = slope * qpos.astype(jnp.float32)

    def online(carry, s, t_q, vt):
        m, acc = carry
        m_new = jnp.maximum(m, jnp.max(s, axis=0, keepdims=True) + t_q)
        alpha = jnp.exp2(m - m_new)
        p = jnp.exp2(s - (m_new - t_q))
        return m_new, alpha * acc + _dot(vt, p.astype(bf))

    jd = (qi * tq) // tk
    lo = jnp.maximum(jd - width, 0)
    hi = jnp.minimum(jd + width, n_kv - 1)
    n_off = hi - lo

    def tile_of(n):
        j = lo + n
        return j + (j >= jd).astype(jnp.int32)

    def scores(n, slot, mp):
        j = tile_of(n)
        start = pl.multiple_of(j * tk, tk)
        kb = k_ref[pl.ds(start, tk), :]
        kf = jnp.where(j < jd, kfeat_before, kfeat_after)
        s_ref[slot, mp] = _dot(jnp.where(own_lanes[mp], kb, kf), qts[mp])

    def consume(n, slot, mp, carry):
        j = tile_of(n)
        start = pl.multiple_of(j * tk, tk)
        vt = vt_ref[:, pl.ds(start, tk)]
        sign = jnp.where(j < jd, 1.0, -1.0)
        t_q = sign * (slope * (j * tk).astype(jnp.float32) - q_bias)
        return online(carry, s_ref[slot, mp], t_q, vt)

    def step(n_next, n_cur, slot_next, slot_cur, carries):
        out = []
        for mp in range(2):
            scores(n_next, slot_next, mp)
            out.append(consume(n_cur, slot_cur, mp, carries[mp]))
        return tuple(out)

    start_d = pl.multiple_of(jd * tk, tk)
    kb_d = k_ref[pl.ds(start_d, tk), :]
    vt_d = vt_ref[:, pl.ds(start_d, tk)]
    kpos = jd * tk + lax.broadcasted_iota(jnp.int32, (tk, tq), 0)
    bias = slope * jnp.abs(kpos - qpos).astype(jnp.float32)
    init = (jnp.full((1, tq), -jnp.inf, jnp.float32), jnp.zeros((V_ROWS, tq), jnp.float32))
    for mp, (own, q_m) in enumerate(zip(own_lanes, qts)):
        s_ref[1, mp] = _dot(jnp.where(own, kb_d, kzero), q_m) - bias
    carries = []
    for mp in range(2):
        scores(jnp.int32(0), 0, mp)
        carries.append(online(init, s_ref[1, mp], 0.0, vt_d))

    def body(i, carries):
        n = 2 * i
        carries = step(n + 1, n, 1, 0, carries)
        return step(n + 2, n + 1, 0, 1, carries)

    n_pairs = (n_off - 1) // 2
    carries = lax.fori_loop(0, n_pairs, body, tuple(carries))
    last = n_off - 1

    def tail_two(carries):
        carries = step(last, last - 1, 1, 0, carries)
        return tuple(consume(last, 1, mp, carries[mp]) for mp in range(2))

    def tail_one(carries):
        return tuple(consume(last, 0, mp, carries[mp]) for mp in range(2))

    (_, acc0), (_, acc1) = lax.cond(last == 2 * n_pairs + 1, tail_two, tail_one, carries)
    o0 = acc0[:DIFF_HEAD] / acc0[DIFF_HEAD:DIFF_HEAD + 1]
    o1 = acc1[:DIFF_HEAD] / acc1[DIFF_HEAD:DIFF_HEAD + 1]
    o = (o0 - lam * o1).T
    o_ref[...] = (_rms(o, g_ref[...], SUBLN_EPS) * (1.0 - lam_init)).astype(o_ref.dtype)


def _bf16_pieces(x, n):
    out = []
    for _ in range(n):
        p = (x.view(np.uint32) & np.uint32(0xFFFF0000)).view(np.float32)
        out.append(p)
        x = (x - p).astype(np.float32)
    return out


def diff_attention(qt, k, vt, lam, subln_g, lam_init, tq=512, tk=512):
    t = k.shape[0]
    assert tk % tq == 0 and tk // 2 <= 256
    assert t // tk >= 2
    slopes = 2.0 ** (-ALIBI_MAX_EXP * np.arange(1, N_DIFF_HEADS + 1, dtype=np.float64) / N_DIFF_HEADS)
    slopes = (slopes * math.log2(math.e)).astype(np.float32)
    pieces = _bf16_pieces(slopes, SLOPE_PIECES)
    slope_used = sum(pieces)
    per_head = np.stack([slope_used] + pieces + [(1.0 / slope_used).astype(np.float32)], axis=1)
    lam = jnp.concatenate([lam, jnp.asarray(per_head.reshape(-1))])
    return pl.pallas_call(
        functools.partial(_diff_attn_kernel, tq=tq, tk=tk, lam_init=lam_init),
        grid=(N_DIFF_HEADS, t // tq),
        in_specs=[pl.BlockSpec(memory_space=pltpu.SMEM),
                  pl.BlockSpec((DIFF_HEAD, tq), lambda h, i: (h, i)),
                  pl.BlockSpec((t, DIFF_HEAD), lambda h, i: (0, h)),
                  pl.BlockSpec((V_ROWS, t), lambda h, i: (h, 0)),
                  pl.BlockSpec((1, DIFF_HEAD), lambda h, i: (0, 0))],
        out_specs=pl.BlockSpec((tq, DIFF_HEAD), lambda h, i: (i, h)),
        out_shape=jax.ShapeDtypeStruct((t, D_DIFF), jnp.bfloat16),
        scratch_shapes=[pltpu.VMEM((2, 2, tk, tq), jnp.float32),
                        pltpu.SMEM((1,), jnp.float32)],
        compiler_params=_cparams(2),
        name="diff_attn",
    )(lam, qt, k, vt, subln_g.reshape(1, DIFF_HEAD))


def _pad_cols(w, to):
    return jnp.pad(w, ((0, 0), (0, to - w.shape[1])))


def _layer(h, l, lam_init, x_norm, prm):
    bf = jnp.bfloat16
    f32 = jnp.float32
    f1 = swiglu_ffn(x_norm, prm["ffn1_w_gate"][l], prm["ffn1_w_up"][l],
                    prm["ffn1_w_down"][l].astype(bf), 1024, 256, 512, 512)
    h, xn = resid_norm(h, f1, prm["ffn1_post_g"][l], prm["mix_pre_g"][l], FFN_RESIDUAL, bf)

    w_in = prm["w_in"][l]
    c_rkv = 3 * D_RWKV
    c_w = c_rkv + DECAY_RANK
    c_a = c_w + ICLR_RANK
    c_g = c_a + GATE_RANK
    w_small = jnp.concatenate([_pad_cols(w_in[:, c_rkv:c_w], RANK_PAD),
                               _pad_cols(w_in[:, c_w:c_a], RANK_PAD),
                               w_in[:, c_a:c_g]], axis=1)
    w_rwkv = jnp.concatenate([w_in[:, :c_rkv], w_small], axis=1).astype(bf)
    proj = matmul(xn, w_rwkv, f32, 1024, 512)
    q_scale = DIFF_D ** -0.5 * math.log2(math.e)
    col_scale = jnp.concatenate([jnp.full((1, D_DIFF), q_scale, f32), jnp.ones((1, 2 * D_DIFF), f32)], axis=1)
    qkv = matmul_col_scaled(xn, w_in[:, c_g:].astype(bf), col_scale, bf, 1024, 512)

    def small_vec(a):
        return jnp.concatenate([_pad_cols(a[None, c_rkv:c_w], RANK_PAD),
                                _pad_cols(a[None, c_w:c_a], RANK_PAD),
                                a[None, c_a:c_g]], axis=1)

    def pad_rows(w):
        return jnp.pad(w, ((0, RANK_PAD - w.shape[0]), (0, 0))).astype(bf)

    mu_p, mu_n = prm["mu_prev"][l], prm["mu_next"][l]
    vecs = [prm[n][l].reshape(1, D_RWKV) for n in ("w0_f", "w0_b", "a0_f", "a0_b", "k_k", "k_a", "r_k")]
    mats = [pad_rows(prm["w2_f"][l]), pad_rows(prm["w2_b"][l]),
            pad_rows(prm["a2_f"][l]), pad_rows(prm["a2_b"][l]), prm["g2"][l].astype(bf)]
    prep = rwkv_prep(proj, [mu_p[None, :c_rkv], mu_n[None, :c_rkv], small_vec(mu_p), small_vec(mu_n)]
                     + vecs + mats)
    fwd, bwd, (v_r, gate, bonus) = prep[0:5], prep[5:10], prep[10:13]
    yf, yb = rwkv_scan(fwd, bwd, v_r)
    y_a = rwkv_post(yf, yb, bonus, gate, prm["gn_w"][l], prm["gn_b"][l])

    lam = (jnp.exp(jnp.sum(prm["lq1"][l] * prm["lk1"][l]))
           - jnp.exp(jnp.sum(prm["lq2"][l] * prm["lk2"][l])) + lam_init).reshape(1)
    t = qkv.shape[0]
    vt = qkv[:, 2 * D_DIFF:].T.reshape(N_DIFF_HEADS, DIFF_HEAD, t)
    vt = jnp.concatenate([vt, jnp.ones((N_DIFF_HEADS, ONES_ROWS, t), bf)], axis=1).reshape(N_DIFF_HEADS * V_ROWS, t)
    y_b = diff_attention(qkv[:, :D_DIFF].T, qkv[:, D_DIFF:2 * D_DIFF], vt, lam, prm["subln_g"][l], lam_init)

    w_out = prm["w_out"][l]
    mix = matmul_two(y_a, y_b, w_out[:D_RWKV].astype(bf), w_out[D_RWKV:].astype(bf), f32, 1024, 512)
    h, xn = resid_norm(h, mix, prm["mix_post_g"][l], prm["ffn2_pre_g"][l], 1.0, bf)

    f2 = swiglu_ffn(xn, prm["ffn2_w_gate"][l], prm["ffn2_w_up"][l],
                    prm["ffn2_w_down"][l].astype(bf), 1024, 256, 512, 512)
    return resid_norm_out(h, f2, prm["ffn2_post_g"][l], prm["final_g"][l], FFN_RESIDUAL)


def kernel(x, ffn1_pre_g, ffn1_w_gate, ffn1_w_up, ffn1_w_down, ffn1_post_g, mix_pre_g, w_in, mu_prev, mu_next, w0_f, w2_f, w0_b, w2_b, a0_f, a2_f, a0_b, a2_b, g2, k_k, k_a, r_k, gn_w, gn_b, lq1, lk1, lq2, lk2, subln_g, w_out, mix_post_g, ffn2_pre_g, ffn2_w_gate, ffn2_w_up, ffn2_w_down, ffn2_post_g, final_g):
    prm = dict(locals())
    bsz, t, d = x.shape
    assert bsz == 1
    depth = ffn1_pre_g.shape[0]
    h = x.reshape(t, d)
    for l in range(depth):
        lam_init = 0.8 - 0.6 * math.exp(-0.3 * l)
        x_norm = rms_norm_cast(h, ffn1_pre_g[l], jnp.bfloat16)
        h = _layer(h, l, lam_init, x_norm, prm)
    return h.reshape(bsz, t, d)
```

```python
import functools
import math

import jax
import jax.numpy as jnp
import numpy as np
from jax import lax
from jax.experimental import pallas as pl
from jax.experimental.pallas import tpu as pltpu

D_MODEL = 4096
D_RWKV = 2048
D_DIFF = 2048
RWKV_HEAD = 64
DECAY_RANK = 96
ICLR_RANK = 96
GATE_RANK = 256
DIFF_D = 64
DIFF_HEAD = 128
N_DIFF_HEADS = 16
ALIBI_MAX_EXP = 8.0
NORM_EPS = 1e-6
RWKV_GN_EPS = 64e-5
SUBLN_EPS = 1e-5
FFN_RESIDUAL = 0.5

LANES = 128
SUBLANES = 8
VMEM_LIMIT = 56 * 1024 * 1024

RANK_PAD = 128
SMALL_COLS = 2 * RANK_PAD + GATE_RANK
CHUNK = 64
PAIR = 2 * RWKV_HEAD


def _cparams(n_axes):
    return pltpu.CompilerParams(
        dimension_semantics=("arbitrary",) * n_axes, vmem_limit_bytes=VMEM_LIMIT)


def _dot(a, b, dims=(((1,), (0,)), ((), ()))):
    return lax.dot_general(a, b, dims, preferred_element_type=jnp.float32)


_NT = (((1,), (1,)), ((), ()))
_TN = (((0,), (0,)), ((), ()))


def _rms(x, g, eps):
    return x * lax.rsqrt(jnp.mean(x * x, axis=-1, keepdims=True) + eps) * g


def _norm_kernel(x_ref, g_ref, o_ref):
    o_ref[...] = _rms(x_ref[...], g_ref[...], NORM_EPS).astype(o_ref.dtype)


def rms_norm_cast(x, g, out_dtype, tm=256):
    m, d = x.shape
    return pl.pallas_call(
        _norm_kernel,
        grid=(m // tm,),
        in_specs=[pl.BlockSpec((tm, d), lambda i: (i, 0)),
                  pl.BlockSpec((1, d), lambda i: (0, 0))],
        out_specs=pl.BlockSpec((tm, d), lambda i: (i, 0)),
        out_shape=jax.ShapeDtypeStruct((m, d), out_dtype),
        compiler_params=_cparams(1),
        name="rms_norm",
    )(x, g.reshape(1, d))


def _resid_norm_kernel(h_ref, f_ref, gp_ref, gn_ref, h_out_ref, n_out_ref, *, scale):
    h = h_ref[...] + scale * _rms(f_ref[...], gp_ref[...], NORM_EPS)
    h_out_ref[...] = h
    n_out_ref[...] = _rms(h, gn_ref[...], NORM_EPS).astype(n_out_ref.dtype)


def resid_norm(h, f, g_post, g_next, scale, out_dtype, tm=256):
    m, d = h.shape
    row = pl.BlockSpec((tm, d), lambda i: (i, 0))
    vec = pl.BlockSpec((1, d), lambda i: (0, 0))
    return pl.pallas_call(
        functools.partial(_resid_norm_kernel, scale=scale),
        grid=(m // tm,),
        in_specs=[row, row, vec, vec],
        out_specs=[row, row],
        out_shape=[jax.ShapeDtypeStruct((m, d), jnp.float32),
                   jax.ShapeDtypeStruct((m, d), out_dtype)],
        compiler_params=_cparams(1),
        name="resid_norm",
    )(h, f, g_post.reshape(1, d), g_next.reshape(1, d))


def _resid_norm_out_kernel(h_ref, f_ref, gp_ref, gn_ref, n_out_ref, *, scale):
    h = h_ref[...] + scale * _rms(f_ref[...], gp_ref[...], NORM_EPS)
    n_out_ref[...] = _rms(h, gn_ref[...], NORM_EPS).astype(n_out_ref.dtype)


def resid_norm_out(h, f, g_post, g_next, scale, tm=256):
    m, d = h.shape
    row = pl.BlockSpec((tm, d), lambda i: (i, 0))
    vec = pl.BlockSpec((1, d), lambda i: (0, 0))
    return pl.pallas_call(
        functools.partial(_resid_norm_out_kernel, scale=scale),
        grid=(m // tm,),
        in_specs=[row, row, vec, vec],
        out_specs=row,
        out_shape=jax.ShapeDtypeStruct((m, d), jnp.float32),
        compiler_params=_cparams(1),
        name="resid_norm_out",
    )(h, f, g_post.reshape(1, d), g_next.reshape(1, d))


def _mm_kernel(x_ref, w_ref, o_ref):
    o_ref[...] = _dot(x_ref[...], w_ref[...]).astype(o_ref.dtype)


def matmul(x, w, out_dtype, tm, tn):
    m, k = x.shape
    _, n = w.shape
    return pl.pallas_call(
        _mm_kernel,
        grid=(n // tn, m // tm),
        in_specs=[pl.BlockSpec((tm, k), lambda j, i: (i, 0)),
                  pl.BlockSpec((k, tn), lambda j, i: (0, j))],
        out_specs=pl.BlockSpec((tm, tn), lambda j, i: (i, j)),
        out_shape=jax.ShapeDtypeStruct((m, n), out_dtype),
        compiler_params=_cparams(2),
        name="matmul",
    )(x, w)


def _mm_two_kernel(xa_ref, xb_ref, wa_ref, wb_ref, o_ref):
    o_ref[...] = (_dot(xa_ref[...], wa_ref[...]) + _dot(xb_ref[...], wb_ref[...])).astype(o_ref.dtype)


def matmul_two(xa, xb, wa, wb, out_dtype, tm, tn):
    m, ka = xa.shape
    _, kb = xb.shape
    _, n = wa.shape
    return pl.pallas_call(
        _mm_two_kernel,
        grid=(n // tn, m // tm),
        in_specs=[pl.BlockSpec((tm, ka), lambda j, i: (i, 0)),
                  pl.BlockSpec((tm, kb), lambda j, i: (i, 0)),
                  pl.BlockSpec((ka, tn), lambda j, i: (0, j)),
                  pl.BlockSpec((kb, tn), lambda j, i: (0, j))],
        out_specs=pl.BlockSpec((tm, tn), lambda j, i: (i, j)),
        out_shape=jax.ShapeDtypeStruct((m, n), out_dtype),
        compiler_params=_cparams(2),
        name="matmul_two",
    )(xa, xb, wa, wb)


def _mm_scaled_kernel(x_ref, w_ref, cs_ref, o_ref):
    o_ref[...] = (_dot(x_ref[...], w_ref[...]) * cs_ref[...]).astype(o_ref.dtype)


def matmul_col_scaled(x, w, col_scale, out_dtype, tm, tn):
    m, k = x.shape
    _, n = w.shape
    return pl.pallas_call(
        _mm_scaled_kernel,
        grid=(n // tn, m // tm),
        in_specs=[pl.BlockSpec((tm, k), lambda j, i: (i, 0)),
                  pl.BlockSpec((k, tn), lambda j, i: (0, j)),
                  pl.BlockSpec((1, tn), lambda j, i: (0, j))],
        out_specs=pl.BlockSpec((tm, tn), lambda j, i: (i, j)),
        out_shape=jax.ShapeDtypeStruct((m, n), out_dtype),
        compiler_params=_cparams(2),
        name="matmul_col_scaled",
    )(x, w, col_scale)


def _gate_up_kernel(x_ref, wg_ref, wu_ref, o_ref, wg_bf_ref, wu_bf_ref):
    @pl.when(pl.program_id(1) == 0)
    def _():
        wg_bf_ref[...] = wg_ref[...].astype(jnp.bfloat16)
        wu_bf_ref[...] = wu_ref[...].astype(jnp.bfloat16)

    x = x_ref[...]
    g = _dot(x, wg_bf_ref[...])
    u = _dot(x, wu_bf_ref[...])
    o_ref[...] = (g * jax.nn.sigmoid(g) * u).astype(o_ref.dtype)


def gate_up(x, wg, wu, tm, tn):
    m, k = x.shape
    _, n = wg.shape
    wspec = pl.BlockSpec((k, tn), lambda j, i: (0, j))
    return pl.pallas_call(
        _gate_up_kernel,
        grid=(n // tn, m // tm),
        in_specs=[pl.BlockSpec((tm, k), lambda j, i: (i, 0)), wspec, wspec],
        out_specs=pl.BlockSpec((tm, tn), lambda j, i: (i, j)),
        out_shape=jax.ShapeDtypeStruct((m, n), jnp.bfloat16),
        scratch_shapes=[pltpu.VMEM((k, tn), jnp.bfloat16)] * 2,
        compiler_params=_cparams(2),
        name="gate_up",
    )(x, wg, wu)


def swiglu_ffn(xn, wg, wu, wd, tm_gu, tn_gu, tm_d, tn_d):
    a = gate_up(xn, wg, wu, tm_gu, tn_gu)
    return matmul(a, wd, jnp.float32, tm_d, tn_d)


def _bf16_split(x, n):
    pieces = []
    for _ in range(n - 1):
        p = x.astype(jnp.bfloat16)
        pieces.append(p)
        x = x - p.astype(jnp.float32)
    pieces.append(x.astype(jnp.bfloat16))
    return pieces


HEAD_SUM_PIECES = 2
CUMSUM_PIECES = 3


def _head_sum(x, w_bd):
    parts = []
    for s in range(x.shape[1] // LANES):
        pieces = _bf16_split(x[:, s * LANES:(s + 1) * LANES], HEAD_SUM_PIECES)
        parts.append(sum(_dot(p, w_bd) for p in pieces))
    return jnp.concatenate(parts, axis=1)


def _token_shift(p_ref, pp_ref, pn_ref, mup_ref, mun_ref, first, last):
    p = p_ref[...]
    rows = p.shape[0]
    rid = lax.broadcasted_iota(jnp.int32, p.shape, 0)
    prev_row = jnp.where(first, 0.0, pp_ref[SUBLANES - 1:SUBLANES, :])
    next_row = jnp.where(last, 0.0, pn_ref[0:1, :])
    p_prev = jnp.where(rid == 0, prev_row, pltpu.roll(p, 1, 0))
    p_next = jnp.where(rid == rows - 1, next_row, pltpu.roll(p, rows - 1, 0))
    return p + mup_ref[...] * (p_prev - p) + mun_ref[...] * (p_next - p)


def _rwkv_prep_kernel(
        p_ref, pp_ref, pn_ref, s_ref, sp_ref, sn_ref,
        mup_ref, mun_ref, mups_ref, muns_ref,
        w0f_ref, w0b_ref, a0f_ref, a0b_ref, kk_ref, ka_ref, rk_ref,
        w2f_ref, w2b_ref, a2f_ref, a2b_ref, g2_ref,
        atf_ref, btf_ref, ktf_ref, rtf_ref, glf_ref,
        atb_ref, btb_ref, ktb_ref, rtb_ref, glb_ref,
        v_ref, gate_ref, bonus_ref):
    i = pl.program_id(0)
    first = i == 0
    last = i == pl.num_programs(0) - 1
    tm = p_ref.shape[0]
    n_chunks = tm // CHUNK

    p = _token_shift(p_ref, pp_ref, pn_ref, mup_ref, mun_ref, first, last)
    s = _token_shift(s_ref, sp_ref, sn_ref, mups_ref, muns_ref, first, last)
    r = p[:, 0:D_RWKV]
    k = p[:, D_RWKV:2 * D_RWKV]
    v = p[:, 2 * D_RWKV:3 * D_RWKV]
    hw = jnp.tanh(s[:, 0:RANK_PAD]).astype(jnp.bfloat16)
    xa = s[:, RANK_PAD:2 * RANK_PAD].astype(jnp.bfloat16)
    sg = jax.nn.sigmoid(s[:, 2 * RANK_PAD:]).astype(jnp.bfloat16)

    li = lax.broadcasted_iota(jnp.int32, (LANES, LANES), 0) // RWKV_HEAD
    lj = lax.broadcasted_iota(jnp.int32, (LANES, LANES), 1) // RWKV_HEAD
    ones_bd = (li == lj).astype(jnp.bfloat16)

    kk = k * kk_ref[...]
    kk = kk / jnp.maximum(jnp.sqrt(_head_sum(kk * kk, ones_bd)), 1e-12)

    ti = lax.broadcasted_iota(jnp.int32, (tm, tm), 0)
    tj = lax.broadcasted_iota(jnp.int32, (tm, tm), 1)
    same_chunk = (ti // CHUNK) == (tj // CHUNK)

    k_sum = jnp.zeros_like(k)
    dirs = ((w0f_ref, w2f_ref, a0f_ref, a2f_ref, atf_ref, btf_ref, ktf_ref, rtf_ref, glf_ref, False),
            (w0b_ref, w2b_ref, a0b_ref, a2b_ref, atb_ref, btb_ref, ktb_ref, rtb_ref, glb_ref, True))
    for w0_ref, w2_ref, a0_ref, a2_ref, at_ref, bt_ref, kt_ref, rt_ref, gl_ref, rev in dirs:
        z = w0_ref[...] + _dot(hw, w2_ref[...])
        lw = -math.exp(-0.5) * jax.nn.sigmoid(z)
        a = jax.nn.sigmoid(a0_ref[...] + _dot(xa, a2_ref[...]))
        k_dir = k * (1.0 + (a - 1.0) * ka_ref[...])
        k_sum = k_sum + k_dir
        order = (tj >= ti) if rev else (tj <= ti)
        tri = jnp.where(same_chunk & order, 1.0, 0.0).astype(jnp.bfloat16)
        c = sum(_dot(tri, piece) for piece in _bf16_split(lw, CUMSUM_PIECES))
        e_neg = jnp.exp(-c)
        at_ref[...] = (-kk * jnp.exp(c - lw)).astype(at_ref.dtype)
        bt_ref[...] = (kk * a * e_neg).astype(bt_ref.dtype)
        kt_ref[...] = (k_dir * e_neg).astype(kt_ref.dtype)
        e_pos = jnp.exp(c)
        rt_ref[...] = (r * e_pos).astype(rt_ref.dtype)
        for j in range(n_chunks):
            end = j * CHUNK if rev else (j + 1) * CHUNK - 1
            gl_ref[j] = e_pos[end:end + 1, :]

    v_ref[...] = v.astype(v_ref.dtype)
    gate_ref[...] = _dot(sg, g2_ref[...]).astype(gate_ref.dtype)
    bonus_ref[...] = (_head_sum(r * k_sum * rk_ref[...], ones_bd) * v).astype(bonus_ref.dtype)


def rwkv_prep(proj, params, tm=128):
    t = proj.shape[0]
    nb = t // tm
    hb = tm // SUBLANES
    n_halo = t // SUBLANES
    wide = 3 * D_RWKV
    small_blk = wide // SMALL_COLS

    def row(c, cb=0):
        return pl.BlockSpec((tm, c), lambda i: (i, cb))

    def halo_prev(c, cb=0):
        return pl.BlockSpec((SUBLANES, c), lambda i: (jnp.maximum(i * hb - 1, 0), cb))

    def halo_next(c, cb=0):
        return pl.BlockSpec((SUBLANES, c), lambda i: (jnp.minimum((i + 1) * hb, n_halo - 1), cb))

    def vec(c):
        return pl.BlockSpec((1, c), lambda i: (0, 0))

    def mat(r_, c):
        return pl.BlockSpec((r_, c), lambda i: (0, 0))

    gl_spec = pl.BlockSpec((tm // CHUNK, 1, D_RWKV), lambda i: (i, 0, 0))
    big = jax.ShapeDtypeStruct((t, D_RWKV), jnp.float32)
    gl = jax.ShapeDtypeStruct((t // CHUNK, 1, D_RWKV), jnp.float32)
    in_specs = [row(wide), halo_prev(wide), halo_next(wide),
                row(SMALL_COLS, small_blk), halo_prev(SMALL_COLS, small_blk), halo_next(SMALL_COLS, small_blk),
                vec(wide), vec(wide), vec(SMALL_COLS), vec(SMALL_COLS)]
    in_specs += [vec(D_RWKV)] * 7
    in_specs += [mat(RANK_PAD, D_RWKV)] * 4 + [mat(GATE_RANK, D_RWKV)]
    dir_specs = [row(D_RWKV)] * 4 + [gl_spec]
    big_bf = jax.ShapeDtypeStruct((t, D_RWKV), jnp.bfloat16)
    dir_shapes = [big_bf] * 4 + [gl]
    return pl.pallas_call(
        _rwkv_prep_kernel,
        grid=(nb,),
        in_specs=in_specs,
        out_specs=dir_specs + dir_specs + [row(D_RWKV)] * 3,
        out_shape=dir_shapes + dir_shapes + [big_bf, big_bf, big_bf],
        compiler_params=_cparams(1),
        name="rwkv_prep",
    )(proj, proj, proj, proj, proj, proj, *params)


def _stack(x, lane_head):
    zero = jnp.zeros_like(x)
    return jnp.concatenate([jnp.where(lane_head == 0, x, zero),
                            jnp.where(lane_head == 1, x, zero)], axis=0)


def _scan_masks(rev):
    n2 = 2 * CHUNK
    ri = lax.broadcasted_iota(jnp.int32, (n2, n2), 0)
    ci = lax.broadcasted_iota(jnp.int32, (n2, n2), 1)
    same = (ri // CHUNK) == (ci // CHUNK)
    before = (ci > ri) if rev else (ci < ri)
    strict = same & before
    incl = same & (before | (ri == ci))
    eye = jnp.where(ri == ci, 1.0, 0.0)
    return strict, jnp.concatenate([incl, incl], axis=1), eye


def _scan_prepare(units):
    bf = jnp.bfloat16
    n2 = 2 * CHUNK
    lane_head = lax.broadcasted_iota(jnp.int32, (CHUNK, PAIR), 1) // RWKV_HEAD
    stacked = [[_stack(x, lane_head) for x in u[:5]] for u in units]
    bks = [jnp.concatenate([s[1], s[2]], axis=0) for s in stacked]
    g_as = [_dot(s[0], bk, _NT) for s, bk in zip(stacked, bks)]
    g_rs = [_dot(s[3], bk, _NT) for s, bk in zip(stacked, bks)]
    xs = [jnp.where(u[6][0], g[:, :n2], 0.0) for u, g in zip(units, g_as)]
    a_aks = [jnp.where(u[6][0], g[:, n2:], 0.0).astype(bf) for u, g in zip(units, g_as)]
    a_rs = [jnp.where(u[6][1], g, 0.0).astype(bf) for u, g in zip(units, g_rs)]
    akv = [_dot(a, s[4]).astype(bf) for a, s in zip(a_aks, stacked)]

    tinvs = [u[6][2] + x for u, x in zip(units, xs)]
    for _ in range(int(math.log2(CHUNK)) - 1):
        xbs = [x.astype(bf) for x in xs]
        xs = [_dot(xb, xb) for xb in xbs]
        tinvs = [t + _dot(t.astype(bf), x.astype(bf)) for t, x in zip(tinvs, xs)]

    wqs = [_dot(t.astype(bf), jnp.concatenate([s[0], kv], axis=1))
           for t, s, kv in zip(tinvs, stacked, akv)]
    out = []
    for u, s, bk, wq, a_r in zip(units, stacked, bks, wqs, a_rs):
        gl = u[5]
        w = wq[:, :PAIR].astype(bf)
        q = wq[:, PAIR:]
        bkg = (bk * gl).astype(bf)
        m = _dot(w, bkg[:n2], _TN)
        n = _dot(jnp.concatenate([q.astype(bf), s[4]], axis=0), bkg, _TN)
        out.append((w, q, s[3], a_r, s[4], m.astype(bf), n, gl))
    return out


def _rwkv_scan_kernel(atf_ref, btf_ref, ktf_ref, rtf_ref, glf_ref, vf_ref,
                      atb_ref, btb_ref, ktb_ref, rtb_ref, glb_ref, vb_ref,
                      yf_ref, yb_ref, sf_ref, sb_ref, *, n_blk, n_par):
    @pl.when(pl.program_id(1) == 0)
    def _():
        sf_ref[...] = jnp.zeros_like(sf_ref)
        sb_ref[...] = jnp.zeros_like(sb_ref)

    def tile(j, p):
        return pl.ds(j * CHUNK, CHUNK), pl.ds(p * PAIR, PAIR)

    masks_f = _scan_masks(False)
    masks_b = _scan_masks(True)
    units = []
    for p in range(n_par):
        for j in range(n_blk):
            units.append(tuple(ref[tile(j, p)] for ref in (atf_ref, btf_ref, ktf_ref, rtf_ref, vf_ref))
                         + (glf_ref[j, :, pl.ds(p * PAIR, PAIR)], masks_f))
            units.append(tuple(ref[tile(j, p)] for ref in (atb_ref, btb_ref, ktb_ref, rtb_ref, vb_ref))
                         + (glb_ref[j, :, pl.ds(p * PAIR, PAIR)], masks_b))
    prep = _scan_prepare(units)

    sf = [sf_ref[p] for p in range(n_par)]
    sb = [sb_ref[p] for p in range(n_par)]
    pending = []
    for j in range(n_blk):
        jb = n_blk - 1 - j
        for p in range(n_par):
            pf = prep[2 * (p * n_blk + j)]
            pb = prep[2 * (p * n_blk + jb) + 1]
            sfb = sf[p].astype(jnp.bfloat16)
            sbb = sb[p].astype(jnp.bfloat16)
            sf[p] = sf[p] * pf[7] + _dot(sfb, pf[5]) + pf[6]
            sb[p] = sb[p] * pb[7] + _dot(sbb, pb[5]) + pb[6]
            pending.append((pf, sfb, yf_ref, tile(j, p)))
            pending.append((pb, sbb, yb_ref, tile(jb, p)))
    for p in range(n_par):
        sf_ref[p] = sf[p]
        sb_ref[p] = sb[p]
    us = [(_dot(u[0], s0b, _NT) + u[1]).astype(jnp.bfloat16) for u, s0b, _, _ in pending]
    for (u, s0b, y_ref, where), uu in zip(pending, us):
        y2 = _dot(u[2], s0b, _NT) + _dot(u[3], jnp.concatenate([uu, u[4]], axis=0))
        y_ref[where] = (y2[:CHUNK] + y2[CHUNK:]).astype(y_ref.dtype)


def rwkv_scan(fwd, bwd, v, n_blk=4, n_par=2):
    t = v.shape[0]
    nb = t // (CHUNK * n_blk)
    n_grp = D_RWKV // (PAIR * n_par)
    width = PAIR * n_par
    f_blk = pl.BlockSpec((CHUNK * n_blk, width), lambda p, c: (c, p))
    b_blk = pl.BlockSpec((CHUNK * n_blk, width), lambda p, c: (nb - 1 - c, p))
    f_gl = pl.BlockSpec((n_blk, 1, width), lambda p, c: (c, 0, p))
    b_gl = pl.BlockSpec((n_blk, 1, width), lambda p, c: (nb - 1 - c, 0, p))
    out = jax.ShapeDtypeStruct((t, D_RWKV), jnp.bfloat16)
    return pl.pallas_call(
        functools.partial(_rwkv_scan_kernel, n_blk=n_blk, n_par=n_par),
        grid=(n_grp, nb),
        in_specs=[f_blk] * 4 + [f_gl, f_blk] + [b_blk] * 4 + [b_gl, b_blk],
        out_specs=[f_blk, b_blk],
        out_shape=[out, out],
        scratch_shapes=[pltpu.VMEM((n_par, PAIR, PAIR), jnp.float32)] * 2,
        compiler_params=_cparams(2),
        name="rwkv_scan",
    )(*fwd, v, *bwd, v)


def _rwkv_post_kernel(yf_ref, yb_ref, bonus_ref, gate_ref, gw_ref, gb_ref, o_ref):
    li = lax.broadcasted_iota(jnp.int32, (LANES, LANES), 0) // RWKV_HEAD
    lj = lax.broadcasted_iota(jnp.int32, (LANES, LANES), 1) // RWKV_HEAD
    mean_bd = jnp.where(li == lj, 1.0 / RWKV_HEAD, 0.0).astype(jnp.bfloat16)
    y = yf_ref[...].astype(jnp.float32) + yb_ref[...].astype(jnp.float32)
    d = y - _head_sum(y, mean_bd)
    var = _head_sum(d * d, mean_bd)
    yn = d * lax.rsqrt(var + RWKV_GN_EPS) * gw_ref[...] + gb_ref[...]
    o_ref[...] = ((yn + bonus_ref[...].astype(jnp.float32))
                  * gate_ref[...].astype(jnp.float32)).astype(o_ref.dtype)


def rwkv_post(yf, yb, bonus, gate, gn_w, gn_b, tm=256):
    t = yf.shape[0]
    row = pl.BlockSpec((tm, D_RWKV), lambda i: (i, 0))
    vec = pl.BlockSpec((1, D_RWKV), lambda i: (0, 0))
    return pl.pallas_call(
        _rwkv_post_kernel,
        grid=(t // tm,),
        in_specs=[row, row, row, row, vec, vec],
        out_specs=row,
        out_shape=jax.ShapeDtypeStruct((t, D_RWKV), jnp.bfloat16),
        compiler_params=_cparams(1),
        name="rwkv_post",
    )(yf, yb, bonus, gate, gn_w.reshape(1, -1), gn_b.reshape(1, -1))


SLOPE_PIECES = 3
SC_PER_HEAD = 2 + SLOPE_PIECES
ONES_ROWS = 16
V_ROWS = DIFF_HEAD + ONES_ROWS
UNDERFLOW_LOG2 = 160.0
BOUND_SLACK = 1.01


def _diff_attn_kernel(sc_ref, qt_ref, k_ref, vt_ref, g_ref, o_ref, s_ref, kmax_ref, *, tq, tk, lam_init):
    bf = jnp.bfloat16
    h = pl.program_id(0)
    qi = pl.program_id(1)
    n_kv = k_ref.shape[0] // tk
    lam = sc_ref[0]
    base = 1 + h * SC_PER_HEAD
    slope = sc_ref[base]
    pieces = [sc_ref[base + 1 + i] for i in range(SLOPE_PIECES)]
    inv_slope = sc_ref[base + 1 + SLOPE_PIECES]

    @pl.when(qi == 0)
    def _():
        li = lax.broadcasted_iota(jnp.int32, (LANES, LANES), 0) // DIFF_D
        lj = lax.broadcasted_iota(jnp.int32, (LANES, LANES), 1) // DIFF_D
        half_ones = (li == lj).astype(bf)

        def tile_max(j, best):
            kf = k_ref[pl.ds(pl.multiple_of(j * tk, tk), tk), :].astype(jnp.float32)
            sq = sum(_dot(p, half_ones) for p in _bf16_split(kf * kf, HEAD_SUM_PIECES))
            return jnp.maximum(best, jnp.max(sq, axis=0, keepdims=True))

        best = lax.fori_loop(0, n_kv, tile_max, jnp.zeros((1, LANES), jnp.float32))
        kmax_ref[0] = jnp.max(best)

    qt = qt_ref[...]
    q_sq = jnp.square(qt.astype(jnp.float32))
    q_n2 = jnp.maximum(jnp.sum(q_sq[:DIFF_D], axis=0, keepdims=True),
                       jnp.sum(q_sq[DIFF_D:], axis=0, keepdims=True))
    bound = jnp.max(jnp.sqrt(q_n2 * kmax_ref[0]))
    reach = (2.0 * BOUND_SLACK * bound + UNDERFLOW_LOG2) * inv_slope
    width = jnp.int32(1)
    for d in range(1, n_kv):
        width = width + jnp.where(reach < float(d * tk), 0, 1).astype(jnp.int32)
    qrow = lax.broadcasted_iota(jnp.int32, (DIFF_HEAD, tq), 0)
    feat = jnp.zeros((DIFF_HEAD, tq), jnp.float32)
    for i, pc in enumerate(pieces):
        feat = jnp.where(qrow % DIFF_D == i, 2.0 * pc, feat)
        feat = jnp.where(qrow % DIFF_D == SLOPE_PIECES + i, pc, feat)
    feat = feat.astype(bf)
    own_rows = [qrow < DIFF_D, qrow >= DIFF_D]
    qts = [jnp.where(own, qt, feat) for own in own_rows]

    kl = lax.broadcasted_iota(jnp.int32, (tk, LANES), 1)
    kr = lax.broadcasted_iota(jnp.int32, (tk, LANES), 0)
    kfeat = jnp.where(kl % DIFF_D < SLOPE_PIECES, kr >> 1,
                      jnp.where(kl % DIFF_D < 2 * SLOPE_PIECES, kr & 1, 0)).astype(jnp.float32)
    kfeat_before = kfeat.astype(bf)
    kfeat_after = (-kfeat).astype(bf)
    kzero = jnp.zeros((tk, LANES), bf)
    own_lanes = [kl < DIFF_D, kl >= DIFF_D]

    qpos = qi * tq + lax.broadcasted_iota(jnp.int32, (1, tq), 1)
    q_bias = slope * qpos.astype(jnp.float32)

    def online(carry, s, t_q, vt):
        m, acc = carry
        m_new = jnp.maximum(m, jnp.max(s, axis=0, keepdims=True) + t_q)
        alpha = jnp.exp2(m - m_new)
        p = jnp.exp2(s - (m_new - t_q))
        return m_new, alpha * acc + _dot(vt, p.astype(bf))

    n_diag = max(tq // tk, 1)
    jd = (qi * tq) // tk
    lo = jnp.maximum(jd - width, 0)
    hi = jnp.minimum(jd + n_diag - 1 + width, n_kv - 1)
    n_off = hi - lo + 1 - n_diag

    def tile_of(n):
        j = lo + n
        return j + n_diag * (j >= jd).astype(jnp.int32)

    def scores(n, slot, mp):
        j = tile_of(n)
        start = pl.multiple_of(j * tk, tk)
        kb = k_ref[pl.ds(start, tk), :]
        kf = jnp.where(j < jd, kfeat_before, kfeat_after)
        s_ref[slot, mp] = _dot(jnp.where(own_lanes[mp], kb, kf), qts[mp])

    def consume(n, slot, mp, carry):
        j = tile_of(n)
        start = pl.multiple_of(j * tk, tk)
        vt = vt_ref[:, pl.ds(start, tk)]
        sign = jnp.where(j < jd, 1.0, -1.0)
        t_q = sign * (slope * (j * tk).astype(jnp.float32) - q_bias)
        return online(carry, s_ref[slot, mp], t_q, vt)

    def step(n_next, n_cur, slot_next, slot_cur, carries):
        out = []
        for mp in range(2):
            scores(n_next, slot_next, mp)
            out.append(consume(n_cur, slot_cur, mp, carries[mp]))
        return tuple(out)

    def diag_scores(d, slot):
        kb_d = k_ref[pl.ds(pl.multiple_of((jd + d) * tk, tk), tk), :]
        kpos = (jd + d) * tk + lax.broadcasted_iota(jnp.int32, (tk, tq), 0)
        bias = slope * jnp.abs(kpos - qpos).astype(jnp.float32)
        for mp, (own, q_m) in enumerate(zip(own_lanes, qts)):
            s_ref[slot, mp] = _dot(jnp.where(own, kb_d, kzero), q_m) - bias

    def diag_consume(d, slot, mp, carry):
        vt_d = vt_ref[:, pl.ds(pl.multiple_of((jd + d) * tk, tk), tk)]
        return online(carry, s_ref[slot, mp], 0.0, vt_d)

    init = (jnp.full((1, tq), -jnp.inf, jnp.float32), jnp.zeros((V_ROWS, tq), jnp.float32))
    carries = [init, init]
    diag_slots = [1] if n_diag == 1 else [0, 1]
    for d, slot in enumerate(diag_slots[:-1]):
        diag_scores(d, slot)
    diag_scores(n_diag - 1, 1)
    for d, slot in enumerate(diag_slots[:-1]):
        carries = [diag_consume(d, slot, mp, carries[mp]) for mp in range(2)]
    for mp in range(2):
        scores(jnp.int32(0), 0, mp)
        carries[mp] = diag_consume(n_diag - 1, 1, mp, carries[mp])

    def body(i, carries):
        n = 2 * i
        carries = step(n + 1, n, 1, 0, carries)
        return step(n + 2, n + 1, 0, 1, carries)

    n_pairs = (n_off - 1) // 2
    carries = lax.fori_loop(0, n_pairs, body, tuple(carries))
    last = n_off - 1

    def tail_two(carries):
        carries = step(last, last - 1, 1, 0, carries)
        return tuple(consume(last, 1, mp, carries[mp]) for mp in range(2))

    def tail_one(carries):
        return tuple(consume(last, 0, mp, carries[mp]) for mp in range(2))

    (_, acc0), (_, acc1) = lax.cond(last == 2 * n_pairs + 1, tail_two, tail_one, carries)
    o0 = acc0[:DIFF_HEAD] / acc0[DIFF_HEAD:DIFF_HEAD + 1]
    o1 = acc1[:DIFF_HEAD] / acc1[DIFF_HEAD:DIFF_HEAD + 1]
    o = (o0 - lam * o1).T
    o_ref[...] = (_rms(o, g_ref[...], SUBLN_EPS) * (1.0 - lam_init)).astype(o_ref.dtype)


def _bf16_pieces(x, n):
    out = []
    for _ in range(n):
        p = (x.view(np.uint32) & np.uint32(0xFFFF0000)).view(np.float32)
        out.append(p)
        x = (x - p).astype(np.float32)
    return out


def diff_attention(qt, k, vt, lam, subln_g, lam_init, tq=1024, tk=512):
    t = k.shape[0]
    assert tq // tk in (1, 2) and tq % tk == 0 and tk // 2 <= 256
    assert t // tk > tq // tk
    slopes = 2.0 ** (-ALIBI_MAX_EXP * np.arange(1, N_DIFF_HEADS + 1, dtype=np.float64) / N_DIFF_HEADS)
    slopes = (slopes * math.log2(math.e)).astype(np.float32)
    pieces = _bf16_pieces(slopes, SLOPE_PIECES)
    slope_used = sum(pieces)
    per_head = np.stack([slope_used] + pieces + [(1.0 / slope_used).astype(np.float32)], axis=1)
    lam = jnp.concatenate([lam, jnp.asarray(per_head.reshape(-1))])
    return pl.pallas_call(
        functools.partial(_diff_attn_kernel, tq=tq, tk=tk, lam_init=lam_init),
        grid=(N_DIFF_HEADS, t // tq),
        in_specs=[pl.BlockSpec(memory_space=pltpu.SMEM),
                  pl.BlockSpec((DIFF_HEAD, tq), lambda h, i: (h, i)),
                  pl.BlockSpec((t, DIFF_HEAD), lambda h, i: (0, h)),
                  pl.BlockSpec((V_ROWS, t), lambda h, i: (h, 0)),
                  pl.BlockSpec((1, DIFF_HEAD), lambda h, i: (0, 0))],
        out_specs=pl.BlockSpec((tq, DIFF_HEAD), lambda h, i: (i, h)),
        out_shape=jax.ShapeDtypeStruct((t, D_DIFF), jnp.bfloat16),
        scratch_shapes=[pltpu.VMEM((2, 2, tk, tq), jnp.float32),
                        pltpu.SMEM((1,), jnp.float32)],
        compiler_params=_cparams(2),
        name="diff_attn",
    )(lam, qt, k, vt, subln_g.reshape(1, DIFF_HEAD))


def _pad_cols(w, to):
    return jnp.pad(w, ((0, 0), (0, to - w.shape[1])))


def _layer(h, l, lam_init, x_norm, prm):
    bf = jnp.bfloat16
    f32 = jnp.float32
    f1 = swiglu_ffn(x_norm, prm["ffn1_w_gate"][l], prm["ffn1_w_up"][l],
                    prm["ffn1_w_down"][l].astype(bf), 1024, 256, 512, 512)
    h, xn = resid_norm(h, f1, prm["ffn1_post_g"][l], prm["mix_pre_g"][l], FFN_RESIDUAL, bf)

    w_in = prm["w_in"][l]
    c_rkv = 3 * D_RWKV
    c_w = c_rkv + DECAY_RANK
    c_a = c_w + ICLR_RANK
    c_g = c_a + GATE_RANK
    w_small = jnp.concatenate([_pad_cols(w_in[:, c_rkv:c_w], RANK_PAD),
                               _pad_cols(w_in[:, c_w:c_a], RANK_PAD),
                               w_in[:, c_a:c_g]], axis=1)
    w_rwkv = jnp.concatenate([w_in[:, :c_rkv], w_small], axis=1).astype(bf)
    proj = matmul(xn, w_rwkv, f32, 1024, 512)
    q_scale = DIFF_D ** -0.5 * math.log2(math.e)
    col_scale = jnp.concatenate([jnp.full((1, D_DIFF), q_scale, f32), jnp.ones((1, 2 * D_DIFF), f32)], axis=1)
    qkv = matmul_col_scaled(xn, w_in[:, c_g:].astype(bf), col_scale, bf, 1024, 512)

    def small_vec(a):
        return jnp.concatenate([_pad_cols(a[None, c_rkv:c_w], RANK_PAD),
                                _pad_cols(a[None, c_w:c_a], RANK_PAD),
                                a[None, c_a:c_g]], axis=1)

    def pad_rows(w):
        return jnp.pad(w, ((0, RANK_PAD - w.shape[0]), (0, 0))).astype(bf)

    mu_p, mu_n = prm["mu_prev"][l], prm["mu_next"][l]
    vecs = [prm[n][l].reshape(1, D_RWKV) for n in ("w0_f", "w0_b", "a0_f", "a0_b", "k_k", "k_a", "r_k")]
    mats = [pad_rows(prm["w2_f"][l]), pad_rows(prm["w2_b"][l]),
            pad_rows(prm["a2_f"][l]), pad_rows(prm["a2_b"][l]), prm["g2"][l].astype(bf)]
    prep = rwkv_prep(proj, [mu_p[None, :c_rkv], mu_n[None, :c_rkv], small_vec(mu_p), small_vec(mu_n)]
                     + vecs + mats)
    fwd, bwd, (v_r, gate, bonus) = prep[0:5], prep[5:10], prep[10:13]
    yf, yb = rwkv_scan(fwd, bwd, v_r)
    y_a = rwkv_post(yf, yb, bonus, gate, prm["gn_w"][l], prm["gn_b"][l])

    lam = (jnp.exp(jnp.sum(prm["lq1"][l] * prm["lk1"][l]))
           - jnp.exp(jnp.sum(prm["lq2"][l] * prm["lk2"][l])) + lam_init).reshape(1)
    t = qkv.shape[0]
    vt = qkv[:, 2 * D_DIFF:].T.reshape(N_DIFF_HEADS, DIFF_HEAD, t)
    vt = jnp.concatenate([vt, jnp.ones((N_DIFF_HEADS, ONES_ROWS, t), bf)], axis=1).reshape(N_DIFF_HEADS * V_ROWS, t)
    y_b = diff_attention(qkv[:, :D_DIFF].T, qkv[:, D_DIFF:2 * D_DIFF], vt, lam, prm["subln_g"][l], lam_init)

    w_out = prm["w_out"][l]
    mix = matmul_two(y_a, y_b, w_out[:D_RWKV].astype(bf), w_out[D_RWKV:].astype(bf), f32, 1024, 512)
    h, xn = resid_norm(h, mix, prm["mix_post_g"][l], prm["ffn2_pre_g"][l], 1.0, bf)

    f2 = swiglu_ffn(xn, prm["ffn2_w_gate"][l], prm["ffn2_w_up"][l],
                    prm["ffn2_w_down"][l].astype(bf), 1024, 256, 512, 512)
    return resid_norm_out(h, f2, prm["ffn2_post_g"][l], prm["final_g"][l], FFN_RESIDUAL)


def kernel(x, ffn1_pre_g, ffn1_w_gate, ffn1_w_up, ffn1_w_down, ffn1_post_g, mix_pre_g, w_in, mu_prev, mu_next, w0_f, w2_f, w0_b, w2_b, a0_f, a2_f, a0_b, a2_b, g2, k_k, k_a, r_k, gn_w, gn_b, lq1, lk1, lq2, lk2, subln_g, w_out, mix_post_g, ffn2_pre_g, ffn2_w_gate, ffn2_w_up, ffn2_w_down, ffn2_post_g, final_g):
    prm = dict(locals())
    bsz, t, d = x.shape
    assert bsz == 1
    depth = ffn1_pre_g.shape[0]
    h = x.reshape(t, d)
    for l in range(depth):
        lam_init = 0.8 - 0.6 * math.exp(-0.3 * l)
        x_norm = rms_norm_cast(h, ffn1_pre_g[l], jnp.bfloat16)
        h = _layer(h, l, lam_init, x_norm, prm)
    return h.reshape(bsz, t, d)
```

```python
import functools
import math

import jax
import jax.numpy as jnp
import numpy as np
from jax import lax
from jax.experimental import pallas as pl
from jax.experimental.pallas import tpu as pltpu

D_MODEL = 4096
D_RWKV = 2048
D_DIFF = 2048
RWKV_HEAD = 64
DECAY_RANK = 96
ICLR_RANK = 96
GATE_RANK = 256
DIFF_D = 64
DIFF_HEAD = 128
N_DIFF_HEADS = 16
ALIBI_MAX_EXP = 8.0
NORM_EPS = 1e-6
RWKV_GN_EPS = 64e-5
SUBLN_EPS = 1e-5
FFN_RESIDUAL = 0.5

LANES = 128
SUBLANES = 8
VMEM_LIMIT = 56 * 1024 * 1024

RANK_PAD = 128
SMALL_COLS = 2 * RANK_PAD + GATE_RANK
CHUNK = 64
PAIR = 2 * RWKV_HEAD


def _cparams(n_axes):
    return pltpu.CompilerParams(
        dimension_semantics=("arbitrary",) * n_axes, vmem_limit_bytes=VMEM_LIMIT)


def _dot(a, b, dims=(((1,), (0,)), ((), ()))):
    return lax.dot_general(a, b, dims, preferred_element_type=jnp.float32)


_NT = (((1,), (1,)), ((), ()))
_TN = (((0,), (0,)), ((), ()))


def _rms(x, g, eps):
    return x * lax.rsqrt(jnp.mean(x * x, axis=-1, keepdims=True) + eps) * g


def _norm_kernel(x_ref, g_ref, o_ref):
    o_ref[...] = _rms(x_ref[...], g_ref[...], NORM_EPS).astype(o_ref.dtype)


def rms_norm_cast(x, g, out_dtype, tm=256):
    m, d = x.shape
    return pl.pallas_call(
        _norm_kernel,
        grid=(m // tm,),
        in_specs=[pl.BlockSpec((tm, d), lambda i: (i, 0)),
                  pl.BlockSpec((1, d), lambda i: (0, 0))],
        out_specs=pl.BlockSpec((tm, d), lambda i: (i, 0)),
        out_shape=jax.ShapeDtypeStruct((m, d), out_dtype),
        compiler_params=_cparams(1),
        name="rms_norm",
    )(x, g.reshape(1, d))


def _resid_norm_kernel(h_ref, f_ref, gp_ref, gn_ref, h_out_ref, n_out_ref, *, scale):
    h = h_ref[...] + scale * _rms(f_ref[...], gp_ref[...], NORM_EPS)
    h_out_ref[...] = h
    n_out_ref[...] = _rms(h, gn_ref[...], NORM_EPS).astype(n_out_ref.dtype)


def resid_norm(h, f, g_post, g_next, scale, out_dtype, tm=256):
    m, d = h.shape
    row = pl.BlockSpec((tm, d), lambda i: (i, 0))
    vec = pl.BlockSpec((1, d), lambda i: (0, 0))
    return pl.pallas_call(
        functools.partial(_resid_norm_kernel, scale=scale),
        grid=(m // tm,),
        in_specs=[row, row, vec, vec],
        out_specs=[row, row],
        out_shape=[jax.ShapeDtypeStruct((m, d), jnp.float32),
                   jax.ShapeDtypeStruct((m, d), out_dtype)],
        compiler_params=_cparams(1),
        name="resid_norm",
    )(h, f, g_post.reshape(1, d), g_next.reshape(1, d))


def _resid_norm_out_kernel(h_ref, f_ref, gp_ref, gn_ref, n_out_ref, *, scale):
    h = h_ref[...] + scale * _rms(f_ref[...], gp_ref[...], NORM_EPS)
    n_out_ref[...] = _rms(h, gn_ref[...], NORM_EPS).astype(n_out_ref.dtype)


def resid_norm_out(h, f, g_post, g_next, scale, tm=256):
    m, d = h.shape
    row = pl.BlockSpec((tm, d), lambda i: (i, 0))
    vec = pl.BlockSpec((1, d), lambda i: (0, 0))
    return pl.pallas_call(
        functools.partial(_resid_norm_out_kernel, scale=scale),
        grid=(m // tm,),
        in_specs=[row, row, vec, vec],
        out_specs=row,
        out_shape=jax.ShapeDtypeStruct((m, d), jnp.float32),
        compiler_params=_cparams(1),
        name="resid_norm_out",
    )(h, f, g_post.reshape(1, d), g_next.reshape(1, d))


def _mm_kernel(x_ref, w_ref, o_ref):
    o_ref[...] = _dot(x_ref[...], w_ref[...]).astype(o_ref.dtype)


def matmul(x, w, out_dtype, tm, tn):
    m, k = x.shape
    _, n = w.shape
    return pl.pallas_call(
        _mm_kernel,
        grid=(n // tn, m // tm),
        in_specs=[pl.BlockSpec((tm, k), lambda j, i: (i, 0)),
                  pl.BlockSpec((k, tn), lambda j, i: (0, j))],
        out_specs=pl.BlockSpec((tm, tn), lambda j, i: (i, j)),
        out_shape=jax.ShapeDtypeStruct((m, n), out_dtype),
        compiler_params=_cparams(2),
        name="matmul",
    )(x, w)


def _mm_two_kernel(xa_ref, xb_ref, wa_ref, wb_ref, o_ref):
    o_ref[...] = (_dot(xa_ref[...], wa_ref[...]) + _dot(xb_ref[...], wb_ref[...])).astype(o_ref.dtype)


def matmul_two(xa, xb, wa, wb, out_dtype, tm, tn):
    m, ka = xa.shape
    _, kb = xb.shape
    _, n = wa.shape
    return pl.pallas_call(
        _mm_two_kernel,
        grid=(n // tn, m // tm),
        in_specs=[pl.BlockSpec((tm, ka), lambda j, i: (i, 0)),
                  pl.BlockSpec((tm, kb), lambda j, i: (i, 0)),
                  pl.BlockSpec((ka, tn), lambda j, i: (0, j)),
                  pl.BlockSpec((kb, tn), lambda j, i: (0, j))],
        out_specs=pl.BlockSpec((tm, tn), lambda j, i: (i, j)),
        out_shape=jax.ShapeDtypeStruct((m, n), out_dtype),
        compiler_params=_cparams(2),
        name="matmul_two",
    )(xa, xb, wa, wb)


def _mm_scaled_kernel(x_ref, w_ref, cs_ref, o_ref):
    o_ref[...] = (_dot(x_ref[...], w_ref[...]) * cs_ref[...]).astype(o_ref.dtype)


def matmul_col_scaled(x, w, col_scale, out_dtype, tm, tn):
    m, k = x.shape
    _, n = w.shape
    return pl.pallas_call(
        _mm_scaled_kernel,
        grid=(n // tn, m // tm),
        in_specs=[pl.BlockSpec((tm, k), lambda j, i: (i, 0)),
                  pl.BlockSpec((k, tn), lambda j, i: (0, j)),
                  pl.BlockSpec((1, tn), lambda j, i: (0, j))],
        out_specs=pl.BlockSpec((tm, tn), lambda j, i: (i, j)),
        out_shape=jax.ShapeDtypeStruct((m, n), out_dtype),
        compiler_params=_cparams(2),
        name="matmul_col_scaled",
    )(x, w, col_scale)


def _gate_up_kernel(x_ref, wg_ref, wu_ref, o_ref, wg_bf_ref, wu_bf_ref):
    @pl.when(pl.program_id(1) == 0)
    def _():
        wg_bf_ref[...] = wg_ref[...].astype(jnp.bfloat16)
        wu_bf_ref[...] = wu_ref[...].astype(jnp.bfloat16)

    x = x_ref[...]
    g = _dot(x, wg_bf_ref[...])
    u = _dot(x, wu_bf_ref[...])
    o_ref[...] = (g * jax.nn.sigmoid(g) * u).astype(o_ref.dtype)


def gate_up(x, wg, wu, tm, tn):
    m, k = x.shape
    _, n = wg.shape
    wspec = pl.BlockSpec((k, tn), lambda j, i: (0, j))
    return pl.pallas_call(
        _gate_up_kernel,
        grid=(n // tn, m // tm),
        in_specs=[pl.BlockSpec((tm, k), lambda j, i: (i, 0)), wspec, wspec],
        out_specs=pl.BlockSpec((tm, tn), lambda j, i: (i, j)),
        out_shape=jax.ShapeDtypeStruct((m, n), jnp.bfloat16),
        scratch_shapes=[pltpu.VMEM((k, tn), jnp.bfloat16)] * 2,
        compiler_params=_cparams(2),
        name="gate_up",
    )(x, wg, wu)


def swiglu_ffn(xn, wg, wu, wd, tm_gu, tn_gu, tm_d, tn_d):
    a = gate_up(xn, wg, wu, tm_gu, tn_gu)
    return matmul(a, wd, jnp.float32, tm_d, tn_d)


def _bf16_split(x, n):
    pieces = []
    for _ in range(n - 1):
        p = x.astype(jnp.bfloat16)
        pieces.append(p)
        x = x - p.astype(jnp.float32)
    pieces.append(x.astype(jnp.bfloat16))
    return pieces


HEAD_SUM_PIECES = 2
CUMSUM_PIECES = 3


def _head_sum(x, w_bd):
    parts = []
    for s in range(x.shape[1] // LANES):
        pieces = _bf16_split(x[:, s * LANES:(s + 1) * LANES], HEAD_SUM_PIECES)
        parts.append(sum(_dot(p, w_bd) for p in pieces))
    return jnp.concatenate(parts, axis=1)


def _token_shift(p_ref, pp_ref, pn_ref, mup_ref, mun_ref, first, last):
    p = p_ref[...]
    rows = p.shape[0]
    rid = lax.broadcasted_iota(jnp.int32, p.shape, 0)
    prev_row = jnp.where(first, 0.0, pp_ref[SUBLANES - 1:SUBLANES, :])
    next_row = jnp.where(last, 0.0, pn_ref[0:1, :])
    p_prev = jnp.where(rid == 0, prev_row, pltpu.roll(p, 1, 0))
    p_next = jnp.where(rid == rows - 1, next_row, pltpu.roll(p, rows - 1, 0))
    return p + mup_ref[...] * (p_prev - p) + mun_ref[...] * (p_next - p)


def _rwkv_prep_kernel(
        p_ref, pp_ref, pn_ref, s_ref, sp_ref, sn_ref,
        mup_ref, mun_ref, mups_ref, muns_ref,
        w0f_ref, w0b_ref, a0f_ref, a0b_ref, kk_ref, ka_ref, rk_ref,
        w2f_ref, w2b_ref, a2f_ref, a2b_ref, g2_ref,
        atf_ref, btf_ref, ktf_ref, rtf_ref, glf_ref,
        atb_ref, btb_ref, ktb_ref, rtb_ref, glb_ref,
        v_ref, gate_ref, bonus_ref):
    i = pl.program_id(0)
    first = i == 0
    last = i == pl.num_programs(0) - 1
    tm = p_ref.shape[0]
    n_chunks = tm // CHUNK

    p = _token_shift(p_ref, pp_ref, pn_ref, mup_ref, mun_ref, first, last)
    s = _token_shift(s_ref, sp_ref, sn_ref, mups_ref, muns_ref, first, last)
    r = p[:, 0:D_RWKV]
    k = p[:, D_RWKV:2 * D_RWKV]
    v = p[:, 2 * D_RWKV:3 * D_RWKV]
    hw = jnp.tanh(s[:, 0:RANK_PAD]).astype(jnp.bfloat16)
    xa = s[:, RANK_PAD:2 * RANK_PAD].astype(jnp.bfloat16)
    sg = jax.nn.sigmoid(s[:, 2 * RANK_PAD:]).astype(jnp.bfloat16)

    li = lax.broadcasted_iota(jnp.int32, (LANES, LANES), 0) // RWKV_HEAD
    lj = lax.broadcasted_iota(jnp.int32, (LANES, LANES), 1) // RWKV_HEAD
    ones_bd = (li == lj).astype(jnp.bfloat16)

    kk = k * kk_ref[...]
    kk = kk / jnp.maximum(jnp.sqrt(_head_sum(kk * kk, ones_bd)), 1e-12)

    ti = lax.broadcasted_iota(jnp.int32, (tm, tm), 0)
    tj = lax.broadcasted_iota(jnp.int32, (tm, tm), 1)
    same_chunk = (ti // CHUNK) == (tj // CHUNK)

    k_sum = jnp.zeros_like(k)
    dirs = ((w0f_ref, w2f_ref, a0f_ref, a2f_ref, atf_ref, btf_ref, ktf_ref, rtf_ref, glf_ref, False),
            (w0b_ref, w2b_ref, a0b_ref, a2b_ref, atb_ref, btb_ref, ktb_ref, rtb_ref, glb_ref, True))
    for w0_ref, w2_ref, a0_ref, a2_ref, at_ref, bt_ref, kt_ref, rt_ref, gl_ref, rev in dirs:
        z = w0_ref[...] + _dot(hw, w2_ref[...])
        lw = -math.exp(-0.5) * jax.nn.sigmoid(z)
        a = jax.nn.sigmoid(a0_ref[...] + _dot(xa, a2_ref[...]))
        k_dir = k * (1.0 + (a - 1.0) * ka_ref[...])
        k_sum = k_sum + k_dir
        order = (tj >= ti) if rev else (tj <= ti)
        tri = jnp.where(same_chunk & order, 1.0, 0.0).astype(jnp.bfloat16)
        c = sum(_dot(tri, piece) for piece in _bf16_split(lw, CUMSUM_PIECES))
        e_neg = jnp.exp(-c)
        at_ref[...] = (-kk * jnp.exp(c - lw)).astype(at_ref.dtype)
        bt_ref[...] = (kk * a * e_neg).astype(bt_ref.dtype)
        kt_ref[...] = (k_dir * e_neg).astype(kt_ref.dtype)
        e_pos = jnp.exp(c)
        rt_ref[...] = (r * e_pos).astype(rt_ref.dtype)
        for j in range(n_chunks):
            end = j * CHUNK if rev else (j + 1) * CHUNK - 1
            gl_ref[j] = e_pos[end:end + 1, :]

    v_ref[...] = v.astype(v_ref.dtype)
    gate_ref[...] = _dot(sg, g2_ref[...]).astype(gate_ref.dtype)
    bonus_ref[...] = (_head_sum(r * k_sum * rk_ref[...], ones_bd) * v).astype(bonus_ref.dtype)


def rwkv_prep(proj, params, tm=128):
    t = proj.shape[0]
    nb = t // tm
    hb = tm // SUBLANES
    n_halo = t // SUBLANES
    wide = 3 * D_RWKV
    small_blk = wide // SMALL_COLS

    def row(c, cb=0):
        return pl.BlockSpec((tm, c), lambda i: (i, cb))

    def halo_prev(c, cb=0):
        return pl.BlockSpec((SUBLANES, c), lambda i: (jnp.maximum(i * hb - 1, 0), cb))

    def halo_next(c, cb=0):
        return pl.BlockSpec((SUBLANES, c), lambda i: (jnp.minimum((i + 1) * hb, n_halo - 1), cb))

    def vec(c):
        return pl.BlockSpec((1, c), lambda i: (0, 0))

    def mat(r_, c):
        return pl.BlockSpec((r_, c), lambda i: (0, 0))

    gl_spec = pl.BlockSpec((tm // CHUNK, 1, D_RWKV), lambda i: (i, 0, 0))
    big = jax.ShapeDtypeStruct((t, D_RWKV), jnp.float32)
    gl = jax.ShapeDtypeStruct((t // CHUNK, 1, D_RWKV), jnp.float32)
    in_specs = [row(wide), halo_prev(wide), halo_next(wide),
                row(SMALL_COLS, small_blk), halo_prev(SMALL_COLS, small_blk), halo_next(SMALL_COLS, small_blk),
                vec(wide), vec(wide), vec(SMALL_COLS), vec(SMALL_COLS)]
    in_specs += [vec(D_RWKV)] * 7
    in_specs += [mat(RANK_PAD, D_RWKV)] * 4 + [mat(GATE_RANK, D_RWKV)]
    dir_specs = [row(D_RWKV)] * 4 + [gl_spec]
    big_bf = jax.ShapeDtypeStruct((t, D_RWKV), jnp.bfloat16)
    dir_shapes = [big_bf] * 4 + [gl]
    return pl.pallas_call(
        _rwkv_prep_kernel,
        grid=(nb,),
        in_specs=in_specs,
        out_specs=dir_specs + dir_specs + [row(D_RWKV)] * 3,
        out_shape=dir_shapes + dir_shapes + [big_bf, big_bf, big_bf],
        compiler_params=_cparams(1),
        name="rwkv_prep",
    )(proj, proj, proj, proj, proj, proj, *params)


def _stack(x, lane_head):
    zero = jnp.zeros_like(x)
    return jnp.concatenate([jnp.where(lane_head == 0, x, zero),
                            jnp.where(lane_head == 1, x, zero)], axis=0)


def _scan_masks(rev):
    n2 = 2 * CHUNK
    ri = lax.broadcasted_iota(jnp.int32, (n2, n2), 0)
    ci = lax.broadcasted_iota(jnp.int32, (n2, n2), 1)
    same = (ri // CHUNK) == (ci // CHUNK)
    before = (ci > ri) if rev else (ci < ri)
    strict = same & before
    incl = same & (before | (ri == ci))
    eye = jnp.where(ri == ci, 1.0, 0.0)
    return strict, jnp.concatenate([incl, incl], axis=1), eye


def _scan_prepare(units):
    bf = jnp.bfloat16
    n2 = 2 * CHUNK
    lane_head = lax.broadcasted_iota(jnp.int32, (CHUNK, PAIR), 1) // RWKV_HEAD
    stacked = [[_stack(x, lane_head) for x in u[:5]] for u in units]
    bks = [jnp.concatenate([s[1], s[2]], axis=0) for s in stacked]
    g_as = [_dot(s[0], bk, _NT) for s, bk in zip(stacked, bks)]
    g_rs = [_dot(s[3], bk, _NT) for s, bk in zip(stacked, bks)]
    xs = [jnp.where(u[6][0], g[:, :n2], 0.0) for u, g in zip(units, g_as)]
    a_aks = [jnp.where(u[6][0], g[:, n2:], 0.0).astype(bf) for u, g in zip(units, g_as)]
    a_rs = [jnp.where(u[6][1], g, 0.0).astype(bf) for u, g in zip(units, g_rs)]
    akv = [_dot(a, s[4]).astype(bf) for a, s in zip(a_aks, stacked)]

    tinvs = [u[6][2] + x for u, x in zip(units, xs)]
    for _ in range(int(math.log2(CHUNK)) - 1):
        xbs = [x.astype(bf) for x in xs]
        xs = [_dot(xb, xb) for xb in xbs]
        tinvs = [t + _dot(t.astype(bf), x.astype(bf)) for t, x in zip(tinvs, xs)]

    wqs = [_dot(t.astype(bf), jnp.concatenate([s[0], kv], axis=1))
           for t, s, kv in zip(tinvs, stacked, akv)]
    out = []
    for u, s, bk, wq, a_r in zip(units, stacked, bks, wqs, a_rs):
        gl = u[5]
        w = wq[:, :PAIR].astype(bf)
        q = wq[:, PAIR:]
        bkg = (bk * gl).astype(bf)
        m = _dot(w, bkg[:n2], _TN)
        n = _dot(jnp.concatenate([q.astype(bf), s[4]], axis=0), bkg, _TN)
        out.append((w, q, s[3], a_r, s[4], m.astype(bf), n, gl))
    return out


def _rwkv_scan_kernel(atf_ref, btf_ref, ktf_ref, rtf_ref, glf_ref, vf_ref,
                      atb_ref, btb_ref, ktb_ref, rtb_ref, glb_ref, vb_ref,
                      yf_ref, yb_ref, sf_ref, sb_ref, *, n_blk, n_par):
    @pl.when(pl.program_id(1) == 0)
    def _():
        sf_ref[...] = jnp.zeros_like(sf_ref)
        sb_ref[...] = jnp.zeros_like(sb_ref)

    def tile(j, p):
        return pl.ds(j * CHUNK, CHUNK), pl.ds(p * PAIR, PAIR)

    masks_f = _scan_masks(False)
    masks_b = _scan_masks(True)
    units = []
    for p in range(n_par):
        for j in range(n_blk):
            units.append(tuple(ref[tile(j, p)] for ref in (atf_ref, btf_ref, ktf_ref, rtf_ref, vf_ref))
                         + (glf_ref[j, :, pl.ds(p * PAIR, PAIR)], masks_f))
            units.append(tuple(ref[tile(j, p)] for ref in (atb_ref, btb_ref, ktb_ref, rtb_ref, vb_ref))
                         + (glb_ref[j, :, pl.ds(p * PAIR, PAIR)], masks_b))
    prep = _scan_prepare(units)

    sf = [sf_ref[p] for p in range(n_par)]
    sb = [sb_ref[p] for p in range(n_par)]
    pending = []
    for j in range(n_blk):
        jb = n_blk - 1 - j
        for p in range(n_par):
            pf = prep[2 * (p * n_blk + j)]
            pb = prep[2 * (p * n_blk + jb) + 1]
            sfb = sf[p].astype(jnp.bfloat16)
            sbb = sb[p].astype(jnp.bfloat16)
            sf[p] = sf[p] * pf[7] + _dot(sfb, pf[5]) + pf[6]
            sb[p] = sb[p] * pb[7] + _dot(sbb, pb[5]) + pb[6]
            pending.append((pf, sfb, yf_ref, tile(j, p)))
            pending.append((pb, sbb, yb_ref, tile(jb, p)))
    for p in range(n_par):
        sf_ref[p] = sf[p]
        sb_ref[p] = sb[p]
    us = [(_dot(u[0], s0b, _NT) + u[1]).astype(jnp.bfloat16) for u, s0b, _, _ in pending]
    for (u, s0b, y_ref, where), uu in zip(pending, us):
        y2 = _dot(u[2], s0b, _NT) + _dot(u[3], jnp.concatenate([uu, u[4]], axis=0))
        y_ref[where] = (y2[:CHUNK] + y2[CHUNK:]).astype(y_ref.dtype)


def rwkv_scan(fwd, bwd, v, n_blk=4, n_par=4):
    t = v.shape[0]
    nb = t // (CHUNK * n_blk)
    n_grp = D_RWKV // (PAIR * n_par)
    width = PAIR * n_par
    f_blk = pl.BlockSpec((CHUNK * n_blk, width), lambda p, c: (c, p))
    b_blk = pl.BlockSpec((CHUNK * n_blk, width), lambda p, c: (nb - 1 - c, p))
    f_gl = pl.BlockSpec((n_blk, 1, width), lambda p, c: (c, 0, p))
    b_gl = pl.BlockSpec((n_blk, 1, width), lambda p, c: (nb - 1 - c, 0, p))
    out = jax.ShapeDtypeStruct((t, D_RWKV), jnp.bfloat16)
    return pl.pallas_call(
        functools.partial(_rwkv_scan_kernel, n_blk=n_blk, n_par=n_par),
        grid=(n_grp, nb),
        in_specs=[f_blk] * 4 + [f_gl, f_blk] + [b_blk] * 4 + [b_gl, b_blk],
        out_specs=[f_blk, b_blk],
        out_shape=[out, out],
        scratch_shapes=[pltpu.VMEM((n_par, PAIR, PAIR), jnp.float32)] * 2,
        compiler_params=_cparams(2),
        name="rwkv_scan",
    )(*fwd, v, *bwd, v)


def _rwkv_post_kernel(yf_ref, yb_ref, bonus_ref, gate_ref, gw_ref, gb_ref, o_ref):
    li = lax.broadcasted_iota(jnp.int32, (LANES, LANES), 0) // RWKV_HEAD
    lj = lax.broadcasted_iota(jnp.int32, (LANES, LANES), 1) // RWKV_HEAD
    mean_bd = jnp.where(li == lj, 1.0 / RWKV_HEAD, 0.0).astype(jnp.bfloat16)
    y = yf_ref[...].astype(jnp.float32) + yb_ref[...].astype(jnp.float32)
    d = y - _head_sum(y, mean_bd)
    var = _head_sum(d * d, mean_bd)
    yn = d * lax.rsqrt(var + RWKV_GN_EPS) * gw_ref[...] + gb_ref[...]
    o_ref[...] = ((yn + bonus_ref[...].astype(jnp.float32))
                  * gate_ref[...].astype(jnp.float32)).astype(o_ref.dtype)


def rwkv_post(yf, yb, bonus, gate, gn_w, gn_b, tm=256):
    t = yf.shape[0]
    row = pl.BlockSpec((tm, D_RWKV), lambda i: (i, 0))
    vec = pl.BlockSpec((1, D_RWKV), lambda i: (0, 0))
    return pl.pallas_call(
        _rwkv_post_kernel,
        grid=(t // tm,),
        in_specs=[row, row, row, row, vec, vec],
        out_specs=row,
        out_shape=jax.ShapeDtypeStruct((t, D_RWKV), jnp.bfloat16),
        compiler_params=_cparams(1),
        name="rwkv_post",
    )(yf, yb, bonus, gate, gn_w.reshape(1, -1), gn_b.reshape(1, -1))


SLOPE_PIECES = 3
SC_PER_HEAD = 2 + SLOPE_PIECES
ONES_ROWS = 16
V_ROWS = DIFF_HEAD + ONES_ROWS
UNDERFLOW_LOG2 = 160.0
BOUND_SLACK = 1.01


def _diff_attn_kernel(sc_ref, qt_ref, k_ref, vt_ref, g_ref, o_ref, s_ref, kmax_ref, *, tq, tk, lam_init):
    bf = jnp.bfloat16
    h = pl.program_id(0)
    qi = pl.program_id(1)
    n_kv = k_ref.shape[0] // tk
    lam = sc_ref[0]
    base = 1 + h * SC_PER_HEAD
    slope = sc_ref[base]
    pieces = [sc_ref[base + 1 + i] for i in range(SLOPE_PIECES)]
    inv_slope = sc_ref[base + 1 + SLOPE_PIECES]

    @pl.when(qi == 0)
    def _():
        li = lax.broadcasted_iota(jnp.int32, (LANES, LANES), 0) // DIFF_D
        lj = lax.broadcasted_iota(jnp.int32, (LANES, LANES), 1) // DIFF_D
        half_ones = (li == lj).astype(bf)

        def tile_max(j, best):
            kf = k_ref[pl.ds(pl.multiple_of(j * tk, tk), tk), :].astype(jnp.float32)
            sq = sum(_dot(p, half_ones) for p in _bf16_split(kf * kf, HEAD_SUM_PIECES))
            return jnp.maximum(best, jnp.max(sq, axis=0, keepdims=True))

        best = lax.fori_loop(0, n_kv, tile_max, jnp.zeros((1, LANES), jnp.float32))
        kmax_ref[0] = jnp.max(best)

    qt = qt_ref[...]
    q_sq = jnp.square(qt.astype(jnp.float32))
    q_n2 = jnp.maximum(jnp.sum(q_sq[:DIFF_D], axis=0, keepdims=True),
                       jnp.sum(q_sq[DIFF_D:], axis=0, keepdims=True))
    bound = jnp.max(jnp.sqrt(q_n2 * kmax_ref[0]))
    reach = (2.0 * BOUND_SLACK * bound + UNDERFLOW_LOG2) * inv_slope
    width = jnp.int32(1)
    for d in range(1, n_kv):
        width = width + jnp.where(reach < float(d * tk), 0, 1).astype(jnp.int32)
    qrow = lax.broadcasted_iota(jnp.int32, (DIFF_HEAD, tq), 0)
    feat = jnp.zeros((DIFF_HEAD, tq), jnp.float32)
    for i, pc in enumerate(pieces):
        feat = jnp.where(qrow % DIFF_D == i, 2.0 * pc, feat)
        feat = jnp.where(qrow % DIFF_D == SLOPE_PIECES + i, pc, feat)
    feat = feat.astype(bf)
    own_rows = [qrow < DIFF_D, qrow >= DIFF_D]
    qts = [jnp.where(own, qt, feat) for own in own_rows]

    kl = lax.broadcasted_iota(jnp.int32, (tk, LANES), 1)
    kr = lax.broadcasted_iota(jnp.int32, (tk, LANES), 0)
    kfeat = jnp.where(kl % DIFF_D < SLOPE_PIECES, kr >> 1,
                      jnp.where(kl % DIFF_D < 2 * SLOPE_PIECES, kr & 1, 0)).astype(jnp.float32)
    kfeat_before = kfeat.astype(bf)
    kfeat_after = (-kfeat).astype(bf)
    kzero = jnp.zeros((tk, LANES), bf)
    own_lanes = [kl < DIFF_D, kl >= DIFF_D]

    qpos = qi * tq + lax.broadcasted_iota(jnp.int32, (1, tq), 1)
    q_bias = slope * qpos.astype(jnp.float32)

    def online(carry, s, t_q, vt):
        m, acc = carry
        m_new = jnp.maximum(m, jnp.max(s, axis=0, keepdims=True) + t_q)
        alpha = jnp.exp2(m - m_new)
        p = jnp.exp2(s - (m_new - t_q))
        return m_new, alpha * acc + _dot(vt, p.astype(bf))

    n_diag = max(tq // tk, 1)
    jd = (qi * tq) // tk
    lo = jnp.maximum(jd - width, 0)
    hi = jnp.minimum(jd + n_diag - 1 + width, n_kv - 1)
    n_off = hi - lo + 1 - n_diag

    def tile_of(n):
        j = lo + n
        return j + n_diag * (j >= jd).astype(jnp.int32)

    def scores(n, slot, mp):
        j = tile_of(n)
        start = pl.multiple_of(j * tk, tk)
        kb = k_ref[pl.ds(start, tk), :]
        kf = jnp.where(j < jd, kfeat_before, kfeat_after)
        s_ref[slot, mp] = _dot(jnp.where(own_lanes[mp], kb, kf), qts[mp])

    def consume(n, slot, mp, carry):
        j = tile_of(n)
        start = pl.multiple_of(j * tk, tk)
        vt = vt_ref[:, pl.ds(start, tk)]
        sign = jnp.where(j < jd, 1.0, -1.0)
        t_q = sign * (slope * (j * tk).astype(jnp.float32) - q_bias)
        return online(carry, s_ref[slot, mp], t_q, vt)

    def step(n_next, n_cur, slot_next, slot_cur, carries):
        out = []
        for mp in range(2):
            scores(n_next, slot_next, mp)
            out.append(consume(n_cur, slot_cur, mp, carries[mp]))
        return tuple(out)

    def diag_scores(d, slot):
        kb_d = k_ref[pl.ds(pl.multiple_of((jd + d) * tk, tk), tk), :]
        kpos = (jd + d) * tk + lax.broadcasted_iota(jnp.int32, (tk, tq), 0)
        bias = slope * jnp.abs(kpos - qpos).astype(jnp.float32)
        for mp, (own, q_m) in enumerate(zip(own_lanes, qts)):
            s_ref[slot, mp] = _dot(jnp.where(own, kb_d, kzero), q_m) - bias

    def diag_consume(d, slot, mp, carry):
        vt_d = vt_ref[:, pl.ds(pl.multiple_of((jd + d) * tk, tk), tk)]
        return online(carry, s_ref[slot, mp], 0.0, vt_d)

    init = (jnp.full((1, tq), -jnp.inf, jnp.float32), jnp.zeros((V_ROWS, tq), jnp.float32))
    carries = [init, init]
    diag_slots = [1] if n_diag == 1 else [0, 1]
    for d, slot in enumerate(diag_slots[:-1]):
        diag_scores(d, slot)
    diag_scores(n_diag - 1, 1)
    for d, slot in enumerate(diag_slots[:-1]):
        carries = [diag_consume(d, slot, mp, carries[mp]) for mp in range(2)]
    for mp in range(2):
        scores(jnp.int32(0), 0, mp)
        carries[mp] = diag_consume(n_diag - 1, 1, mp, carries[mp])

    def body(i, carries):
        n = 2 * i
        carries = step(n + 1, n, 1, 0, carries)
        return step(n + 2, n + 1, 0, 1, carries)

    n_pairs = (n_off - 1) // 2
    carries = lax.fori_loop(0, n_pairs, body, tuple(carries))
    last = n_off - 1

    def tail_two(carries):
        carries = step(last, last - 1, 1, 0, carries)
        return tuple(consume(last, 1, mp, carries[mp]) for mp in range(2))

    def tail_one(carries):
        return tuple(consume(last, 0, mp, carries[mp]) for mp in range(2))

    (_, acc0), (_, acc1) = lax.cond(last == 2 * n_pairs + 1, tail_two, tail_one, carries)
    o0 = acc0[:DIFF_HEAD] / acc0[DIFF_HEAD:DIFF_HEAD + 1]
    o1 = acc1[:DIFF_HEAD] / acc1[DIFF_HEAD:DIFF_HEAD + 1]
    o = (o0 - lam * o1).T
    o_ref[...] = (_rms(o, g_ref[...], SUBLN_EPS) * (1.0 - lam_init)).astype(o_ref.dtype)


def _bf16_pieces(x, n):
    out = []
    for _ in range(n):
        p = (x.view(np.uint32) & np.uint32(0xFFFF0000)).view(np.float32)
        out.append(p)
        x = (x - p).astype(np.float32)
    return out


def diff_attention(qt, k, vt, lam, subln_g, lam_init, tq=1024, tk=512):
    t = k.shape[0]
    assert tq // tk in (1, 2) and tq % tk == 0 and tk // 2 <= 256
    assert t // tk > tq // tk
    slopes = 2.0 ** (-ALIBI_MAX_EXP * np.arange(1, N_DIFF_HEADS + 1, dtype=np.float64) / N_DIFF_HEADS)
    slopes = (slopes * math.log2(math.e)).astype(np.float32)
    pieces = _bf16_pieces(slopes, SLOPE_PIECES)
    slope_used = sum(pieces)
    per_head = np.stack([slope_used] + pieces + [(1.0 / slope_used).astype(np.float32)], axis=1)
    lam = jnp.concatenate([lam, jnp.asarray(per_head.reshape(-1))])
    return pl.pallas_call(
        functools.partial(_diff_attn_kernel, tq=tq, tk=tk, lam_init=lam_init),
        grid=(N_DIFF_HEADS, t // tq),
        in_specs=[pl.BlockSpec(memory_space=pltpu.SMEM),
                  pl.BlockSpec((DIFF_HEAD, tq), lambda h, i: (h, i)),
                  pl.BlockSpec((t, DIFF_HEAD), lambda h, i: (0, h)),
                  pl.BlockSpec((V_ROWS, t), lambda h, i: (h, 0)),
                  pl.BlockSpec((1, DIFF_HEAD), lambda h, i: (0, 0))],
        out_specs=pl.BlockSpec((tq, DIFF_HEAD), lambda h, i: (i, h)),
        out_shape=jax.ShapeDtypeStruct((t, D_DIFF), jnp.bfloat16),
        scratch_shapes=[pltpu.VMEM((2, 2, tk, tq), jnp.float32),
                        pltpu.SMEM((1,), jnp.float32)],
        compiler_params=_cparams(2),
        name="diff_attn",
    )(lam, qt, k, vt, subln_g.reshape(1, DIFF_HEAD))


def _pad_cols(w, to):
    return jnp.pad(w, ((0, 0), (0, to - w.shape[1])))


def _layer(h, l, lam_init, x_norm, prm):
    bf = jnp.bfloat16
    f32 = jnp.float32
    f1 = swiglu_ffn(x_norm, prm["ffn1_w_gate"][l], prm["ffn1_w_up"][l],
                    prm["ffn1_w_down"][l].astype(bf), 1024, 256, 512, 512)
    h, xn = resid_norm(h, f1, prm["ffn1_post_g"][l], prm["mix_pre_g"][l], FFN_RESIDUAL, bf)

    w_in = prm["w_in"][l]
    c_rkv = 3 * D_RWKV
    c_w = c_rkv + DECAY_RANK
    c_a = c_w + ICLR_RANK
    c_g = c_a + GATE_RANK
    w_small = jnp.concatenate([_pad_cols(w_in[:, c_rkv:c_w], RANK_PAD),
                               _pad_cols(w_in[:, c_w:c_a], RANK_PAD),
                               w_in[:, c_a:c_g]], axis=1)
    w_rwkv = jnp.concatenate([w_in[:, :c_rkv], w_small], axis=1).astype(bf)
    proj = matmul(xn, w_rwkv, f32, 1024, 512)
    q_scale = DIFF_D ** -0.5 * math.log2(math.e)
    col_scale = jnp.concatenate([jnp.full((1, D_DIFF), q_scale, f32), jnp.ones((1, 2 * D_DIFF), f32)], axis=1)
    qkv = matmul_col_scaled(xn, w_in[:, c_g:].astype(bf), col_scale, bf, 1024, 512)

    def small_vec(a):
        return jnp.concatenate([_pad_cols(a[None, c_rkv:c_w], RANK_PAD),
                                _pad_cols(a[None, c_w:c_a], RANK_PAD),
                                a[None, c_a:c_g]], axis=1)

    def pad_rows(w):
        return jnp.pad(w, ((0, RANK_PAD - w.shape[0]), (0, 0))).astype(bf)

    mu_p, mu_n = prm["mu_prev"][l], prm["mu_next"][l]
    vecs = [prm[n][l].reshape(1, D_RWKV) for n in ("w0_f", "w0_b", "a0_f", "a0_b", "k_k", "k_a", "r_k")]
    mats = [pad_rows(prm["w2_f"][l]), pad_rows(prm["w2_b"][l]),
            pad_rows(prm["a2_f"][l]), pad_rows(prm["a2_b"][l]), prm["g2"][l].astype(bf)]
    prep = rwkv_prep(proj, [mu_p[None, :c_rkv], mu_n[None, :c_rkv], small_vec(mu_p), small_vec(mu_n)]
                     + vecs + mats)
    fwd, bwd, (v_r, gate, bonus) = prep[0:5], prep[5:10], prep[10:13]
    yf, yb = rwkv_scan(fwd, bwd, v_r)
    y_a = rwkv_post(yf, yb, bonus, gate, prm["gn_w"][l], prm["gn_b"][l])

    lam = (jnp.exp(jnp.sum(prm["lq1"][l] * prm["lk1"][l]))
           - jnp.exp(jnp.sum(prm["lq2"][l] * prm["lk2"][l])) + lam_init).reshape(1)
    t = qkv.shape[0]
    vt = qkv[:, 2 * D_DIFF:].T.reshape(N_DIFF_HEADS, DIFF_HEAD, t)
    vt = jnp.concatenate([vt, jnp.ones((N_DIFF_HEADS, ONES_ROWS, t), bf)], axis=1).reshape(N_DIFF_HEADS * V_ROWS, t)
    y_b = diff_attention(qkv[:, :D_DIFF].T, qkv[:, D_DIFF:2 * D_DIFF], vt, lam, prm["subln_g"][l], lam_init)

    w_out = prm["w_out"][l]
    mix = matmul_two(y_a, y_b, w_out[:D_RWKV].astype(bf), w_out[D_RWKV:].astype(bf), f32, 1024, 512)
    h, xn = resid_norm(h, mix, prm["mix_post_g"][l], prm["ffn2_pre_g"][l], 1.0, bf)

    f2 = swiglu_ffn(xn, prm["ffn2_w_gate"][l], prm["ffn2_w_up"][l],
                    prm["ffn2_w_down"][l].astype(bf), 1024, 256, 512, 512)
    return resid_norm_out(h, f2, prm["ffn2_post_g"][l], prm["final_g"][l], FFN_RESIDUAL)


def kernel(x, ffn1_pre_g, ffn1_w_gate, ffn1_w_up, ffn1_w_down, ffn1_post_g, mix_pre_g, w_in, mu_prev, mu_next, w0_f, w2_f, w0_b, w2_b, a0_f, a2_f, a0_b, a2_b, g2, k_k, k_a, r_k, gn_w, gn_b, lq1, lk1, lq2, lk2, subln_g, w_out, mix_post_g, ffn2_pre_g, ffn2_w_gate, ffn2_w_up, ffn2_w_down, ffn2_post_g, final_g):
    prm = dict(locals())
    bsz, t, d = x.shape
    assert bsz == 1
    depth = ffn1_pre_g.shape[0]
    h = x.reshape(t, d)
    for l in range(depth):
        lam_init = 0.8 - 0.6 * math.exp(-0.3 * l)
        x_norm = rms_norm_cast(h, ffn1_pre_g[l], jnp.bfloat16)
        h = _layer(h, l, lam_init, x_norm, prm)
    return h.reshape(bsz, t, d)
```

```python
import functools
import math

import jax
import jax.numpy as jnp
import numpy as np
from jax import lax
from jax.experimental import pallas as pl
from jax.experimental.pallas import tpu as pltpu

D_MODEL = 4096
D_RWKV = 2048
D_DIFF = 2048
RWKV_HEAD = 64
DECAY_RANK = 96
ICLR_RANK = 96
GATE_RANK = 256
DIFF_D = 64
DIFF_HEAD = 128
N_DIFF_HEADS = 16
ALIBI_MAX_EXP = 8.0
NORM_EPS = 1e-6
RWKV_GN_EPS = 64e-5
SUBLN_EPS = 1e-5
FFN_RESIDUAL = 0.5

LANES = 128
SUBLANES = 8
VMEM_LIMIT = 60 * 1024 * 1024

RANK_PAD = 128
SMALL_COLS = 2 * RANK_PAD + GATE_RANK
CHUNK = 64
PAIR = 2 * RWKV_HEAD


def _cparams(n_axes):
    return pltpu.CompilerParams(
        dimension_semantics=("arbitrary",) * n_axes, vmem_limit_bytes=VMEM_LIMIT)


def _dot(a, b, dims=(((1,), (0,)), ((), ()))):
    return lax.dot_general(a, b, dims, preferred_element_type=jnp.float32)


_NT = (((1,), (1,)), ((), ()))
_TN = (((0,), (0,)), ((), ()))


def _rms(x, g, eps):
    return x * lax.rsqrt(jnp.mean(x * x, axis=-1, keepdims=True) + eps) * g


def _norm_kernel(x_ref, g_ref, o_ref):
    o_ref[...] = _rms(x_ref[...], g_ref[...], NORM_EPS).astype(o_ref.dtype)


def rms_norm_cast(x, g, out_dtype, tm=256):
    m, d = x.shape
    return pl.pallas_call(
        _norm_kernel,
        grid=(m // tm,),
        in_specs=[pl.BlockSpec((tm, d), lambda i: (i, 0)),
                  pl.BlockSpec((1, d), lambda i: (0, 0))],
        out_specs=pl.BlockSpec((tm, d), lambda i: (i, 0)),
        out_shape=jax.ShapeDtypeStruct((m, d), out_dtype),
        compiler_params=_cparams(1),
        name="rms_norm",
    )(x, g.reshape(1, d))


def _resid_norm_kernel(h_ref, f_ref, gp_ref, gn_ref, h_out_ref, n_out_ref, *, scale):
    h = h_ref[...] + scale * _rms(f_ref[...], gp_ref[...], NORM_EPS)
    h_out_ref[...] = h
    n_out_ref[...] = _rms(h, gn_ref[...], NORM_EPS).astype(n_out_ref.dtype)


def resid_norm(h, f, g_post, g_next, scale, out_dtype, tm=256):
    m, d = h.shape
    row = pl.BlockSpec((tm, d), lambda i: (i, 0))
    vec = pl.BlockSpec((1, d), lambda i: (0, 0))
    return pl.pallas_call(
        functools.partial(_resid_norm_kernel, scale=scale),
        grid=(m // tm,),
        in_specs=[row, row, vec, vec],
        out_specs=[row, row],
        out_shape=[jax.ShapeDtypeStruct((m, d), jnp.float32),
                   jax.ShapeDtypeStruct((m, d), out_dtype)],
        compiler_params=_cparams(1),
        name="resid_norm",
    )(h, f, g_post.reshape(1, d), g_next.reshape(1, d))


def _resid_norm_out_kernel(h_ref, f_ref, gp_ref, gn_ref, n_out_ref, *, scale):
    h = h_ref[...] + scale * _rms(f_ref[...], gp_ref[...], NORM_EPS)
    n_out_ref[...] = _rms(h, gn_ref[...], NORM_EPS).astype(n_out_ref.dtype)


def resid_norm_out(h, f, g_post, g_next, scale, tm=256):
    m, d = h.shape
    row = pl.BlockSpec((tm, d), lambda i: (i, 0))
    vec = pl.BlockSpec((1, d), lambda i: (0, 0))
    return pl.pallas_call(
        functools.partial(_resid_norm_out_kernel, scale=scale),
        grid=(m // tm,),
        in_specs=[row, row, vec, vec],
        out_specs=row,
        out_shape=jax.ShapeDtypeStruct((m, d), jnp.float32),
        compiler_params=_cparams(1),
        name="resid_norm_out",
    )(h, f, g_post.reshape(1, d), g_next.reshape(1, d))


def _mm_kernel(x_ref, w_ref, o_ref):
    o_ref[...] = _dot(x_ref[...], w_ref[...]).astype(o_ref.dtype)


def matmul(x, w, out_dtype, tm, tn):
    m, k = x.shape
    _, n = w.shape
    return pl.pallas_call(
        _mm_kernel,
        grid=(n // tn, m // tm),
        in_specs=[pl.BlockSpec((tm, k), lambda j, i: (i, 0)),
                  pl.BlockSpec((k, tn), lambda j, i: (0, j))],
        out_specs=pl.BlockSpec((tm, tn), lambda j, i: (i, j)),
        out_shape=jax.ShapeDtypeStruct((m, n), out_dtype),
        compiler_params=_cparams(2),
        name="matmul",
    )(x, w)


def _mm_two_kernel(xa_ref, xb_ref, wa_ref, wb_ref, o_ref):
    o_ref[...] = (_dot(xa_ref[...], wa_ref[...]) + _dot(xb_ref[...], wb_ref[...])).astype(o_ref.dtype)


def matmul_two(xa, xb, wa, wb, out_dtype, tm, tn):
    m, ka = xa.shape
    _, kb = xb.shape
    _, n = wa.shape
    return pl.pallas_call(
        _mm_two_kernel,
        grid=(n // tn, m // tm),
        in_specs=[pl.BlockSpec((tm, ka), lambda j, i: (i, 0)),
                  pl.BlockSpec((tm, kb), lambda j, i: (i, 0)),
                  pl.BlockSpec((ka, tn), lambda j, i: (0, j)),
                  pl.BlockSpec((kb, tn), lambda j, i: (0, j))],
        out_specs=pl.BlockSpec((tm, tn), lambda j, i: (i, j)),
        out_shape=jax.ShapeDtypeStruct((m, n), out_dtype),
        compiler_params=_cparams(2),
        name="matmul_two",
    )(xa, xb, wa, wb)


def _mm_scaled_kernel(x_ref, w_ref, cs_ref, o_ref):
    o_ref[...] = (_dot(x_ref[...], w_ref[...]) * cs_ref[...]).astype(o_ref.dtype)


def matmul_col_scaled(x, w, col_scale, out_dtype, tm, tn):
    m, k = x.shape
    _, n = w.shape
    return pl.pallas_call(
        _mm_scaled_kernel,
        grid=(n // tn, m // tm),
        in_specs=[pl.BlockSpec((tm, k), lambda j, i: (i, 0)),
                  pl.BlockSpec((k, tn), lambda j, i: (0, j)),
                  pl.BlockSpec((1, tn), lambda j, i: (0, j))],
        out_specs=pl.BlockSpec((tm, tn), lambda j, i: (i, j)),
        out_shape=jax.ShapeDtypeStruct((m, n), out_dtype),
        compiler_params=_cparams(2),
        name="matmul_col_scaled",
    )(x, w, col_scale)


def _gate_up_kernel(x_ref, wg_ref, wu_ref, o_ref, wg_bf_ref, wu_bf_ref):
    @pl.when(pl.program_id(1) == 0)
    def _():
        wg_bf_ref[...] = wg_ref[...].astype(jnp.bfloat16)
        wu_bf_ref[...] = wu_ref[...].astype(jnp.bfloat16)

    x = x_ref[...]
    g = _dot(x, wg_bf_ref[...])
    u = _dot(x, wu_bf_ref[...])
    o_ref[...] = (g * jax.nn.sigmoid(g) * u).astype(o_ref.dtype)


def gate_up(x, wg, wu, tm, tn):
    m, k = x.shape
    _, n = wg.shape
    wspec = pl.BlockSpec((k, tn), lambda j, i: (0, j))
    return pl.pallas_call(
        _gate_up_kernel,
        grid=(n // tn, m // tm),
        in_specs=[pl.BlockSpec((tm, k), lambda j, i: (i, 0)), wspec, wspec],
        out_specs=pl.BlockSpec((tm, tn), lambda j, i: (i, j)),
        out_shape=jax.ShapeDtypeStruct((m, n), jnp.bfloat16),
        scratch_shapes=[pltpu.VMEM((k, tn), jnp.bfloat16)] * 2,
        compiler_params=_cparams(2),
        name="gate_up",
    )(x, wg, wu)


def swiglu_ffn(xn, wg, wu, wd, tm_gu, tn_gu, tm_d, tn_d):
    a = gate_up(xn, wg, wu, tm_gu, tn_gu)
    return matmul(a, wd, jnp.float32, tm_d, tn_d)


def _bf16_split(x, n):
    pieces = []
    for _ in range(n - 1):
        p = x.astype(jnp.bfloat16)
        pieces.append(p)
        x = x - p.astype(jnp.float32)
    pieces.append(x.astype(jnp.bfloat16))
    return pieces


HEAD_SUM_PIECES = 2
CUMSUM_PIECES = 3


def _head_sum(x, w_bd):
    parts = []
    for s in range(x.shape[1] // LANES):
        pieces = _bf16_split(x[:, s * LANES:(s + 1) * LANES], HEAD_SUM_PIECES)
        parts.append(sum(_dot(p, w_bd) for p in pieces))
    return jnp.concatenate(parts, axis=1)


def _token_shift(p_ref, pp_ref, pn_ref, mup_ref, mun_ref, first, last):
    p = p_ref[...]
    rows = p.shape[0]
    rid = lax.broadcasted_iota(jnp.int32, p.shape, 0)
    prev_row = jnp.where(first, 0.0, pp_ref[SUBLANES - 1:SUBLANES, :])
    next_row = jnp.where(last, 0.0, pn_ref[0:1, :])
    p_prev = jnp.where(rid == 0, prev_row, pltpu.roll(p, 1, 0))
    p_next = jnp.where(rid == rows - 1, next_row, pltpu.roll(p, rows - 1, 0))
    return p + mup_ref[...] * (p_prev - p) + mun_ref[...] * (p_next - p)


def _rwkv_prep_kernel(
        p_ref, pp_ref, pn_ref, s_ref, sp_ref, sn_ref,
        mup_ref, mun_ref, mups_ref, muns_ref,
        w0f_ref, w0b_ref, a0f_ref, a0b_ref, kk_ref, ka_ref, rk_ref,
        w2f_ref, w2b_ref, a2f_ref, a2b_ref, g2_ref,
        atf_ref, btf_ref, ktf_ref, rtf_ref, glf_ref,
        atb_ref, btb_ref, ktb_ref, rtb_ref, glb_ref,
        v_ref, gate_ref, bonus_ref):
    i = pl.program_id(0)
    first = i == 0
    last = i == pl.num_programs(0) - 1
    tm = p_ref.shape[0]
    n_chunks = tm // CHUNK

    p = _token_shift(p_ref, pp_ref, pn_ref, mup_ref, mun_ref, first, last)
    s = _token_shift(s_ref, sp_ref, sn_ref, mups_ref, muns_ref, first, last)
    r = p[:, 0:D_RWKV]
    k = p[:, D_RWKV:2 * D_RWKV]
    v = p[:, 2 * D_RWKV:3 * D_RWKV]
    hw = jnp.tanh(s[:, 0:RANK_PAD]).astype(jnp.bfloat16)
    xa = s[:, RANK_PAD:2 * RANK_PAD].astype(jnp.bfloat16)
    sg = jax.nn.sigmoid(s[:, 2 * RANK_PAD:]).astype(jnp.bfloat16)

    li = lax.broadcasted_iota(jnp.int32, (LANES, LANES), 0) // RWKV_HEAD
    lj = lax.broadcasted_iota(jnp.int32, (LANES, LANES), 1) // RWKV_HEAD
    ones_bd = (li == lj).astype(jnp.bfloat16)

    kk = k * kk_ref[...]
    kk = kk / jnp.maximum(jnp.sqrt(_head_sum(kk * kk, ones_bd)), 1e-12)

    ti = lax.broadcasted_iota(jnp.int32, (tm, tm), 0)
    tj = lax.broadcasted_iota(jnp.int32, (tm, tm), 1)
    same_chunk = (ti // CHUNK) == (tj // CHUNK)

    k_sum = jnp.zeros_like(k)
    dirs = ((w0f_ref, w2f_ref, a0f_ref, a2f_ref, atf_ref, btf_ref, ktf_ref, rtf_ref, glf_ref, False),
            (w0b_ref, w2b_ref, a0b_ref, a2b_ref, atb_ref, btb_ref, ktb_ref, rtb_ref, glb_ref, True))
    for w0_ref, w2_ref, a0_ref, a2_ref, at_ref, bt_ref, kt_ref, rt_ref, gl_ref, rev in dirs:
        z = w0_ref[...] + _dot(hw, w2_ref[...])
        lw = -math.exp(-0.5) * jax.nn.sigmoid(z)
        a = jax.nn.sigmoid(a0_ref[...] + _dot(xa, a2_ref[...]))
        k_dir = k * (1.0 + (a - 1.0) * ka_ref[...])
        k_sum = k_sum + k_dir
        order = (tj >= ti) if rev else (tj <= ti)
        tri = jnp.where(same_chunk & order, 1.0, 0.0).astype(jnp.bfloat16)
        c = sum(_dot(tri, piece) for piece in _bf16_split(lw, CUMSUM_PIECES))
        e_neg = jnp.exp(-c)
        at_ref[...] = (-kk * jnp.exp(c - lw)).astype(at_ref.dtype)
        bt_ref[...] = (kk * a * e_neg).astype(bt_ref.dtype)
        kt_ref[...] = (k_dir * e_neg).astype(kt_ref.dtype)
        e_pos = jnp.exp(c)
        rt_ref[...] = (r * e_pos).astype(rt_ref.dtype)
        for j in range(n_chunks):
            end = j * CHUNK if rev else (j + 1) * CHUNK - 1
            gl_ref[j] = e_pos[end:end + 1, :]

    v_ref[...] = v.astype(v_ref.dtype)
    gate_ref[...] = _dot(sg, g2_ref[...]).astype(gate_ref.dtype)
    bonus_ref[...] = (_head_sum(r * k_sum * rk_ref[...], ones_bd) * v).astype(bonus_ref.dtype)


def rwkv_prep(proj, params, tm=128):
    t = proj.shape[0]
    nb = t // tm
    hb = tm // SUBLANES
    n_halo = t // SUBLANES
    wide = 3 * D_RWKV
    small_blk = wide // SMALL_COLS

    def row(c, cb=0):
        return pl.BlockSpec((tm, c), lambda i: (i, cb))

    def halo_prev(c, cb=0):
        return pl.BlockSpec((SUBLANES, c), lambda i: (jnp.maximum(i * hb - 1, 0), cb))

    def halo_next(c, cb=0):
        return pl.BlockSpec((SUBLANES, c), lambda i: (jnp.minimum((i + 1) * hb, n_halo - 1), cb))

    def vec(c):
        return pl.BlockSpec((1, c), lambda i: (0, 0))

    def mat(r_, c):
        return pl.BlockSpec((r_, c), lambda i: (0, 0))

    gl_spec = pl.BlockSpec((tm // CHUNK, 1, D_RWKV), lambda i: (i, 0, 0))
    big = jax.ShapeDtypeStruct((t, D_RWKV), jnp.float32)
    gl = jax.ShapeDtypeStruct((t // CHUNK, 1, D_RWKV), jnp.float32)
    in_specs = [row(wide), halo_prev(wide), halo_next(wide),
                row(SMALL_COLS, small_blk), halo_prev(SMALL_COLS, small_blk), halo_next(SMALL_COLS, small_blk),
                vec(wide), vec(wide), vec(SMALL_COLS), vec(SMALL_COLS)]
    in_specs += [vec(D_RWKV)] * 7
    in_specs += [mat(RANK_PAD, D_RWKV)] * 4 + [mat(GATE_RANK, D_RWKV)]
    dir_specs = [row(D_RWKV)] * 4 + [gl_spec]
    big_bf = jax.ShapeDtypeStruct((t, D_RWKV), jnp.bfloat16)
    dir_shapes = [big_bf] * 4 + [gl]
    return pl.pallas_call(
        _rwkv_prep_kernel,
        grid=(nb,),
        in_specs=in_specs,
        out_specs=dir_specs + dir_specs + [row(D_RWKV)] * 3,
        out_shape=dir_shapes + dir_shapes + [big_bf, big_bf, big_bf],
        compiler_params=_cparams(1),
        name="rwkv_prep",
    )(proj, proj, proj, proj, proj, proj, *params)


def _stack(x, lane_head):
    zero = jnp.zeros_like(x)
    return jnp.concatenate([jnp.where(lane_head == 0, x, zero),
                            jnp.where(lane_head == 1, x, zero)], axis=0)


def _scan_masks(rev):
    n2 = 2 * CHUNK
    ri = lax.broadcasted_iota(jnp.int32, (n2, n2), 0)
    ci = lax.broadcasted_iota(jnp.int32, (n2, n2), 1)
    same = (ri // CHUNK) == (ci // CHUNK)
    before = (ci > ri) if rev else (ci < ri)
    strict = same & before
    incl = same & (before | (ri == ci))
    eye = jnp.where(ri == ci, 1.0, 0.0)
    return strict, jnp.concatenate([incl, incl], axis=1), eye


def _scan_prepare(units):
    bf = jnp.bfloat16
    n2 = 2 * CHUNK
    lane_head = lax.broadcasted_iota(jnp.int32, (CHUNK, PAIR), 1) // RWKV_HEAD
    stacked = [[_stack(x, lane_head) for x in u[:5]] for u in units]
    bks = [jnp.concatenate([s[1], s[2]], axis=0) for s in stacked]
    g_as = [_dot(s[0], bk, _NT) for s, bk in zip(stacked, bks)]
    g_rs = [_dot(s[3], bk, _NT) for s, bk in zip(stacked, bks)]
    xs = [jnp.where(u[6][0], g[:, :n2], 0.0) for u, g in zip(units, g_as)]
    a_aks = [jnp.where(u[6][0], g[:, n2:], 0.0).astype(bf) for u, g in zip(units, g_as)]
    a_rs = [jnp.where(u[6][1], g, 0.0).astype(bf) for u, g in zip(units, g_rs)]
    akv = [_dot(a, s[4]).astype(bf) for a, s in zip(a_aks, stacked)]

    tinvs = [u[6][2] + x for u, x in zip(units, xs)]
    for _ in range(int(math.log2(CHUNK)) - 1):
        xbs = [x.astype(bf) for x in xs]
        xs = [_dot(xb, xb) for xb in xbs]
        tinvs = [t + _dot(t.astype(bf), x.astype(bf)) for t, x in zip(tinvs, xs)]

    wqs = [_dot(t.astype(bf), jnp.concatenate([s[0], kv], axis=1))
           for t, s, kv in zip(tinvs, stacked, akv)]
    out = []
    for u, s, bk, wq, a_r in zip(units, stacked, bks, wqs, a_rs):
        gl = u[5]
        w = wq[:, :PAIR].astype(bf)
        q = wq[:, PAIR:]
        bkg = (bk * gl).astype(bf)
        m = _dot(w, bkg[:n2], _TN)
        n = _dot(jnp.concatenate([q.astype(bf), s[4]], axis=0), bkg, _TN)
        out.append((w, q, s[3], a_r, s[4], m.astype(bf), n, gl))
    return out


def _rwkv_scan_kernel(atf_ref, btf_ref, ktf_ref, rtf_ref, glf_ref, vf_ref,
                      atb_ref, btb_ref, ktb_ref, rtb_ref, glb_ref, vb_ref,
                      yf_ref, yb_ref, sf_ref, sb_ref, *, n_blk, n_par):
    @pl.when(pl.program_id(1) == 0)
    def _():
        sf_ref[...] = jnp.zeros_like(sf_ref)
        sb_ref[...] = jnp.zeros_like(sb_ref)

    def tile(j, p):
        return pl.ds(j * CHUNK, CHUNK), pl.ds(p * PAIR, PAIR)

    masks_f = _scan_masks(False)
    masks_b = _scan_masks(True)
    units = []
    for p in range(n_par):
        for j in range(n_blk):
            units.append(tuple(ref[tile(j, p)] for ref in (atf_ref, btf_ref, ktf_ref, rtf_ref, vf_ref))
                         + (glf_ref[j, :, pl.ds(p * PAIR, PAIR)], masks_f))
            units.append(tuple(ref[tile(j, p)] for ref in (atb_ref, btb_ref, ktb_ref, rtb_ref, vb_ref))
                         + (glb_ref[j, :, pl.ds(p * PAIR, PAIR)], masks_b))
    prep = _scan_prepare(units)

    sf = [sf_ref[p] for p in range(n_par)]
    sb = [sb_ref[p] for p in range(n_par)]
    pending = []
    for j in range(n_blk):
        jb = n_blk - 1 - j
        for p in range(n_par):
            pf = prep[2 * (p * n_blk + j)]
            pb = prep[2 * (p * n_blk + jb) + 1]
            sfb = sf[p].astype(jnp.bfloat16)
            sbb = sb[p].astype(jnp.bfloat16)
            sf[p] = sf[p] * pf[7] + _dot(sfb, pf[5]) + pf[6]
            sb[p] = sb[p] * pb[7] + _dot(sbb, pb[5]) + pb[6]
            pending.append((pf, sfb, yf_ref, tile(j, p)))
            pending.append((pb, sbb, yb_ref, tile(jb, p)))
    for p in range(n_par):
        sf_ref[p] = sf[p]
        sb_ref[p] = sb[p]
    us = [(_dot(u[0], s0b, _NT) + u[1]).astype(jnp.bfloat16) for u, s0b, _, _ in pending]
    for (u, s0b, y_ref, where), uu in zip(pending, us):
        y2 = _dot(u[2], s0b, _NT) + _dot(u[3], jnp.concatenate([uu, u[4]], axis=0))
        y_ref[where] = (y2[:CHUNK] + y2[CHUNK:]).astype(y_ref.dtype)


def rwkv_scan(fwd, bwd, v, n_blk=4, n_par=4):
    t = v.shape[0]
    nb = t // (CHUNK * n_blk)
    n_grp = D_RWKV // (PAIR * n_par)
    width = PAIR * n_par
    f_blk = pl.BlockSpec((CHUNK * n_blk, width), lambda p, c: (c, p))
    b_blk = pl.BlockSpec((CHUNK * n_blk, width), lambda p, c: (nb - 1 - c, p))
    f_gl = pl.BlockSpec((n_blk, 1, width), lambda p, c: (c, 0, p))
    b_gl = pl.BlockSpec((n_blk, 1, width), lambda p, c: (nb - 1 - c, 0, p))
    out = jax.ShapeDtypeStruct((t, D_RWKV), jnp.bfloat16)
    return pl.pallas_call(
        functools.partial(_rwkv_scan_kernel, n_blk=n_blk, n_par=n_par),
        grid=(n_grp, nb),
        in_specs=[f_blk] * 4 + [f_gl, f_blk] + [b_blk] * 4 + [b_gl, b_blk],
        out_specs=[f_blk, b_blk],
        out_shape=[out, out],
        scratch_shapes=[pltpu.VMEM((n_par, PAIR, PAIR), jnp.float32)] * 2,
        compiler_params=_cparams(2),
        name="rwkv_scan",
    )(*fwd, v, *bwd, v)


def _rwkv_post_kernel(yf_ref, yb_ref, bonus_ref, gate_ref, gw_ref, gb_ref, o_ref):
    li = lax.broadcasted_iota(jnp.int32, (LANES, LANES), 0) // RWKV_HEAD
    lj = lax.broadcasted_iota(jnp.int32, (LANES, LANES), 1) // RWKV_HEAD
    mean_bd = jnp.where(li == lj, 1.0 / RWKV_HEAD, 0.0).astype(jnp.bfloat16)
    y = yf_ref[...].astype(jnp.float32) + yb_ref[...].astype(jnp.float32)
    d = y - _head_sum(y, mean_bd)
    var = _head_sum(d * d, mean_bd)
    yn = d * lax.rsqrt(var + RWKV_GN_EPS) * gw_ref[...] + gb_ref[...]
    o_ref[...] = ((yn + bonus_ref[...].astype(jnp.float32))
                  * gate_ref[...].astype(jnp.float32)).astype(o_ref.dtype)


def rwkv_post(yf, yb, bonus, gate, gn_w, gn_b, tm=256):
    t = yf.shape[0]
    row = pl.BlockSpec((tm, D_RWKV), lambda i: (i, 0))
    vec = pl.BlockSpec((1, D_RWKV), lambda i: (0, 0))
    return pl.pallas_call(
        _rwkv_post_kernel,
        grid=(t // tm,),
        in_specs=[row, row, row, row, vec, vec],
        out_specs=row,
        out_shape=jax.ShapeDtypeStruct((t, D_RWKV), jnp.bfloat16),
        compiler_params=_cparams(1),
        name="rwkv_post",
    )(yf, yb, bonus, gate, gn_w.reshape(1, -1), gn_b.reshape(1, -1))


SLOPE_PIECES = 3
SC_PER_HEAD = 2 + SLOPE_PIECES
ONES_ROWS = 16
V_ROWS = DIFF_HEAD + ONES_ROWS
UNDERFLOW_LOG2 = 160.0
BOUND_SLACK = 1.01


def _diff_attn_kernel(sc_ref, qt_ref, k_ref, vt_ref, g_ref, o_ref, s_ref, kmax_ref, *, tq, tk, lam_init):
    bf = jnp.bfloat16
    h = pl.program_id(0)
    qi = pl.program_id(1)
    n_kv = k_ref.shape[0] // tk
    lam = sc_ref[0]
    base = 1 + h * SC_PER_HEAD
    slope = sc_ref[base]
    pieces = [sc_ref[base + 1 + i] for i in range(SLOPE_PIECES)]
    inv_slope = sc_ref[base + 1 + SLOPE_PIECES]

    @pl.when(qi == 0)
    def _():
        li = lax.broadcasted_iota(jnp.int32, (LANES, LANES), 0) // DIFF_D
        lj = lax.broadcasted_iota(jnp.int32, (LANES, LANES), 1) // DIFF_D
        half_ones = (li == lj).astype(bf)

        def tile_max(j, best):
            kf = k_ref[pl.ds(pl.multiple_of(j * tk, tk), tk), :].astype(jnp.float32)
            sq = sum(_dot(p, half_ones) for p in _bf16_split(kf * kf, HEAD_SUM_PIECES))
            return jnp.maximum(best, jnp.max(sq, axis=0, keepdims=True))

        best = lax.fori_loop(0, n_kv, tile_max, jnp.zeros((1, LANES), jnp.float32))
        kmax_ref[0] = jnp.max(best)

    qt = qt_ref[...]
    q_sq = jnp.square(qt.astype(jnp.float32))
    q_n2 = jnp.maximum(jnp.sum(q_sq[:DIFF_D], axis=0, keepdims=True),
                       jnp.sum(q_sq[DIFF_D:], axis=0, keepdims=True))
    bound = jnp.max(jnp.sqrt(q_n2 * kmax_ref[0]))
    reach = (2.0 * BOUND_SLACK * bound + UNDERFLOW_LOG2) * inv_slope
    width = jnp.int32(1)
    for d in range(1, n_kv):
        width = width + jnp.where(reach < float(d * tk), 0, 1).astype(jnp.int32)
    qrow = lax.broadcasted_iota(jnp.int32, (DIFF_HEAD, tq), 0)
    feat = jnp.zeros((DIFF_HEAD, tq), jnp.float32)
    for i, pc in enumerate(pieces):
        feat = jnp.where(qrow % DIFF_D == i, 2.0 * pc, feat)
        feat = jnp.where(qrow % DIFF_D == SLOPE_PIECES + i, pc, feat)
    feat = feat.astype(bf)
    own_rows = [qrow < DIFF_D, qrow >= DIFF_D]
    qts = [jnp.where(own, qt, feat) for own in own_rows]

    kl = lax.broadcasted_iota(jnp.int32, (tk, LANES), 1)
    kr = lax.broadcasted_iota(jnp.int32, (tk, LANES), 0)
    kfeat = jnp.where(kl % DIFF_D < SLOPE_PIECES, kr >> 1,
                      jnp.where(kl % DIFF_D < 2 * SLOPE_PIECES, kr & 1, 0)).astype(jnp.float32)
    kfeat_before = kfeat.astype(bf)
    kfeat_after = (-kfeat).astype(bf)
    kzero = jnp.zeros((tk, LANES), bf)
    own_lanes = [kl < DIFF_D, kl >= DIFF_D]

    qpos = qi * tq + lax.broadcasted_iota(jnp.int32, (1, tq), 1)
    q_bias = slope * qpos.astype(jnp.float32)

    def online(carry, s, t_q, vt):
        m, acc = carry
        m_new = jnp.maximum(m, jnp.max(s, axis=0, keepdims=True) + t_q)
        alpha = jnp.exp2(m - m_new)
        p = jnp.exp2(s - (m_new - t_q))
        return m_new, alpha * acc + _dot(vt, p.astype(bf))

    n_diag = max(tq // tk, 1)
    jd = (qi * tq) // tk
    lo = jnp.maximum(jd - width, 0)
    hi = jnp.minimum(jd + n_diag - 1 + width, n_kv - 1)
    n_off = hi - lo + 1 - n_diag

    def tile_of(n):
        j = lo + n
        return j + n_diag * (j >= jd).astype(jnp.int32)

    def scores(n, slot, mp):
        j = tile_of(n)
        start = pl.multiple_of(j * tk, tk)
        kb = k_ref[pl.ds(start, tk), :]
        kf = jnp.where(j < jd, kfeat_before, kfeat_after)
        s_ref[slot, mp] = _dot(jnp.where(own_lanes[mp], kb, kf), qts[mp])

    def consume(n, slot, mp, carry):
        j = tile_of(n)
        start = pl.multiple_of(j * tk, tk)
        vt = vt_ref[:, pl.ds(start, tk)]
        sign = jnp.where(j < jd, 1.0, -1.0)
        t_q = sign * (slope * (j * tk).astype(jnp.float32) - q_bias)
        return online(carry, s_ref[slot, mp], t_q, vt)

    def step(n_next, n_cur, slot_next, slot_cur, carries):
        out = []
        for mp in range(2):
            scores(n_next, slot_next, mp)
            out.append(consume(n_cur, slot_cur, mp, carries[mp]))
        return tuple(out)

    def diag_scores(d, slot):
        kb_d = k_ref[pl.ds(pl.multiple_of((jd + d) * tk, tk), tk), :]
        kpos = (jd + d) * tk + lax.broadcasted_iota(jnp.int32, (tk, tq), 0)
        bias = slope * jnp.abs(kpos - qpos).astype(jnp.float32)
        for mp, (own, q_m) in enumerate(zip(own_lanes, qts)):
            s_ref[slot, mp] = _dot(jnp.where(own, kb_d, kzero), q_m) - bias

    def diag_consume(d, slot, mp, carry):
        vt_d = vt_ref[:, pl.ds(pl.multiple_of((jd + d) * tk, tk), tk)]
        return online(carry, s_ref[slot, mp], 0.0, vt_d)

    init = (jnp.full((1, tq), -jnp.inf, jnp.float32), jnp.zeros((V_ROWS, tq), jnp.float32))
    carries = [init, init]
    diag_slots = [1] if n_diag == 1 else [0, 1]
    for d, slot in enumerate(diag_slots[:-1]):
        diag_scores(d, slot)
    diag_scores(n_diag - 1, 1)
    for d, slot in enumerate(diag_slots[:-1]):
        carries = [diag_consume(d, slot, mp, carries[mp]) for mp in range(2)]
    for mp in range(2):
        scores(jnp.int32(0), 0, mp)
        carries[mp] = diag_consume(n_diag - 1, 1, mp, carries[mp])

    def body(i, carries):
        n = 2 * i
        carries = step(n + 1, n, 1, 0, carries)
        return step(n + 2, n + 1, 0, 1, carries)

    n_pairs = (n_off - 1) // 2
    carries = lax.fori_loop(0, n_pairs, body, tuple(carries))
    last = n_off - 1

    def tail_two(carries):
        carries = step(last, last - 1, 1, 0, carries)
        return tuple(consume(last, 1, mp, carries[mp]) for mp in range(2))

    def tail_one(carries):
        return tuple(consume(last, 0, mp, carries[mp]) for mp in range(2))

    (_, acc0), (_, acc1) = lax.cond(last == 2 * n_pairs + 1, tail_two, tail_one, carries)
    o0 = acc0[:DIFF_HEAD] / acc0[DIFF_HEAD:DIFF_HEAD + 1]
    o1 = acc1[:DIFF_HEAD] / acc1[DIFF_HEAD:DIFF_HEAD + 1]
    o = (o0 - lam * o1).T
    o_ref[...] = (_rms(o, g_ref[...], SUBLN_EPS) * (1.0 - lam_init)).astype(o_ref.dtype)


def _bf16_pieces(x, n):
    out = []
    for _ in range(n):
        p = (x.view(np.uint32) & np.uint32(0xFFFF0000)).view(np.float32)
        out.append(p)
        x = (x - p).astype(np.float32)
    return out


def diff_attention(qt, k, vt, lam, subln_g, lam_init, tq=1024, tk=512):
    t = k.shape[0]
    assert tq // tk in (1, 2) and tq % tk == 0 and tk // 2 <= 256
    assert t // tk > tq // tk
    slopes = 2.0 ** (-ALIBI_MAX_EXP * np.arange(1, N_DIFF_HEADS + 1, dtype=np.float64) / N_DIFF_HEADS)
    slopes = (slopes * math.log2(math.e)).astype(np.float32)
    pieces = _bf16_pieces(slopes, SLOPE_PIECES)
    slope_used = sum(pieces)
    per_head = np.stack([slope_used] + pieces + [(1.0 / slope_used).astype(np.float32)], axis=1)
    lam = jnp.concatenate([lam, jnp.asarray(per_head.reshape(-1))])
    return pl.pallas_call(
        functools.partial(_diff_attn_kernel, tq=tq, tk=tk, lam_init=lam_init),
        grid=(N_DIFF_HEADS, t // tq),
        in_specs=[pl.BlockSpec(memory_space=pltpu.SMEM),
                  pl.BlockSpec((DIFF_HEAD, tq), lambda h, i: (h, i)),
                  pl.BlockSpec((t, DIFF_HEAD), lambda h, i: (0, h)),
                  pl.BlockSpec((V_ROWS, t), lambda h, i: (h, 0)),
                  pl.BlockSpec((1, DIFF_HEAD), lambda h, i: (0, 0))],
        out_specs=pl.BlockSpec((tq, DIFF_HEAD), lambda h, i: (i, h)),
        out_shape=jax.ShapeDtypeStruct((t, D_DIFF), jnp.bfloat16),
        scratch_shapes=[pltpu.VMEM((2, 2, tk, tq), jnp.float32),
                        pltpu.SMEM((1,), jnp.float32)],
        compiler_params=_cparams(2),
        name="diff_attn",
    )(lam, qt, k, vt, subln_g.reshape(1, DIFF_HEAD))


def _pad_cols(w, to):
    return jnp.pad(w, ((0, 0), (0, to - w.shape[1])))


def _layer(h, l, lam_init, x_norm, prm):
    bf = jnp.bfloat16
    f32 = jnp.float32
    f1 = swiglu_ffn(x_norm, prm["ffn1_w_gate"][l], prm["ffn1_w_up"][l],
                    prm["ffn1_w_down"][l].astype(bf), 2048, 256, 512, 512)
    h, xn = resid_norm(h, f1, prm["ffn1_post_g"][l], prm["mix_pre_g"][l], FFN_RESIDUAL, bf)

    w_in = prm["w_in"][l]
    c_rkv = 3 * D_RWKV
    c_w = c_rkv + DECAY_RANK
    c_a = c_w + ICLR_RANK
    c_g = c_a + GATE_RANK
    w_small = jnp.concatenate([_pad_cols(w_in[:, c_rkv:c_w], RANK_PAD),
                               _pad_cols(w_in[:, c_w:c_a], RANK_PAD),
                               w_in[:, c_a:c_g]], axis=1)
    w_rwkv = jnp.concatenate([w_in[:, :c_rkv], w_small], axis=1).astype(bf)
    proj = matmul(xn, w_rwkv, f32, 1024, 512)
    q_scale = DIFF_D ** -0.5 * math.log2(math.e)
    col_scale = jnp.concatenate([jnp.full((1, D_DIFF), q_scale, f32), jnp.ones((1, 2 * D_DIFF), f32)], axis=1)
    qkv = matmul_col_scaled(xn, w_in[:, c_g:].astype(bf), col_scale, bf, 1024, 512)

    def small_vec(a):
        return jnp.concatenate([_pad_cols(a[None, c_rkv:c_w], RANK_PAD),
                                _pad_cols(a[None, c_w:c_a], RANK_PAD),
                                a[None, c_a:c_g]], axis=1)

    def pad_rows(w):
        return jnp.pad(w, ((0, RANK_PAD - w.shape[0]), (0, 0))).astype(bf)

    mu_p, mu_n = prm["mu_prev"][l], prm["mu_next"][l]
    vecs = [prm[n][l].reshape(1, D_RWKV) for n in ("w0_f", "w0_b", "a0_f", "a0_b", "k_k", "k_a", "r_k")]
    mats = [pad_rows(prm["w2_f"][l]), pad_rows(prm["w2_b"][l]),
            pad_rows(prm["a2_f"][l]), pad_rows(prm["a2_b"][l]), prm["g2"][l].astype(bf)]
    prep = rwkv_prep(proj, [mu_p[None, :c_rkv], mu_n[None, :c_rkv], small_vec(mu_p), small_vec(mu_n)]
                     + vecs + mats)
    fwd, bwd, (v_r, gate, bonus) = prep[0:5], prep[5:10], prep[10:13]
    yf, yb = rwkv_scan(fwd, bwd, v_r)
    y_a = rwkv_post(yf, yb, bonus, gate, prm["gn_w"][l], prm["gn_b"][l])

    lam = (jnp.exp(jnp.sum(prm["lq1"][l] * prm["lk1"][l]))
           - jnp.exp(jnp.sum(prm["lq2"][l] * prm["lk2"][l])) + lam_init).reshape(1)
    t = qkv.shape[0]
    vt = qkv[:, 2 * D_DIFF:].T.reshape(N_DIFF_HEADS, DIFF_HEAD, t)
    vt = jnp.concatenate([vt, jnp.ones((N_DIFF_HEADS, ONES_ROWS, t), bf)], axis=1).reshape(N_DIFF_HEADS * V_ROWS, t)
    y_b = diff_attention(qkv[:, :D_DIFF].T, qkv[:, D_DIFF:2 * D_DIFF], vt, lam, prm["subln_g"][l], lam_init)

    w_out = prm["w_out"][l]
    mix = matmul_two(y_a, y_b, w_out[:D_RWKV].astype(bf), w_out[D_RWKV:].astype(bf), f32, 1024, 512)
    h, xn = resid_norm(h, mix, prm["mix_post_g"][l], prm["ffn2_pre_g"][l], 1.0, bf)

    f2 = swiglu_ffn(xn, prm["ffn2_w_gate"][l], prm["ffn2_w_up"][l],
                    prm["ffn2_w_down"][l].astype(bf), 2048, 256, 512, 512)
    return resid_norm_out(h, f2, prm["ffn2_post_g"][l], prm["final_g"][l], FFN_RESIDUAL)


def kernel(x, ffn1_pre_g, ffn1_w_gate, ffn1_w_up, ffn1_w_down, ffn1_post_g, mix_pre_g, w_in, mu_prev, mu_next, w0_f, w2_f, w0_b, w2_b, a0_f, a2_f, a0_b, a2_b, g2, k_k, k_a, r_k, gn_w, gn_b, lq1, lk1, lq2, lk2, subln_g, w_out, mix_post_g, ffn2_pre_g, ffn2_w_gate, ffn2_w_up, ffn2_w_down, ffn2_post_g, final_g):
    prm = dict(locals())
    bsz, t, d = x.shape
    assert bsz == 1
    depth = ffn1_pre_g.shape[0]
    h = x.reshape(t, d)
    for l in range(depth):
        lam_init = 0.8 - 0.6 * math.exp(-0.3 * l)
        x_norm = rms_norm_cast(h, ffn1_pre_g[l], jnp.bfloat16)
        h = _layer(h, l, lam_init, x_norm, prm)
    return h.reshape(bsz, t, d)
```

```python
import functools
import math

import jax
import jax.numpy as jnp
import numpy as np
from jax import lax
from jax.experimental import pallas as pl
from jax.experimental.pallas import tpu as pltpu

D_MODEL = 4096
D_RWKV = 2048
D_DIFF = 2048
RWKV_HEAD = 64
DECAY_RANK = 96
ICLR_RANK = 96
GATE_RANK = 256
DIFF_D = 64
DIFF_HEAD = 128
N_DIFF_HEADS = 16
ALIBI_MAX_EXP = 8.0
NORM_EPS = 1e-6
RWKV_GN_EPS = 64e-5
SUBLN_EPS = 1e-5
FFN_RESIDUAL = 0.5

LANES = 128
SUBLANES = 8
VMEM_LIMIT = 60 * 1024 * 1024

RANK_PAD = 128
SMALL_COLS = 2 * RANK_PAD + GATE_RANK
CHUNK = 64
PAIR = 2 * RWKV_HEAD


def _cparams(n_axes):
    return pltpu.CompilerParams(
        dimension_semantics=("arbitrary",) * n_axes, vmem_limit_bytes=VMEM_LIMIT)


def _dot(a, b, dims=(((1,), (0,)), ((), ()))):
    return lax.dot_general(a, b, dims, preferred_element_type=jnp.float32)


_NT = (((1,), (1,)), ((), ()))
_TN = (((0,), (0,)), ((), ()))


def _rms(x, g, eps):
    return x * lax.rsqrt(jnp.mean(x * x, axis=-1, keepdims=True) + eps) * g


def _norm_kernel(x_ref, g_ref, o_ref):
    o_ref[...] = _rms(x_ref[...], g_ref[...], NORM_EPS).astype(o_ref.dtype)


def rms_norm_cast(x, g, out_dtype, tm=256):
    m, d = x.shape
    return pl.pallas_call(
        _norm_kernel,
        grid=(m // tm,),
        in_specs=[pl.BlockSpec((tm, d), lambda i: (i, 0)),
                  pl.BlockSpec((1, d), lambda i: (0, 0))],
        out_specs=pl.BlockSpec((tm, d), lambda i: (i, 0)),
        out_shape=jax.ShapeDtypeStruct((m, d), out_dtype),
        compiler_params=_cparams(1),
        name="rms_norm",
    )(x, g.reshape(1, d))


def _resid_norm_kernel(h_ref, f_ref, gp_ref, gn_ref, h_out_ref, n_out_ref, *, scale):
    h = h_ref[...] + scale * _rms(f_ref[...], gp_ref[...], NORM_EPS)
    h_out_ref[...] = h
    n_out_ref[...] = _rms(h, gn_ref[...], NORM_EPS).astype(n_out_ref.dtype)


def resid_norm(h, f, g_post, g_next, scale, out_dtype, tm=256):
    m, d = h.shape
    row = pl.BlockSpec((tm, d), lambda i: (i, 0))
    vec = pl.BlockSpec((1, d), lambda i: (0, 0))
    return pl.pallas_call(
        functools.partial(_resid_norm_kernel, scale=scale),
        grid=(m // tm,),
        in_specs=[row, row, vec, vec],
        out_specs=[row, row],
        out_shape=[jax.ShapeDtypeStruct((m, d), jnp.float32),
                   jax.ShapeDtypeStruct((m, d), out_dtype)],
        compiler_params=_cparams(1),
        name="resid_norm",
    )(h, f, g_post.reshape(1, d), g_next.reshape(1, d))


def _resid_norm_out_kernel(h_ref, f_ref, gp_ref, gn_ref, n_out_ref, *, scale):
    h = h_ref[...] + scale * _rms(f_ref[...], gp_ref[...], NORM_EPS)
    n_out_ref[...] = _rms(h, gn_ref[...], NORM_EPS).astype(n_out_ref.dtype)


def resid_norm_out(h, f, g_post, g_next, scale, tm=256):
    m, d = h.shape
    row = pl.BlockSpec((tm, d), lambda i: (i, 0))
    vec = pl.BlockSpec((1, d), lambda i: (0, 0))
    return pl.pallas_call(
        functools.partial(_resid_norm_out_kernel, scale=scale),
        grid=(m // tm,),
        in_specs=[row, row, vec, vec],
        out_specs=row,
        out_shape=jax.ShapeDtypeStruct((m, d), jnp.float32),
        compiler_params=_cparams(1),
        name="resid_norm_out",
    )(h, f, g_post.reshape(1, d), g_next.reshape(1, d))


def _mm_kernel(x_ref, w_ref, o_ref):
    o_ref[...] = _dot(x_ref[...], w_ref[...]).astype(o_ref.dtype)


def matmul(x, w, out_dtype, tm, tn):
    m, k = x.shape
    _, n = w.shape
    return pl.pallas_call(
        _mm_kernel,
        grid=(n // tn, m // tm),
        in_specs=[pl.BlockSpec((tm, k), lambda j, i: (i, 0)),
                  pl.BlockSpec((k, tn), lambda j, i: (0, j))],
        out_specs=pl.BlockSpec((tm, tn), lambda j, i: (i, j)),
        out_shape=jax.ShapeDtypeStruct((m, n), out_dtype),
        compiler_params=_cparams(2),
        name="matmul",
    )(x, w)


def _mm_two_kernel(xa_ref, xb_ref, wa_ref, wb_ref, o_ref):
    o_ref[...] = (_dot(xa_ref[...], wa_ref[...]) + _dot(xb_ref[...], wb_ref[...])).astype(o_ref.dtype)


def matmul_two(xa, xb, wa, wb, out_dtype, tm, tn):
    m, ka = xa.shape
    _, kb = xb.shape
    _, n = wa.shape
    return pl.pallas_call(
        _mm_two_kernel,
        grid=(n // tn, m // tm),
        in_specs=[pl.BlockSpec((tm, ka), lambda j, i: (i, 0)),
                  pl.BlockSpec((tm, kb), lambda j, i: (i, 0)),
                  pl.BlockSpec((ka, tn), lambda j, i: (0, j)),
                  pl.BlockSpec((kb, tn), lambda j, i: (0, j))],
        out_specs=pl.BlockSpec((tm, tn), lambda j, i: (i, j)),
        out_shape=jax.ShapeDtypeStruct((m, n), out_dtype),
        compiler_params=_cparams(2),
        name="matmul_two",
    )(xa, xb, wa, wb)


def _mm_scaled_kernel(x_ref, w_ref, cs_ref, o_ref):
    o_ref[...] = (_dot(x_ref[...], w_ref[...]) * cs_ref[...]).astype(o_ref.dtype)


def matmul_col_scaled(x, w, col_scale, out_dtype, tm, tn):
    m, k = x.shape
    _, n = w.shape
    return pl.pallas_call(
        _mm_scaled_kernel,
        grid=(n // tn, m // tm),
        in_specs=[pl.BlockSpec((tm, k), lambda j, i: (i, 0)),
                  pl.BlockSpec((k, tn), lambda j, i: (0, j)),
                  pl.BlockSpec((1, tn), lambda j, i: (0, j))],
        out_specs=pl.BlockSpec((tm, tn), lambda j, i: (i, j)),
        out_shape=jax.ShapeDtypeStruct((m, n), out_dtype),
        compiler_params=_cparams(2),
        name="matmul_col_scaled",
    )(x, w, col_scale)


def _gate_up_kernel(x_ref, wg_ref, wu_ref, o_ref, wg_bf_ref, wu_bf_ref):
    @pl.when(pl.program_id(1) == 0)
    def _():
        wg_bf_ref[...] = wg_ref[...].astype(jnp.bfloat16)
        wu_bf_ref[...] = wu_ref[...].astype(jnp.bfloat16)

    x = x_ref[...]
    g = _dot(x, wg_bf_ref[...])
    u = _dot(x, wu_bf_ref[...])
    o_ref[...] = (g * jax.nn.sigmoid(g) * u).astype(o_ref.dtype)


def gate_up(x, wg, wu, tm, tn):
    m, k = x.shape
    _, n = wg.shape
    wspec = pl.BlockSpec((k, tn), lambda j, i: (0, j))
    return pl.pallas_call(
        _gate_up_kernel,
        grid=(n // tn, m // tm),
        in_specs=[pl.BlockSpec((tm, k), lambda j, i: (i, 0)), wspec, wspec],
        out_specs=pl.BlockSpec((tm, tn), lambda j, i: (i, j)),
        out_shape=jax.ShapeDtypeStruct((m, n), jnp.bfloat16),
        scratch_shapes=[pltpu.VMEM((k, tn), jnp.bfloat16)] * 2,
        compiler_params=_cparams(2),
        name="gate_up",
    )(x, wg, wu)


def swiglu_ffn(xn, wg, wu, wd, tm_gu, tn_gu, tm_d, tn_d):
    a = gate_up(xn, wg, wu, tm_gu, tn_gu)
    return matmul(a, wd, jnp.float32, tm_d, tn_d)


def _bf16_split(x, n):
    pieces = []
    for _ in range(n - 1):
        p = x.astype(jnp.bfloat16)
        pieces.append(p)
        x = x - p.astype(jnp.float32)
    pieces.append(x.astype(jnp.bfloat16))
    return pieces


HEAD_SUM_PIECES = 2
CUMSUM_PIECES = 3


def _head_sum(x, w_bd):
    parts = []
    for s in range(x.shape[1] // LANES):
        pieces = _bf16_split(x[:, s * LANES:(s + 1) * LANES], HEAD_SUM_PIECES)
        parts.append(sum(_dot(p, w_bd) for p in pieces))
    return jnp.concatenate(parts, axis=1)


def _token_shift(p_ref, pp_ref, pn_ref, mup_ref, mun_ref, first, last):
    p = p_ref[...]
    rows = p.shape[0]
    rid = lax.broadcasted_iota(jnp.int32, p.shape, 0)
    prev_row = jnp.where(first, 0.0, pp_ref[SUBLANES - 1:SUBLANES, :])
    next_row = jnp.where(last, 0.0, pn_ref[0:1, :])
    p_prev = jnp.where(rid == 0, prev_row, pltpu.roll(p, 1, 0))
    p_next = jnp.where(rid == rows - 1, next_row, pltpu.roll(p, rows - 1, 0))
    return p + mup_ref[...] * (p_prev - p) + mun_ref[...] * (p_next - p)


def _rwkv_prep_kernel(
        p_ref, pp_ref, pn_ref, s_ref, sp_ref, sn_ref,
        mup_ref, mun_ref, mups_ref, muns_ref,
        w0f_ref, w0b_ref, a0f_ref, a0b_ref, kk_ref, ka_ref, rk_ref,
        w2f_ref, w2b_ref, a2f_ref, a2b_ref, g2_ref,
        atf_ref, btf_ref, ktf_ref, rtf_ref, glf_ref,
        atb_ref, btb_ref, ktb_ref, rtb_ref, glb_ref,
        v_ref, gate_ref, bonus_ref):
    i = pl.program_id(0)
    first = i == 0
    last = i == pl.num_programs(0) - 1
    tm = p_ref.shape[0]
    n_chunks = tm // CHUNK

    p = _token_shift(p_ref, pp_ref, pn_ref, mup_ref, mun_ref, first, last)
    s = _token_shift(s_ref, sp_ref, sn_ref, mups_ref, muns_ref, first, last)
    r = p[:, 0:D_RWKV]
    k = p[:, D_RWKV:2 * D_RWKV]
    v = p[:, 2 * D_RWKV:3 * D_RWKV]
    hw = jnp.tanh(s[:, 0:RANK_PAD]).astype(jnp.bfloat16)
    xa = s[:, RANK_PAD:2 * RANK_PAD].astype(jnp.bfloat16)
    sg = jax.nn.sigmoid(s[:, 2 * RANK_PAD:]).astype(jnp.bfloat16)

    li = lax.broadcasted_iota(jnp.int32, (LANES, LANES), 0) // RWKV_HEAD
    lj = lax.broadcasted_iota(jnp.int32, (LANES, LANES), 1) // RWKV_HEAD
    ones_bd = (li == lj).astype(jnp.bfloat16)

    kk = k * kk_ref[...]
    kk = kk / jnp.maximum(jnp.sqrt(_head_sum(kk * kk, ones_bd)), 1e-12)

    ti = lax.broadcasted_iota(jnp.int32, (tm, tm), 0)
    tj = lax.broadcasted_iota(jnp.int32, (tm, tm), 1)
    same_chunk = (ti // CHUNK) == (tj // CHUNK)

    k_sum = jnp.zeros_like(k)
    dirs = ((w0f_ref, w2f_ref, a0f_ref, a2f_ref, atf_ref, btf_ref, ktf_ref, rtf_ref, glf_ref, False),
            (w0b_ref, w2b_ref, a0b_ref, a2b_ref, atb_ref, btb_ref, ktb_ref, rtb_ref, glb_ref, True))
    for w0_ref, w2_ref, a0_ref, a2_ref, at_ref, bt_ref, kt_ref, rt_ref, gl_ref, rev in dirs:
        z = w0_ref[...] + _dot(hw, w2_ref[...])
        lw = -math.exp(-0.5) * jax.nn.sigmoid(z)
        a = jax.nn.sigmoid(a0_ref[...] + _dot(xa, a2_ref[...]))
        k_dir = k * (1.0 + (a - 1.0) * ka_ref[...])
        k_sum = k_sum + k_dir
        order = (tj >= ti) if rev else (tj <= ti)
        tri = jnp.where(same_chunk & order, 1.0, 0.0).astype(jnp.bfloat16)
        c = sum(_dot(tri, piece) for piece in _bf16_split(lw, CUMSUM_PIECES))
        e_neg = jnp.exp(-c)
        at_ref[...] = (-kk * jnp.exp(c - lw)).astype(at_ref.dtype)
        bt_ref[...] = (kk * a * e_neg).astype(bt_ref.dtype)
        kt_ref[...] = (k_dir * e_neg).astype(kt_ref.dtype)
        e_pos = jnp.exp(c)
        rt_ref[...] = (r * e_pos).astype(rt_ref.dtype)
        for j in range(n_chunks):
            end = j * CHUNK if rev else (j + 1) * CHUNK - 1
            gl_ref[j] = e_pos[end:end + 1, :]

    v_ref[...] = v.astype(v_ref.dtype)
    gate_ref[...] = _dot(sg, g2_ref[...]).astype(gate_ref.dtype)
    bonus_ref[...] = (_head_sum(r * k_sum * rk_ref[...], ones_bd) * v).astype(bonus_ref.dtype)


def rwkv_prep(proj, params, tm=128):
    t = proj.shape[0]
    nb = t // tm
    hb = tm // SUBLANES
    n_halo = t // SUBLANES
    wide = 3 * D_RWKV
    small_blk = wide // SMALL_COLS

    def row(c, cb=0):
        return pl.BlockSpec((tm, c), lambda i: (i, cb))

    def halo_prev(c, cb=0):
        return pl.BlockSpec((SUBLANES, c), lambda i: (jnp.maximum(i * hb - 1, 0), cb))

    def halo_next(c, cb=0):
        return pl.BlockSpec((SUBLANES, c), lambda i: (jnp.minimum((i + 1) * hb, n_halo - 1), cb))

    def vec(c):
        return pl.BlockSpec((1, c), lambda i: (0, 0))

    def mat(r_, c):
        return pl.BlockSpec((r_, c), lambda i: (0, 0))

    gl_spec = pl.BlockSpec((tm // CHUNK, 1, D_RWKV), lambda i: (i, 0, 0))
    big = jax.ShapeDtypeStruct((t, D_RWKV), jnp.float32)
    gl = jax.ShapeDtypeStruct((t // CHUNK, 1, D_RWKV), jnp.float32)
    in_specs = [row(wide), halo_prev(wide), halo_next(wide),
                row(SMALL_COLS, small_blk), halo_prev(SMALL_COLS, small_blk), halo_next(SMALL_COLS, small_blk),
                vec(wide), vec(wide), vec(SMALL_COLS), vec(SMALL_COLS)]
    in_specs += [vec(D_RWKV)] * 7
    in_specs += [mat(RANK_PAD, D_RWKV)] * 4 + [mat(GATE_RANK, D_RWKV)]
    dir_specs = [row(D_RWKV)] * 4 + [gl_spec]
    big_bf = jax.ShapeDtypeStruct((t, D_RWKV), jnp.bfloat16)
    dir_shapes = [big_bf] * 4 + [gl]
    return pl.pallas_call(
        _rwkv_prep_kernel,
        grid=(nb,),
        in_specs=in_specs,
        out_specs=dir_specs + dir_specs + [row(D_RWKV)] * 3,
        out_shape=dir_shapes + dir_shapes + [big_bf, big_bf, big_bf],
        compiler_params=_cparams(1),
        name="rwkv_prep",
    )(proj, proj, proj, proj, proj, proj, *params)


def _stack(x, lane_head):
    zero = jnp.zeros_like(x)
    return jnp.concatenate([jnp.where(lane_head == 0, x, zero),
                            jnp.where(lane_head == 1, x, zero)], axis=0)


def _scan_masks(rev):
    n2 = 2 * CHUNK
    ri = lax.broadcasted_iota(jnp.int32, (n2, n2), 0)
    ci = lax.broadcasted_iota(jnp.int32, (n2, n2), 1)
    same = (ri // CHUNK) == (ci // CHUNK)
    before = (ci > ri) if rev else (ci < ri)
    strict = same & before
    incl = same & (before | (ri == ci))
    eye = jnp.where(ri == ci, 1.0, 0.0)
    return strict, jnp.concatenate([incl, incl], axis=1), eye


def _scan_prepare(units):
    bf = jnp.bfloat16
    n2 = 2 * CHUNK
    lane_head = lax.broadcasted_iota(jnp.int32, (CHUNK, PAIR), 1) // RWKV_HEAD
    stacked = [[_stack(x, lane_head) for x in u[:5]] for u in units]
    bks = [jnp.concatenate([s[1], s[2]], axis=0) for s in stacked]
    g_as = [_dot(s[0], bk, _NT) for s, bk in zip(stacked, bks)]
    g_rs = [_dot(s[3], bk, _NT) for s, bk in zip(stacked, bks)]
    xs = [jnp.where(u[6][0], g[:, :n2], 0.0) for u, g in zip(units, g_as)]
    a_aks = [jnp.where(u[6][0], g[:, n2:], 0.0).astype(bf) for u, g in zip(units, g_as)]
    a_rs = [jnp.where(u[6][1], g, 0.0).astype(bf) for u, g in zip(units, g_rs)]
    akv = [_dot(a, s[4]).astype(bf) for a, s in zip(a_aks, stacked)]

    tinvs = [u[6][2] + x for u, x in zip(units, xs)]
    for _ in range(int(math.log2(CHUNK)) - 1):
        xbs = [x.astype(bf) for x in xs]
        xs = [_dot(xb, xb) for xb in xbs]
        tinvs = [t + _dot(t.astype(bf), x.astype(bf)) for t, x in zip(tinvs, xs)]

    wqs = [_dot(t.astype(bf), jnp.concatenate([s[0], kv], axis=1))
           for t, s, kv in zip(tinvs, stacked, akv)]
    out = []
    for u, s, bk, wq, a_r in zip(units, stacked, bks, wqs, a_rs):
        gl = u[5]
        w = wq[:, :PAIR].astype(bf)
        q = wq[:, PAIR:]
        bkg = (bk * gl).astype(bf)
        m = _dot(w, bkg[:n2], _TN)
        n = _dot(jnp.concatenate([q.astype(bf), s[4]], axis=0), bkg, _TN)
        out.append((w, q, s[3], a_r, s[4], m.astype(bf), n, gl))
    return out


def _rwkv_scan_kernel(atf_ref, btf_ref, ktf_ref, rtf_ref, glf_ref, vf_ref,
                      atb_ref, btb_ref, ktb_ref, rtb_ref, glb_ref, vb_ref,
                      yf_ref, yb_ref, sf_ref, sb_ref, *, n_blk, n_par):
    @pl.when(pl.program_id(1) == 0)
    def _():
        sf_ref[...] = jnp.zeros_like(sf_ref)
        sb_ref[...] = jnp.zeros_like(sb_ref)

    def tile(j, p):
        return pl.ds(j * CHUNK, CHUNK), pl.ds(p * PAIR, PAIR)

    masks_f = _scan_masks(False)
    masks_b = _scan_masks(True)
    units = []
    for p in range(n_par):
        for j in range(n_blk):
            units.append(tuple(ref[tile(j, p)] for ref in (atf_ref, btf_ref, ktf_ref, rtf_ref, vf_ref))
                         + (glf_ref[j, :, pl.ds(p * PAIR, PAIR)], masks_f))
            units.append(tuple(ref[tile(j, p)] for ref in (atb_ref, btb_ref, ktb_ref, rtb_ref, vb_ref))
                         + (glb_ref[j, :, pl.ds(p * PAIR, PAIR)], masks_b))
    prep = _scan_prepare(units)

    sf = [sf_ref[p] for p in range(n_par)]
    sb = [sb_ref[p] for p in range(n_par)]
    pending = []
    for j in range(n_blk):
        jb = n_blk - 1 - j
        for p in range(n_par):
            pf = prep[2 * (p * n_blk + j)]
            pb = prep[2 * (p * n_blk + jb) + 1]
            sfb = sf[p].astype(jnp.bfloat16)
            sbb = sb[p].astype(jnp.bfloat16)
            sf[p] = sf[p] * pf[7] + _dot(sfb, pf[5]) + pf[6]
            sb[p] = sb[p] * pb[7] + _dot(sbb, pb[5]) + pb[6]
            pending.append((pf, sfb, yf_ref, tile(j, p)))
            pending.append((pb, sbb, yb_ref, tile(jb, p)))
    for p in range(n_par):
        sf_ref[p] = sf[p]
        sb_ref[p] = sb[p]
    us = [(_dot(u[0], s0b, _NT) + u[1]).astype(jnp.bfloat16) for u, s0b, _, _ in pending]
    for (u, s0b, y_ref, where), uu in zip(pending, us):
        y2 = _dot(u[2], s0b, _NT) + _dot(u[3], jnp.concatenate([uu, u[4]], axis=0))
        y_ref[where] = (y2[:CHUNK] + y2[CHUNK:]).astype(y_ref.dtype)


def rwkv_scan(fwd, bwd, v, n_blk=4, n_par=4):
    t = v.shape[0]
    nb = t // (CHUNK * n_blk)
    n_grp = D_RWKV // (PAIR * n_par)
    width = PAIR * n_par
    f_blk = pl.BlockSpec((CHUNK * n_blk, width), lambda p, c: (c, p))
    b_blk = pl.BlockSpec((CHUNK * n_blk, width), lambda p, c: (nb - 1 - c, p))
    f_gl = pl.BlockSpec((n_blk, 1, width), lambda p, c: (c, 0, p))
    b_gl = pl.BlockSpec((n_blk, 1, width), lambda p, c: (nb - 1 - c, 0, p))
    out = jax.ShapeDtypeStruct((t, D_RWKV), jnp.bfloat16)
    return pl.pallas_call(
        functools.partial(_rwkv_scan_kernel, n_blk=n_blk, n_par=n_par),
        grid=(n_grp, nb),
        in_specs=[f_blk] * 4 + [f_gl, f_blk] + [b_blk] * 4 + [b_gl, b_blk],
        out_specs=[f_blk, b_blk],
        out_shape=[out, out],
        scratch_shapes=[pltpu.VMEM((n_par, PAIR, PAIR), jnp.float32)] * 2,
        compiler_params=_cparams(2),
        name="rwkv_scan",
    )(*fwd, v, *bwd, v)


def _rwkv_post_kernel(yf_ref, yb_ref, bonus_ref, gate_ref, gw_ref, gb_ref, o_ref):
    li = lax.broadcasted_iota(jnp.int32, (LANES, LANES), 0) // RWKV_HEAD
    lj = lax.broadcasted_iota(jnp.int32, (LANES, LANES), 1) // RWKV_HEAD
    mean_bd = jnp.where(li == lj, 1.0 / RWKV_HEAD, 0.0).astype(jnp.bfloat16)
    y = yf_ref[...].astype(jnp.float32) + yb_ref[...].astype(jnp.float32)
    d = y - _head_sum(y, mean_bd)
    var = _head_sum(d * d, mean_bd)
    yn = d * lax.rsqrt(var + RWKV_GN_EPS) * gw_ref[...] + gb_ref[...]
    o_ref[...] = ((yn + bonus_ref[...].astype(jnp.float32))
                  * gate_ref[...].astype(jnp.float32)).astype(o_ref.dtype)


def rwkv_post(yf, yb, bonus, gate, gn_w, gn_b, tm=256):
    t = yf.shape[0]
    row = pl.BlockSpec((tm, D_RWKV), lambda i: (i, 0))
    vec = pl.BlockSpec((1, D_RWKV), lambda i: (0, 0))
    return pl.pallas_call(
        _rwkv_post_kernel,
        grid=(t // tm,),
        in_specs=[row, row, row, row, vec, vec],
        out_specs=row,
        out_shape=jax.ShapeDtypeStruct((t, D_RWKV), jnp.bfloat16),
        compiler_params=_cparams(1),
        name="rwkv_post",
    )(yf, yb, bonus, gate, gn_w.reshape(1, -1), gn_b.reshape(1, -1))


SLOPE_PIECES = 3
SC_PER_HEAD = 2 + SLOPE_PIECES
ONES_ROWS = 16
V_ROWS = DIFF_HEAD + ONES_ROWS
UNDERFLOW_LOG2 = 160.0
BOUND_SLACK = 1.01


def _diff_attn_kernel(sc_ref, qt_ref, k_ref, vt_ref, g_ref, o_ref, s_ref, kmax_ref, *, tq, tk, lam_init):
    bf = jnp.bfloat16
    h = pl.program_id(0)
    qi = pl.program_id(1)
    n_kv = k_ref.shape[0] // tk
    lam = sc_ref[0]
    base = 1 + h * SC_PER_HEAD
    slope = sc_ref[base]
    pieces = [sc_ref[base + 1 + i] for i in range(SLOPE_PIECES)]
    inv_slope = sc_ref[base + 1 + SLOPE_PIECES]

    @pl.when(qi == 0)
    def _():
        li = lax.broadcasted_iota(jnp.int32, (LANES, LANES), 0) // DIFF_D
        lj = lax.broadcasted_iota(jnp.int32, (LANES, LANES), 1) // DIFF_D
        half_ones = (li == lj).astype(bf)

        def tile_max(j, best):
            kf = k_ref[pl.ds(pl.multiple_of(j * tk, tk), tk), :].astype(jnp.float32)
            sq = sum(_dot(p, half_ones) for p in _bf16_split(kf * kf, HEAD_SUM_PIECES))
            return jnp.maximum(best, jnp.max(sq, axis=0, keepdims=True))

        best = lax.fori_loop(0, n_kv, tile_max, jnp.zeros((1, LANES), jnp.float32))
        kmax_ref[0] = jnp.max(best)

    qt = qt_ref[...]
    q_sq = jnp.square(qt.astype(jnp.float32))
    q_n2 = jnp.maximum(jnp.sum(q_sq[:DIFF_D], axis=0, keepdims=True),
                       jnp.sum(q_sq[DIFF_D:], axis=0, keepdims=True))
    bound = jnp.max(jnp.sqrt(q_n2 * kmax_ref[0]))
    reach = (2.0 * BOUND_SLACK * bound + UNDERFLOW_LOG2) * inv_slope
    width = jnp.int32(1)
    for d in range(1, n_kv):
        width = width + jnp.where(reach < float(d * tk), 0, 1).astype(jnp.int32)
    qrow = lax.broadcasted_iota(jnp.int32, (DIFF_HEAD, tq), 0)
    feat = jnp.zeros((DIFF_HEAD, tq), jnp.float32)
    for i, pc in enumerate(pieces):
        feat = jnp.where(qrow % DIFF_D == i, 2.0 * pc, feat)
        feat = jnp.where(qrow % DIFF_D == SLOPE_PIECES + i, pc, feat)
    feat = feat.astype(bf)
    own_rows = [qrow < DIFF_D, qrow >= DIFF_D]
    qts = [jnp.where(own, qt, feat) for own in own_rows]

    kl = lax.broadcasted_iota(jnp.int32, (tk, LANES), 1)
    kr = lax.broadcasted_iota(jnp.int32, (tk, LANES), 0)
    kfeat = jnp.where(kl % DIFF_D < SLOPE_PIECES, kr >> 1,
                      jnp.where(kl % DIFF_D < 2 * SLOPE_PIECES, kr & 1, 0)).astype(jnp.float32)
    kfeat_before = kfeat.astype(bf)
    kfeat_after = (-kfeat).astype(bf)
    kzero = jnp.zeros((tk, LANES), bf)
    own_lanes = [kl < DIFF_D, kl >= DIFF_D]

    qpos = qi * tq + lax.broadcasted_iota(jnp.int32, (1, tq), 1)
    q_bias = slope * qpos.astype(jnp.float32)

    def online(carry, s, t_q, vt):
        m, acc = carry
        m_new = jnp.maximum(m, jnp.max(s, axis=0, keepdims=True) + t_q)
        alpha = jnp.exp2(m - m_new)
        p = jnp.exp2(s - (m_new - t_q))
        return m_new, alpha * acc + _dot(vt, p.astype(bf))

    n_diag = max(tq // tk, 1)
    jd = (qi * tq) // tk
    lo = jnp.maximum(jd - width, 0)
    hi = jnp.minimum(jd + n_diag - 1 + width, n_kv - 1)
    n_off = hi - lo + 1 - n_diag

    def tile_of(n):
        j = lo + n
        return j + n_diag * (j >= jd).astype(jnp.int32)

    def scores(n, slot, mp):
        j = tile_of(n)
        start = pl.multiple_of(j * tk, tk)
        kb = k_ref[pl.ds(start, tk), :]
        kf = jnp.where(j < jd, kfeat_before, kfeat_after)
        s_ref[slot, mp] = _dot(jnp.where(own_lanes[mp], kb, kf), qts[mp])

    def consume(n, slot, mp, carry):
        j = tile_of(n)
        start = pl.multiple_of(j * tk, tk)
        vt = vt_ref[:, pl.ds(start, tk)]
        sign = jnp.where(j < jd, 1.0, -1.0)
        t_q = sign * (slope * (j * tk).astype(jnp.float32) - q_bias)
        return online(carry, s_ref[slot, mp], t_q, vt)

    def step(n_next, n_cur, slot_next, slot_cur, carries):
        out = []
        for mp in range(2):
            scores(n_next, slot_next, mp)
            out.append(consume(n_cur, slot_cur, mp, carries[mp]))
        return tuple(out)

    def diag_scores(d, slot):
        kb_d = k_ref[pl.ds(pl.multiple_of((jd + d) * tk, tk), tk), :]
        kpos = (jd + d) * tk + lax.broadcasted_iota(jnp.int32, (tk, tq), 0)
        bias = slope * jnp.abs(kpos - qpos).astype(jnp.float32)
        for mp, (own, q_m) in enumerate(zip(own_lanes, qts)):
            s_ref[slot, mp] = _dot(jnp.where(own, kb_d, kzero), q_m) - bias

    def diag_consume(d, slot, mp, carry):
        vt_d = vt_ref[:, pl.ds(pl.multiple_of((jd + d) * tk, tk), tk)]
        return online(carry, s_ref[slot, mp], 0.0, vt_d)

    init = (jnp.full((1, tq), -jnp.inf, jnp.float32), jnp.zeros((V_ROWS, tq), jnp.float32))
    carries = [init, init]
    diag_slots = [1] if n_diag == 1 else [0, 1]
    for d, slot in enumerate(diag_slots[:-1]):
        diag_scores(d, slot)
    diag_scores(n_diag - 1, 1)
    for d, slot in enumerate(diag_slots[:-1]):
        carries = [diag_consume(d, slot, mp, carries[mp]) for mp in range(2)]
    for mp in range(2):
        scores(jnp.int32(0), 0, mp)
        carries[mp] = diag_consume(n_diag - 1, 1, mp, carries[mp])

    def body(i, carries):
        n = 2 * i
        carries = step(n + 1, n, 1, 0, carries)
        return step(n + 2, n + 1, 0, 1, carries)

    n_pairs = (n_off - 1) // 2
    carries = lax.fori_loop(0, n_pairs, body, tuple(carries))
    last = n_off - 1

    def tail_two(carries):
        carries = step(last, last - 1, 1, 0, carries)
        return tuple(consume(last, 1, mp, carries[mp]) for mp in range(2))

    def tail_one(carries):
        return tuple(consume(last, 0, mp, carries[mp]) for mp in range(2))

    (_, acc0), (_, acc1) = lax.cond(last == 2 * n_pairs + 1, tail_two, tail_one, carries)
    o0 = acc0[:DIFF_HEAD] / acc0[DIFF_HEAD:DIFF_HEAD + 1]
    o1 = acc1[:DIFF_HEAD] / acc1[DIFF_HEAD:DIFF_HEAD + 1]
    o = (o0 - lam * o1).T
    o_ref[...] = (_rms(o, g_ref[...], SUBLN_EPS) * (1.0 - lam_init)).astype(o_ref.dtype)


def _bf16_pieces(x, n):
    out = []
    for _ in range(n):
        p = (x.view(np.uint32) & np.uint32(0xFFFF0000)).view(np.float32)
        out.append(p)
        x = (x - p).astype(np.float32)
    return out


def diff_attention(qt, k, vt, lam, subln_g, lam_init, tq=1024, tk=512):
    t = k.shape[0]
    assert tq // tk in (1, 2) and tq % tk == 0 and tk // 2 <= 256
    assert t // tk > tq // tk
    slopes = 2.0 ** (-ALIBI_MAX_EXP * np.arange(1, N_DIFF_HEADS + 1, dtype=np.float64) / N_DIFF_HEADS)
    slopes = (slopes * math.log2(math.e)).astype(np.float32)
    pieces = _bf16_pieces(slopes, SLOPE_PIECES)
    slope_used = sum(pieces)
    per_head = np.stack([slope_used] + pieces + [(1.0 / slope_used).astype(np.float32)], axis=1)
    lam = jnp.concatenate([lam, jnp.asarray(per_head.reshape(-1))])
    return pl.pallas_call(
        functools.partial(_diff_attn_kernel, tq=tq, tk=tk, lam_init=lam_init),
        grid=(N_DIFF_HEADS, t // tq),
        in_specs=[pl.BlockSpec(memory_space=pltpu.SMEM),
                  pl.BlockSpec((DIFF_HEAD, tq), lambda h, i: (h, i)),
                  pl.BlockSpec((t, DIFF_HEAD), lambda h, i: (0, h)),
                  pl.BlockSpec((V_ROWS, t), lambda h, i: (h, 0)),
                  pl.BlockSpec((1, DIFF_HEAD), lambda h, i: (0, 0))],
        out_specs=pl.BlockSpec((tq, DIFF_HEAD), lambda h, i: (i, h)),
        out_shape=jax.ShapeDtypeStruct((t, D_DIFF), jnp.bfloat16),
        scratch_shapes=[pltpu.VMEM((2, 2, tk, tq), jnp.float32),
                        pltpu.SMEM((1,), jnp.float32)],
        compiler_params=_cparams(2),
        name="diff_attn",
    )(lam, qt, k, vt, subln_g.reshape(1, DIFF_HEAD))


def _pad_cols(w, to):
    return jnp.pad(w, ((0, 0), (0, to - w.shape[1])))


def _layer(h, l, lam_init, x_norm, prm):
    bf = jnp.bfloat16
    f32 = jnp.float32
    f1 = swiglu_ffn(x_norm, prm["ffn1_w_gate"][l], prm["ffn1_w_up"][l],
                    prm["ffn1_w_down"][l].astype(bf), 2048, 256, 512, 512)
    h, xn = resid_norm(h, f1, prm["ffn1_post_g"][l], prm["mix_pre_g"][l], FFN_RESIDUAL, bf)

    w_in = prm["w_in"][l]
    c_rkv = 3 * D_RWKV
    c_w = c_rkv + DECAY_RANK
    c_a = c_w + ICLR_RANK
    c_g = c_a + GATE_RANK
    w_small = jnp.concatenate([_pad_cols(w_in[:, c_rkv:c_w], RANK_PAD),
                               _pad_cols(w_in[:, c_w:c_a], RANK_PAD),
                               w_in[:, c_a:c_g]], axis=1)
    w_rwkv = jnp.concatenate([w_in[:, :c_rkv], w_small], axis=1).astype(bf)
    proj = matmul(xn, w_rwkv, f32, 2048, 512)
    q_scale = DIFF_D ** -0.5 * math.log2(math.e)
    col_scale = jnp.concatenate([jnp.full((1, D_DIFF), q_scale, f32), jnp.ones((1, 2 * D_DIFF), f32)], axis=1)
    qkv = matmul_col_scaled(xn, w_in[:, c_g:].astype(bf), col_scale, bf, 2048, 512)

    def small_vec(a):
        return jnp.concatenate([_pad_cols(a[None, c_rkv:c_w], RANK_PAD),
                                _pad_cols(a[None, c_w:c_a], RANK_PAD),
                                a[None, c_a:c_g]], axis=1)

    def pad_rows(w):
        return jnp.pad(w, ((0, RANK_PAD - w.shape[0]), (0, 0))).astype(bf)

    mu_p, mu_n = prm["mu_prev"][l], prm["mu_next"][l]
    vecs = [prm[n][l].reshape(1, D_RWKV) for n in ("w0_f", "w0_b", "a0_f", "a0_b", "k_k", "k_a", "r_k")]
    mats = [pad_rows(prm["w2_f"][l]), pad_rows(prm["w2_b"][l]),
            pad_rows(prm["a2_f"][l]), pad_rows(prm["a2_b"][l]), prm["g2"][l].astype(bf)]
    prep = rwkv_prep(proj, [mu_p[None, :c_rkv], mu_n[None, :c_rkv], small_vec(mu_p), small_vec(mu_n)]
                     + vecs + mats)
    fwd, bwd, (v_r, gate, bonus) = prep[0:5], prep[5:10], prep[10:13]
    yf, yb = rwkv_scan(fwd, bwd, v_r)
    y_a = rwkv_post(yf, yb, bonus, gate, prm["gn_w"][l], prm["gn_b"][l])

    lam = (jnp.exp(jnp.sum(prm["lq1"][l] * prm["lk1"][l]))
           - jnp.exp(jnp.sum(prm["lq2"][l] * prm["lk2"][l])) + lam_init).reshape(1)
    t = qkv.shape[0]
    vt = qkv[:, 2 * D_DIFF:].T.reshape(N_DIFF_HEADS, DIFF_HEAD, t)
    vt = jnp.concatenate([vt, jnp.ones((N_DIFF_HEADS, ONES_ROWS, t), bf)], axis=1).reshape(N_DIFF_HEADS * V_ROWS, t)
    y_b = diff_attention(qkv[:, :D_DIFF].T, qkv[:, D_DIFF:2 * D_DIFF], vt, lam, prm["subln_g"][l], lam_init)

    w_out = prm["w_out"][l]
    mix = matmul_two(y_a, y_b, w_out[:D_RWKV].astype(bf), w_out[D_RWKV:].astype(bf), f32, 2048, 512)
    h, xn = resid_norm(h, mix, prm["mix_post_g"][l], prm["ffn2_pre_g"][l], 1.0, bf)

    f2 = swiglu_ffn(xn, prm["ffn2_w_gate"][l], prm["ffn2_w_up"][l],
                    prm["ffn2_w_down"][l].astype(bf), 2048, 256, 512, 512)
    return resid_norm_out(h, f2, prm["ffn2_post_g"][l], prm["final_g"][l], FFN_RESIDUAL)


def kernel(x, ffn1_pre_g, ffn1_w_gate, ffn1_w_up, ffn1_w_down, ffn1_post_g, mix_pre_g, w_in, mu_prev, mu_next, w0_f, w2_f, w0_b, w2_b, a0_f, a2_f, a0_b, a2_b, g2, k_k, k_a, r_k, gn_w, gn_b, lq1, lk1, lq2, lk2, subln_g, w_out, mix_post_g, ffn2_pre_g, ffn2_w_gate, ffn2_w_up, ffn2_w_down, ffn2_post_g, final_g):
    prm = dict(locals())
    bsz, t, d = x.shape
    assert bsz == 1
    depth = ffn1_pre_g.shape[0]
    h = x.reshape(t, d)
    for l in range(depth):
        lam_init = 0.8 - 0.6 * math.exp(-0.3 * l)
        x_norm = rms_norm_cast(h, ffn1_pre_g[l], jnp.bfloat16)
        h = _layer(h, l, lam_init, x_norm, prm)
    return h.reshape(bsz, t, d)
```
